```python
import math
import jax, jax.numpy as jnp
from jax import lax
import numpy as np

D_MODEL = 2048
BATCH = 2
SEQ = 4096
DEPTH = 2
DEC_BATCH = 16
DEC_SEQ = 16
PAST_LEN = 1024

CHUNK = 64
D_MIX = D_MODEL
D_SSM = D_MIX // 4
SSM_GROUP = 16
N_SSM_GROUPS = D_SSM // SSM_GROUP
SSM_STATE = 64
D_ATT = D_MIX // 2
ATT_HEAD_DIM = 64
N_ATT_HEADS = D_ATT // ATT_HEAD_DIM
LEFT_CHUNKS = 8
BAND_CHUNKS = LEFT_CHUNKS + 1
ATT_REACH = LEFT_CHUNKS * CHUNK
REL_CLIP = 128
ATT_SCALE = ATT_HEAD_DIM ** -0.5
NEG_INF = -1e30
D_RWKV = D_MIX - D_SSM - D_ATT
RWKV_HEAD_DIM = 64
N_RWKV_HEADS = D_RWKV // RWKV_HEAD_DIM
RWKV_LORA = 64
GN_EPS = 64e-5
LN_EPS = 1e-5
SPLIT_SIZES = (D_SSM, D_SSM, D_ATT, D_ATT, D_ATT, D_ATT, D_RWKV, D_RWKV, D_RWKV, D_RWKV, D_RWKV)
SPLIT_POINTS = tuple(int(s) for s in np.cumsum(SPLIT_SIZES)[:-1])
D_IN = sum(SPLIT_SIZES)
ALPHA = (2.0 * DEPTH) ** 0.25
BETA = (8.0 * DEPTH) ** -0.25

kernel_name = 'hybrid_streaming_encoder_step'


def layer_norm(x, g, b):
    xf = x.astype(jnp.float32)
    mu = jnp.mean(xf, axis=-1, keepdims=True)
    var = jnp.mean(jnp.square(xf - mu), axis=-1, keepdims=True)
    y = (xf - mu) * lax.rsqrt(var + LN_EPS) * g.astype(jnp.float32) + b.astype(jnp.float32)
    return y.astype(x.dtype)


def complex_affine_combine(e1, e2):
    a1r, a1i, b1r, b1i = e1
    a2r, a2i, b2r, b2i = e2
    return (a2r * a1r - a2i * a1i, a2r * a1i + a2i * a1r,
            a2r * b1r - a2i * b1i + b2r, a2r * b1i + a2i * b1r + b2i)


def s5_mixer(u, h0_re, h0_im, lam_re, lam_im, log_dt, b_re, b_im, c_re, c_im, d_skip, w_glu, b_glu):
    n_b, n_l, _ = u.shape
    f32 = lambda t: t.astype(jnp.float32)
    uf = f32(u)
    ug = uf.reshape(n_b, n_l, N_SSM_GROUPS, SSM_GROUP)
    lr, li = f32(lam_re), f32(lam_im)
    dt = jnp.exp(f32(log_dt))[:, None]
    e = jnp.exp(lr * dt)
    ab_re, ab_im = e * jnp.cos(li * dt), e * jnp.sin(li * dt)
    den = lr * lr + li * li
    nr, ni = ab_re - 1.0, ab_im
    q_re = (nr * lr + ni * li) / den
    q_im = (ni * lr - nr * li) / den
    br, bi = f32(b_re), f32(b_im)
    bb_re = q_re[..., None] * br - q_im[..., None] * bi
    bb_im = q_re[..., None] * bi + q_im[..., None] * br
    bu_re = jnp.einsum('gpc,blgc->blgp', bb_re, ug)
    bu_im = jnp.einsum('gpc,blgc->blgp', bb_im, ug)
    if h0_re is not None:
        h0r, h0i = f32(h0_re), f32(h0_im)
        bu_re = bu_re.at[:, 0].add(ab_re * h0r - ab_im * h0i)
        bu_im = bu_im.at[:, 0].add(ab_re * h0i + ab_im * h0r)
    a_re = jnp.broadcast_to(ab_re, bu_re.shape)
    a_im = jnp.broadcast_to(ab_im, bu_im.shape)
    _, _, h_re, h_im = lax.associative_scan(complex_affine_combine, (a_re, a_im, bu_re, bu_im), axis=1)
    y = jnp.einsum('gcp,blgp->blgc', f32(c_re), h_re) - jnp.einsum('gcp,blgp->blgc', f32(c_im), h_im)
    y = y.reshape(n_b, n_l, D_SSM) + f32(d_skip) * uf
    z = jax.nn.gelu(y)
    out = z * jax.nn.sigmoid(z @ f32(w_glu) + f32(b_glu))
    return out.astype(u.dtype), h_re[:, -1], h_im[:, -1]


def rel_bias_lookup(table, rel):
    return table[:, jnp.clip(rel, -REL_CLIP, REL_CLIP) + REL_CLIP].astype(jnp.float32)


def band_attention_prompt(q, k, v, rel_table):
    n_b, n_l, n_h, n_d = q.shape
    n_c = n_l // CHUNK
    qc = q.reshape(n_b, n_c, CHUNK, n_h, n_d)
    pad = ((0, 0), (ATT_REACH, 0), (0, 0), (0, 0))
    kp = jnp.pad(k, pad).reshape(n_b, n_c + LEFT_CHUNKS, CHUNK, n_h, n_d)
    vp = jnp.pad(v, pad).reshape(n_b, n_c + LEFT_CHUNKS, CHUNK, n_h, n_d)
    kb = jnp.concatenate([kp[:, i:i + n_c] for i in range(BAND_CHUNKS)], axis=2)
    vb = jnp.concatenate([vp[:, i:i + n_c] for i in range(BAND_CHUNKS)], axis=2)
    qi = jnp.arange(CHUNK)
    kj = jnp.arange(BAND_CHUNKS * CHUNK)
    bias = rel_bias_lookup(rel_table, ATT_REACH + qi[:, None] - kj[None, :])
    valid = (jnp.arange(n_c)[:, None] * CHUNK - ATT_REACH + kj[None, :]) >= 0
    s = jnp.einsum('bcqhd,bckhd->bchqk', qc, kb).astype(jnp.float32) * ATT_SCALE + bias
    s = jnp.where(valid[None, :, None, None, :], s, NEG_INF)
    p = jax.nn.softmax(s, axis=-1).astype(v.dtype)
    o = jnp.einsum('bchqk,bckhd->bcqhd', p, vb)
    return o.reshape(n_b, n_l, n_h * n_d)


def band_attention_step(q, k_new, v_new, k_cache, v_cache, rel_table):
    n_b, n_s, n_h, n_d = q.shape
    n_w = k_cache.shape[1]
    kk = jnp.concatenate([k_cache.astype(k_new.dtype), k_new], axis=1)
    vv = jnp.concatenate([v_cache.astype(v_new.dtype), v_new], axis=1)
    rel = (n_w + jnp.arange(n_s))[:, None] - jnp.arange(n_w + n_s)[None, :]
    bias = rel_bias_lookup(rel_table, rel)
    s = jnp.einsum('bqhd,bkhd->bhqk', q, kk).astype(jnp.float32) * ATT_SCALE + bias[None]
    p = jax.nn.softmax(s, axis=-1).astype(vv.dtype)
    o = jnp.einsum('bhqk,bkhd->bqhd', p, vv)
    return o.reshape(n_b, n_s, n_h * n_d)


def rwkv7_mixer(r_p, k_p, v_p, u_p, wkv0, shift0, mu, w0, w1, w2, a0, a1, a2, k_k, k_a, r_k, lnx_g, lnx_b):
    n_b, n_l, _ = r_p.shape
    cur = jnp.stack([r_p, k_p, v_p, u_p], axis=2)
    if shift0 is None:
        first = jnp.zeros_like(cur[:, :1])
    else:
        first = shift0.reshape(n_b, 1, 4, D_RWKV).astype(cur.dtype)
    delta = jnp.concatenate([first, cur[:, :-1]], axis=1) - cur
    r = r_p + delta[:, :, 0] * mu[0]
    k = k_p + delta[:, :, 1] * mu[1]
    v = v_p + delta[:, :, 2] * mu[2]
    xw = u_p + delta[:, :, 3] * mu[3]
    xa = u_p + delta[:, :, 3] * mu[4]
    w = -jax.nn.softplus(-(w0 + jnp.tanh(xw @ w1) @ w2)) - 0.5
    decay = jnp.exp(-jnp.exp(w.astype(jnp.float32)))
    a = jax.nn.sigmoid(a0 + (xa @ a1) @ a2)
    heads = lambda t: t.reshape(n_b, n_l, N_RWKV_HEADS, RWKV_HEAD_DIM).astype(jnp.float32)
    kk = heads(k * k_k)
    kk = kk * lax.rsqrt(jnp.maximum(jnp.sum(kk * kk, axis=-1, keepdims=True), 1e-24))
    k = k * (1.0 + (a - 1.0) * k_a)
    rh, kh, vh, wh, ah = heads(r), heads(k), heads(v), heads(decay), heads(a)
    bh = kk * ah
    if wkv0 is None:
        s0 = jnp.zeros((n_b, N_RWKV_HEADS, RWKV_HEAD_DIM, RWKV_HEAD_DIM), jnp.float32)
    else:
        s0 = wkv0.astype(jnp.float32)

    def step(S, inp):
        r_t, w_t, k_t, v_t, kk_t, b_t = inp
        sa = jnp.einsum('bhij,bhj->bhi', S, -kk_t)
        S = S * w_t[:, :, None, :] + sa[..., None] * b_t[:, :, None, :] + v_t[..., None] * k_t[:, :, None, :]
        return S, jnp.einsum('bhij,bhj->bhi', S, r_t)

    tm = lambda t: jnp.swapaxes(t, 0, 1)
    s_last, ys = lax.scan(step, s0, (tm(rh), tm(wh), tm(kh), tm(vh), tm(kk), tm(bh)))
    y = tm(ys)
    mean = jnp.mean(y, axis=-1, keepdims=True)
    var = jnp.mean(jnp.square(y - mean), axis=-1, keepdims=True)
    y = (y - mean) * lax.rsqrt(var + GN_EPS) * lnx_g.astype(jnp.float32).reshape(N_RWKV_HEADS, RWKV_HEAD_DIM) \
        + lnx_b.astype(jnp.float32).reshape(N_RWKV_HEADS, RWKV_HEAD_DIM)
    y = y + jnp.sum(rh * kh * r_k.astype(jnp.float32), axis=-1, keepdims=True) * vh
    return y.reshape(n_b, n_l, D_RWKV).astype(r_p.dtype), s_last, cur[:, -1].reshape(n_b, 4 * D_RWKV)


def hybrid_layer(x, st, lw):
    n_b, n_l, _ = x.shape
    proj = x @ lw['w_in']
    u_s, g_s, q, k, v, g_a, r_c, k_c, v_c, u_c, g_c = jnp.split(proj, SPLIT_POINTS, axis=-1)
    if st is None:
        k_cache = v_cache = h0_re = h0_im = wkv0 = shift0 = None
    else:
        k_cache, v_cache, h0_re, h0_im, wkv0, shift0 = st
    y_s, h_re, h_im = s5_mixer(u_s, h0_re, h0_im, lw['ssm_lam_re'], lw['ssm_lam_im'], lw['ssm_log_dt'],
                               lw['ssm_b_re'], lw['ssm_b_im'], lw['ssm_c_re'], lw['ssm_c_im'],
                               lw['ssm_d'], lw['ssm_w_glu'], lw['ssm_b_glu'])
    hd = lambda t: t.reshape(n_b, n_l, N_ATT_HEADS, ATT_HEAD_DIM)
    qh, kh, vh = hd(q), hd(k), hd(v)
    if st is None:
        y_a = band_attention_prompt(qh, kh, vh, lw['att_rel_bias'])
        n_keep = min(ATT_REACH, n_l)
        k_rows, v_rows = kh[:, n_l - n_keep:], vh[:, n_l - n_keep:]
    else:
        y_a = band_attention_step(qh, kh, vh, k_cache, v_cache, lw['att_rel_bias'])
        k_rows, v_rows = kh, vh
    y_c, wkv, shift = rwkv7_mixer(r_c, k_c, v_c, u_c, wkv0, shift0, lw['rwkv_mu'], lw['rwkv_w0'],
                                  lw['rwkv_w1'], lw['rwkv_w2'], lw['rwkv_a0'], lw['rwkv_a1'], lw['rwkv_a2'],
                                  lw['rwkv_k_k'], lw['rwkv_k_a'], lw['rwkv_r_k'], lw['rwkv_lnx_g'], lw['rwkv_lnx_b'])
    mixed = jnp.concatenate([y_s * jax.nn.silu(g_s), y_a * jax.nn.silu(g_a), y_c * jax.nn.silu(g_c)], axis=-1)
    out = mixed @ lw['w_out']
    y = layer_norm(ALPHA * x + out, lw['ln_g'], lw['ln_b'])
    return y, (k_rows, v_rows, h_re, h_im, wkv, shift)


def setup_inputs(seed: int = 0) -> dict:
    key = jax.random.key(seed)
    ks = iter(jax.random.split(key, 48))
    nrm = lambda shape, scale: scale * jax.random.normal(next(ks), shape, jnp.float32)
    att_rows = min(ATT_REACH, PAST_LEN)
    G, P, L_ = N_SSM_GROUPS, SSM_STATE, DEPTH
    inp = {}
    inp['x_prompt'] = nrm((BATCH, SEQ, D_MODEL), 1.0)
    inp['x_sample'] = nrm((DEC_BATCH, DEC_SEQ, D_MODEL), 1.0)
    inp['cache_att_k'] = nrm((L_, DEC_BATCH, att_rows, N_ATT_HEADS, ATT_HEAD_DIM), 1.0)
    inp['cache_att_v'] = nrm((L_, DEC_BATCH, att_rows, N_ATT_HEADS, ATT_HEAD_DIM), 1.0)
    inp['state_ssm_re'] = nrm((L_, DEC_BATCH, G, P), 0.5)
    inp['state_ssm_im'] = nrm((L_, DEC_BATCH, G, P), 0.5)
    inp['state_rwkv'] = nrm((L_, DEC_BATCH, N_RWKV_HEADS, RWKV_HEAD_DIM, RWKV_HEAD_DIM), 0.5)
    inp['state_rwkv_shift'] = nrm((L_, DEC_BATCH, 4 * D_RWKV), 1.0)
    inp['w_in'] = nrm((L_, D_MODEL, D_IN), D_MODEL ** -0.5)
    inp['ssm_lam_re'] = -0.5 * jnp.exp(nrm((L_, G, P), 0.02))
    inp['ssm_lam_im'] = jnp.pi * jnp.arange(P, dtype=jnp.float32) + nrm((L_, G, P), 0.01)
    inp['ssm_log_dt'] = jax.random.uniform(next(ks), (L_, G), jnp.float32, math.log(1e-3), math.log(1e-1))
    inp['ssm_b_re'] = nrm((L_, G, P, SSM_GROUP), (2.0 * SSM_GROUP) ** -0.5)
    inp['ssm_b_im'] = nrm((L_, G, P, SSM_GROUP), (2.0 * SSM_GROUP) ** -0.5)
    inp['ssm_c_re'] = nrm((L_, G, SSM_GROUP, P), (2.0 * P) ** -0.5)
    inp['ssm_c_im'] = nrm((L_, G, SSM_GROUP, P), (2.0 * P) ** -0.5)
    inp['ssm_d'] = nrm((L_, D_SSM), 1.0)
    inp['ssm_w_glu'] = nrm((L_, D_SSM, D_SSM), D_SSM ** -0.5)
    inp['ssm_b_glu'] = nrm((L_, D_SSM), 0.02)
    inp['att_rel_bias'] = nrm((L_, N_ATT_HEADS, 2 * REL_CLIP + 1), 0.5)
    inp['rwkv_mu'] = jax.random.uniform(next(ks), (L_, 5, D_RWKV), jnp.float32)
    inp['rwkv_w0'] = jnp.linspace(-6.5, -1.5, D_RWKV, dtype=jnp.float32) + nrm((L_, D_RWKV), 0.1)
    inp['rwkv_w1'] = nrm((L_, D_RWKV, RWKV_LORA), D_RWKV ** -0.5)
    inp['rwkv_w2'] = nrm((L_, RWKV_LORA, D_RWKV), 0.1 * RWKV_LORA ** -0.5)
    inp['rwkv_a0'] = nrm((L_, D_RWKV), 0.1)
    inp['rwkv_a1'] = nrm((L_, D_RWKV, RWKV_LORA), D_RWKV ** -0.5)
    inp['rwkv_a2'] = nrm((L_, RWKV_LORA, D_RWKV), 0.1 * RWKV_LORA ** -0.5)
    inp['rwkv_k_k'] = 0.85 + nrm((L_, D_RWKV), 0.05)
    inp['rwkv_k_a'] = 1.0 + nrm((L_, D_RWKV), 0.05)
    inp['rwkv_r_k'] = nrm((L_, N_RWKV_HEADS, RWKV_HEAD_DIM), 0.1)
    inp['rwkv_lnx_g'] = 1.0 + nrm((L_, D_RWKV), 0.02)
    inp['rwkv_lnx_b'] = nrm((L_, D_RWKV), 0.02)
    inp['w_out'] = nrm((L_, D_MIX, D_MODEL), BETA * D_MIX ** -0.5)
    inp['ln_g'] = 1.0 + nrm((L_, D_MODEL), 0.02)
    inp['ln_b'] = nrm((L_, D_MODEL), 0.02)
    return inp


def reference(x_prompt, x_sample, cache_att_k, cache_att_v, state_ssm_re, state_ssm_im, state_rwkv,
              state_rwkv_shift, w_in, ssm_lam_re, ssm_lam_im, ssm_log_dt, ssm_b_re, ssm_b_im, ssm_c_re,
              ssm_c_im, ssm_d, ssm_w_glu, ssm_b_glu, att_rel_bias, rwkv_mu, rwkv_w0, rwkv_w1, rwkv_w2,
              rwkv_a0, rwkv_a1, rwkv_a2, rwkv_k_k, rwkv_k_a, rwkv_r_k, rwkv_lnx_g, rwkv_lnx_b, w_out,
              ln_g, ln_b):
    y_p = x_prompt
    y_s = x_sample
    p_st = []
    s_st = []
    for l in range(DEPTH):
        lw = {'w_in': w_in[l], 'ssm_lam_re': ssm_lam_re[l], 'ssm_lam_im': ssm_lam_im[l],
              'ssm_log_dt': ssm_log_dt[l], 'ssm_b_re': ssm_b_re[l], 'ssm_b_im': ssm_b_im[l],
              'ssm_c_re': ssm_c_re[l], 'ssm_c_im': ssm_c_im[l], 'ssm_d': ssm_d[l],
              'ssm_w_glu': ssm_w_glu[l], 'ssm_b_glu': ssm_b_glu[l], 'att_rel_bias': att_rel_bias[l],
              'rwkv_mu': rwkv_mu[l], 'rwkv_w0': rwkv_w0[l], 'rwkv_w1': rwkv_w1[l], 'rwkv_w2': rwkv_w2[l],
              'rwkv_a0': rwkv_a0[l], 'rwkv_a1': rwkv_a1[l], 'rwkv_a2': rwkv_a2[l],
              'rwkv_k_k': rwkv_k_k[l], 'rwkv_k_a': rwkv_k_a[l], 'rwkv_r_k': rwkv_r_k[l],
              'rwkv_lnx_g': rwkv_lnx_g[l], 'rwkv_lnx_b': rwkv_lnx_b[l], 'w_out': w_out[l],
              'ln_g': ln_g[l], 'ln_b': ln_b[l]}
        y_p, st_p = hybrid_layer(y_p, None, lw)
        y_s, st_s = hybrid_layer(y_s, (cache_att_k[l], cache_att_v[l], state_ssm_re[l], state_ssm_im[l],
                                       state_rwkv[l], state_rwkv_shift[l]), lw)
        p_st.append(st_p)
        s_st.append(st_s)

    def stacked(states, i):
        return jnp.stack([st[i] for st in states], axis=0)

    return (y_p, y_s,
            stacked(p_st, 0), stacked(p_st, 1), stacked(p_st, 2), stacked(p_st, 3), stacked(p_st, 4), stacked(p_st, 5),
            stacked(s_st, 0), stacked(s_st, 1), stacked(s_st, 2), stacked(s_st, 3), stacked(s_st, 4), stacked(s_st, 5))
```

```python
import functools
import math

import jax
import jax.numpy as jnp
import numpy as np
from jax import lax
from jax.experimental import pallas as pl
from jax.experimental.pallas import tpu as pltpu

F32 = jnp.float32
BF16 = jnp.bfloat16

LANES = 128
SUBLANES = 8
VMEM_LIMIT = 56 * 1024 * 1024

CHUNK = 64
LEFT_CHUNKS = 8
ATT_REACH = LEFT_CHUNKS * CHUNK
REL_CLIP = 128
HEAD_DIM = 64
SSM_GROUP = 16
SSM_STATE = 64
RWKV_LORA = 64
NEG_INF = -1e30
GN_EPS = 64e-5
LN_EPS = 1e-5

Q_BLOCK = 4 * CHUNK
K_WINDOW = Q_BLOCK + ATT_REACH
S5_SEGMENTS = SUBLANES
S5_BLOCK = 512


def _cparams(*sem):
    return pltpu.CompilerParams(dimension_semantics=sem, vmem_limit_bytes=VMEM_LIMIT)


def _block_ones(n):
    r = lax.broadcasted_iota(jnp.int32, (n, n), 0) // HEAD_DIM
    c = lax.broadcasted_iota(jnp.int32, (n, n), 1) // HEAD_DIM
    return (r == c).astype(BF16)


def _head_sum(x, ones):
    hi = x.astype(BF16)
    lo = (x - hi.astype(F32)).astype(BF16)
    return jnp.dot(hi, ones, preferred_element_type=F32) + jnp.dot(lo, ones, preferred_element_type=F32)


def _matmul_kernel(x_ref, w_ref, o_ref):
    o_ref[...] = jnp.dot(x_ref[...], w_ref[...], preferred_element_type=F32)


def _in_proj(x, w):
    m, k = x.shape
    n = w.shape[1]
    tm = min(m, 1024)
    tn = 768
    assert m % tm == 0 and n % tn == 0
    return pl.pallas_call(
        _matmul_kernel,
        grid=(n // tn, m // tm),
        in_specs=[pl.BlockSpec((tm, k), lambda j, i: (i, 0)),
                  pl.BlockSpec((k, tn), lambda j, i: (0, j))],
        out_specs=pl.BlockSpec((tm, tn), lambda j, i: (i, j)),
        out_shape=jax.ShapeDtypeStruct((m, n), F32),
        compiler_params=_cparams("parallel", "parallel"),
        name="in_proj",
    )(x, w)


def _s5_kernel(u_ref, wbr_ref, wbi_ref, ar_ref, ai_ref, pr_ref, pi_ref, h0r_ref, h0i_ref,
               wcr_ref, wci_ref, d_ref, y_ref, hr_out, hi_out, *scratch, seg):
    nlb = len(scratch) // 2
    sr, si = scratch[:nlb], scratch[nlb:]
    lb = lambda j: slice(j * LANES, (j + 1) * LANES)
    u = u_ref[0]
    ub = u.astype(BF16)
    for j in range(nlb):
        sr[j][...] = jnp.dot(ub, wbr_ref[0, :, lb(j)], preferred_element_type=F32)
        si[j][...] = jnp.dot(ub, wbi_ref[0, :, lb(j)], preferred_element_type=F32)
    bcast = lambda ref, j: jnp.broadcast_to(ref[0, :, lb(j)], (S5_SEGMENTS, LANES))
    ar = [bcast(ar_ref, j) for j in range(nlb)]
    ai = [bcast(ai_ref, j) for j in range(nlb)]
    rows = lambda i: pl.ds(i, S5_SEGMENTS, stride=seg)

    def advance(i, hr, hi):
        nr = [ar[j] * hr[j] - ai[j] * hi[j] + sr[j][rows(i), :] for j in range(nlb)]
        ni = [ar[j] * hi[j] + ai[j] * hr[j] + si[j][rows(i), :] for j in range(nlb)]
        return nr, ni

    def body1(i, c):
        nr, ni = advance(i, c[:nlb], c[nlb:])
        return tuple(nr + ni)

    zero = jnp.zeros((S5_SEGMENTS, LANES), F32)
    ends = lax.fori_loop(0, seg, body1, (zero,) * (2 * nlb))

    starts = []
    for j in range(nlb):
        er, ei = ends[j], ends[nlb + j]
        pr, pi = pr_ref[0, :, lb(j)], pi_ref[0, :, lb(j)]
        cr, ci = h0r_ref[0, 0, :, lb(j)], h0i_ref[0, 0, :, lb(j)]
        start_r, start_i = [cr], [ci]
        for s in range(S5_SEGMENTS - 1):
            cr, ci = (pr * cr - pi * ci + er[s:s + 1, :], pr * ci + pi * cr + ei[s:s + 1, :])
            start_r.append(cr)
            start_i.append(ci)
        starts.append((jnp.concatenate(start_r, axis=0), jnp.concatenate(start_i, axis=0)))

    def body2(i, c):
        nr, ni = advance(i, c[:nlb], c[nlb:])
        for j in range(nlb):
            sr[j][rows(i), :] = nr[j]
            si[j][rows(i), :] = ni[j]
        return tuple(nr + ni)

    last = lax.fori_loop(0, seg, body2, tuple(s[0] for s in starts) + tuple(s[1] for s in starts))
    y = d_ref[...] * u
    for j in range(nlb):
        hr_out[0, 0, :, lb(j)] = last[j][S5_SEGMENTS - 1:, :]
        hi_out[0, 0, :, lb(j)] = last[nlb + j][S5_SEGMENTS - 1:, :]
        y += (jnp.dot(sr[j][...].astype(BF16), wcr_ref[0, lb(j), :], preferred_element_type=F32)
              - jnp.dot(si[j][...].astype(BF16), wci_ref[0, lb(j), :], preferred_element_type=F32))
    y_ref[0] = y


def _s5(proj, prm, h0r, h0i):
    n_b, n_l, _ = proj.shape
    nblk = prm["wbr"].shape[0]
    cin = prm["wbr"].shape[1]
    seg = n_l // S5_SEGMENTS
    assert seg * S5_SEGMENTS == n_l and cin == LANES
    wspec = lambda shape: pl.BlockSpec((1,) + shape, lambda b, s: (s, 0, 0))
    st_spec = pl.BlockSpec((1, 1, 1, S5_BLOCK), lambda b, s: (b, s, 0, 0))
    return pl.pallas_call(
        functools.partial(_s5_kernel, seg=seg),
        grid=(n_b, nblk),
        in_specs=[pl.BlockSpec((1, n_l, cin), lambda b, s: (b, 0, s)),
                  wspec((cin, S5_BLOCK)), wspec((cin, S5_BLOCK)),
                  wspec((1, S5_BLOCK)), wspec((1, S5_BLOCK)), wspec((1, S5_BLOCK)), wspec((1, S5_BLOCK)),
                  st_spec, st_spec,
                  wspec((S5_BLOCK, cin)), wspec((S5_BLOCK, cin)),
                  pl.BlockSpec((1, cin), lambda b, s: (0, s))],
        out_specs=[pl.BlockSpec((1, n_l, cin), lambda b, s: (b, 0, s)), st_spec, st_spec],
        out_shape=[jax.ShapeDtypeStruct((n_b, n_l, nblk * cin), F32),
                   jax.ShapeDtypeStruct((n_b, nblk, 1, S5_BLOCK), F32),
                   jax.ShapeDtypeStruct((n_b, nblk, 1, S5_BLOCK), F32)],
        scratch_shapes=[pltpu.VMEM((n_l, LANES), F32)] * (2 * S5_BLOCK // LANES),
        compiler_params=_cparams("parallel", "parallel"),
        name="s5_scan",
    )(proj, prm["wbr"], prm["wbi"], prm["ar"], prm["ai"], prm["pr"], prm["pi"], h0r, h0i,
      prm["wcr"], prm["wci"], prm["d"])


def _s5_params(lam_re, lam_im, log_dt, b_re, b_im, c_re, c_im, d_skip, seg_lens):
    n_g, n_p = lam_re.shape
    dt = jnp.exp(log_dt)[:, None]
    e = jnp.exp(lam_re * dt)
    ab_re, ab_im = e * jnp.cos(lam_im * dt), e * jnp.sin(lam_im * dt)
    den = lam_re * lam_re + lam_im * lam_im
    nr, ni = ab_re - 1.0, ab_im
    q_re = (nr * lam_re + ni * lam_im) / den
    q_im = (ni * lam_re - nr * lam_im) / den
    bb_re = q_re[..., None] * b_re - q_im[..., None] * b_im
    bb_im = q_re[..., None] * b_im + q_im[..., None] * b_re
    gpb = S5_BLOCK // n_p
    nblk = n_g // gpb
    eye = jnp.eye(gpb, dtype=F32)

    def pack_b(t):
        t = t.reshape(nblk, gpb, n_p, SSM_GROUP)
        return jnp.einsum("sgpc,gh->sgchp", t, eye).reshape(nblk, gpb * SSM_GROUP, gpb * n_p).astype(BF16)

    def pack_c(t):
        t = t.reshape(nblk, gpb, SSM_GROUP, n_p)
        return jnp.einsum("sgcp,gh->shpgc", t, eye).reshape(nblk, gpb * n_p, gpb * SSM_GROUP).astype(BF16)

    flat = lambda t: t.reshape(nblk, 1, gpb * n_p)
    out = {"wbr": pack_b(bb_re), "wbi": pack_b(bb_im), "wcr": pack_c(c_re), "wci": pack_c(c_im),
           "ar": flat(ab_re), "ai": flat(ab_im), "d": d_skip.reshape(1, -1), "pows": {}}
    for seg in seg_lens:
        assert seg & (seg - 1) == 0
        pr, pi = ab_re, ab_im
        for _ in range(int(math.log2(seg))):
            pr, pi = pr * pr - pi * pi, 2.0 * pr * pi
        out["pows"][seg] = (flat(pr), flat(pi))
    return out


def _att_prompt_kernel(q_ref, k_ref, v_ref, bias_ref, o_ref, kpad, vpad, *, n_l, scale):
    zpad = jnp.zeros((ATT_REACH, LANES), BF16)
    kpad[0:ATT_REACH, :] = zpad
    vpad[0:ATT_REACH, :] = zpad
    kpad[ATT_REACH:, :] = k_ref[0].astype(BF16)
    vpad[ATT_REACH:, :] = v_ref[0].astype(BF16)
    lane = lax.broadcasted_iota(jnp.int32, (1, LANES), 1)
    first = lane < HEAD_DIM
    col = lax.broadcasted_iota(jnp.int32, (1, K_WINDOW), 1)

    def body(qb, carry):
        r0 = pl.multiple_of(qb * Q_BLOCK, Q_BLOCK)
        q = q_ref[0, pl.ds(r0, Q_BLOCK), :] * scale
        kw = kpad[pl.ds(r0, K_WINDOW), :]
        vw = vpad[pl.ds(r0, K_WINDOW), :]
        before_start = (col + r0) < ATT_REACH
        outs = []
        for h in range(2):
            sel = first if h == 0 else jnp.logical_not(first)
            qh = jnp.where(sel, q, 0.0).astype(BF16)
            s = lax.dot_general(qh, kw, (((1,), (1,)), ((), ())), preferred_element_type=F32)
            s = jnp.where(before_start, NEG_INF, s + bias_ref[0, h])
            m = jnp.max(s, axis=-1, keepdims=True)
            p = jnp.exp(s - m)
            l = jnp.sum(p, axis=-1, keepdims=True)
            outs.append(jnp.dot(p.astype(BF16), vw, preferred_element_type=F32) / l)
        o_ref[0, pl.ds(r0, Q_BLOCK), :] = jnp.where(first, outs[0], outs[1])
        return carry

    lax.fori_loop(0, n_l // Q_BLOCK, body, 0)


def _att_prompt(proj, bias, q_col, k_col, v_col, n_pairs):
    n_b, n_l, _ = proj.shape
    assert n_l % Q_BLOCK == 0
    spec = lambda c0: pl.BlockSpec((1, n_l, LANES), lambda b, p: (b, 0, c0 + p))
    return pl.pallas_call(
        functools.partial(_att_prompt_kernel, n_l=n_l, scale=HEAD_DIM ** -0.5),
        grid=(n_b, n_pairs),
        in_specs=[spec(q_col), spec(k_col), spec(v_col),
                  pl.BlockSpec((1, 2, Q_BLOCK, K_WINDOW), lambda b, p: (p, 0, 0, 0))],
        out_specs=pl.BlockSpec((1, n_l, LANES), lambda b, p: (b, 0, p)),
        out_shape=jax.ShapeDtypeStruct((n_b, n_l, n_pairs * LANES), F32),
        scratch_shapes=[pltpu.VMEM((n_l + ATT_REACH, LANES), BF16), pltpu.VMEM((n_l + ATT_REACH, LANES), BF16)],
        compiler_params=_cparams("parallel", "parallel"),
        name="att_prompt",
    )(proj, proj, proj, bias)


def _prompt_bias(table):
    n_h = table.shape[0]
    qc, qi = np.arange(Q_BLOCK) // CHUNK, np.arange(Q_BLOCK) % CHUNK
    kc, ki = np.arange(K_WINDOW) // CHUNK, np.arange(K_WINDOW) % CHUNK
    dj = kc[None, :] - qc[:, None]
    rel = ATT_REACH + qi[:, None] - (dj * CHUNK + ki[None, :])
    idx = np.clip(rel, -REL_CLIP, REL_CLIP) + REL_CLIP
    inband = (dj >= 0) & (dj <= LEFT_CHUNKS)
    bias = jnp.where(inband[None], table[:, idx].astype(F32), NEG_INF)
    return bias.reshape(n_h // 2, 2, Q_BLOCK, K_WINDOW)


def _att_step_kernel(q_ref, kn_ref, vn_ref, kc_ref, vc_ref, bc_ref, bn_ref, o_ref, *, n_h, scale):
    n_s = q_ref.shape[1]
    d = q_ref.shape[2]
    q = q_ref[0] * scale
    qt = jnp.concatenate([q] * n_h, axis=0)
    rh = lax.broadcasted_iota(jnp.int32, (n_h * n_s, d), 0) // n_s
    ch = lax.broadcasted_iota(jnp.int32, (n_h * n_s, d), 1) // HEAD_DIM
    qbd = jnp.where(rh == ch, qt, 0.0).astype(BF16)
    nt = (((1,), (1,)), ((), ()))
    s1 = lax.dot_general(qbd, kc_ref[0].astype(BF16), nt, preferred_element_type=F32) + bc_ref[...]
    s2 = lax.dot_general(qbd, kn_ref[0].astype(BF16), nt, preferred_element_type=F32) + bn_ref[...]
    m = jnp.maximum(jnp.max(s1, axis=-1, keepdims=True), jnp.max(s2, axis=-1, keepdims=True))
    p1 = jnp.exp(s1 - m)
    p2 = jnp.exp(s2 - m)
    l = jnp.sum(p1, axis=-1, keepdims=True) + jnp.sum(p2, axis=-1, keepdims=True)
    o = (jnp.dot(p1.astype(BF16), vc_ref[0].astype(BF16), preferred_element_type=F32)
         + jnp.dot(p2.astype(BF16), vn_ref[0].astype(BF16), preferred_element_type=F32)) / l
    first = lax.broadcasted_iota(jnp.int32, (1, LANES), 1) < HEAD_DIM
    for j in range(d // LANES):
        blk = o[:, j * LANES:(j + 1) * LANES]
        lo = blk[(2 * j) * n_s:(2 * j + 1) * n_s, :]
        hi = blk[(2 * j + 1) * n_s:(2 * j + 2) * n_s, :]
        o_ref[0, :, j * LANES:(j + 1) * LANES] = jnp.where(first, lo, hi)


def _att_step(proj, k_cache, v_cache, bias_c, bias_n, q_col, k_col, v_col, n_h):
    n_b, n_s, _ = proj.shape
    n_w, d = k_cache.shape[1], k_cache.shape[2]
    spec = lambda c0: pl.BlockSpec((1, n_s, d), lambda b: (b, 0, c0))
    cspec = pl.BlockSpec((1, n_w, d), lambda b: (b, 0, 0))
    return pl.pallas_call(
        functools.partial(_att_step_kernel, n_h=n_h, scale=HEAD_DIM ** -0.5),
        grid=(n_b,),
        in_specs=[spec(q_col), spec(k_col), spec(v_col), cspec, cspec,
                  pl.BlockSpec((n_h * n_s, n_w), lambda b: (0, 0)),
                  pl.BlockSpec((n_h * n_s, n_s), lambda b: (0, 0))],
        out_specs=pl.BlockSpec((1, n_s, d), lambda b: (b, 0, 0)),
        out_shape=jax.ShapeDtypeStruct((n_b, n_s, d), F32),
        compiler_params=_cparams("parallel"),
        name="att_step",
    )(proj, proj, proj, k_cache, v_cache, bias_c, bias_n)


def _step_bias(table, n_w, n_s):
    rel = (n_w + np.arange(n_s))[:, None] - np.arange(n_w + n_s)[None, :]
    idx = np.clip(rel, -REL_CLIP, REL_CLIP) + REL_CLIP
    bias = table[:, idx].astype(F32).reshape(table.shape[0] * n_s, n_w + n_s)
    return bias[:, :n_w], bias[:, n_w:]


def _rwkv_prep_kernel(r_ref, k_ref, v_ref, u_ref, sh_ref, mu_ref, w0_ref, w1_ref, w2_ref, a0_ref, a1_ref, a2_ref,
                      kk_ref, ka_ref, nkk_out, w_out, b_out, k_out, r_out, v_out, prev):
    j = pl.program_id(1)

    @pl.when(j == 0)
    def _():
        prev[...] = sh_ref[0]

    n_r = r_ref.shape[1]
    row0 = lax.broadcasted_iota(jnp.int32, (n_r, 1), 0) == 0

    def delta(x, i):
        sh = jnp.where(row0, prev[i:i + 1, :], pltpu.roll(x, 1, axis=0))
        prev[i:i + 1, :] = x[n_r - 1:, :]
        return sh - x

    rp, kp, vp, up = r_ref[0], k_ref[0], v_ref[0], u_ref[0]
    du = delta(up, 3)
    r = rp + delta(rp, 0) * mu_ref[0:1, :]
    k = kp + delta(kp, 1) * mu_ref[1:2, :]
    v = vp + delta(vp, 2) * mu_ref[2:3, :]
    xw = up + du * mu_ref[3:4, :]
    xa = up + du * mu_ref[4:5, :]
    dot = lambda a, b: jnp.dot(a.astype(BF16), b, preferred_element_type=F32)
    z = -(w0_ref[...] + dot(jnp.tanh(dot(xw, w1_ref[...])), w2_ref[...]))
    softplus = jnp.maximum(z, 0.0) + jnp.log1p(jnp.exp(-jnp.abs(z)))
    w = -softplus - 0.5
    decay = jnp.exp(-jnp.exp(w))
    a = jax.nn.sigmoid(a0_ref[...] + dot(dot(xa, a1_ref[...]), a2_ref[...]))
    kk = k * kk_ref[...]
    ss = _head_sum(kk * kk, _block_ones(kk.shape[1]))
    kk = kk * lax.rsqrt(jnp.maximum(ss, 1e-24))
    nkk_out[0] = -kk
    w_out[0] = decay
    b_out[0] = kk * a
    k_out[0] = k * (1.0 + (a - 1.0) * ka_ref[...])
    r_out[0] = r
    v_out[0] = v


def _rwkv_prep(proj, shift0, lw, col0):
    n_b, n_l, _ = proj.shape
    d = shift0.shape[2]
    tr = min(n_l, 512)
    assert n_l % tr == 0
    spec = lambda c: pl.BlockSpec((1, tr, d), lambda b, j: (b, j, c))
    full = lambda a: pl.BlockSpec(a.shape, lambda b, j: (0,) * a.ndim)
    ws = [lw["mu"], lw["w0"], lw["w1"], lw["w2"], lw["a0"], lw["a1"], lw["a2"], lw["k_k"], lw["k_a"]]
    ospec = pl.BlockSpec((1, tr, d), lambda b, j: (b, j, 0))
    return pl.pallas_call(
        _rwkv_prep_kernel,
        grid=(n_b, n_l // tr),
        in_specs=[spec(col0), spec(col0 + 1), spec(col0 + 2), spec(col0 + 3),
                  pl.BlockSpec((1, 4, d), lambda b, j: (b, 0, 0))] + [full(a) for a in ws],
        out_specs=[ospec] * 6,
        out_shape=[jax.ShapeDtypeStruct((n_b, n_l, d), F32)] * 6,
        scratch_shapes=[pltpu.VMEM((4, d), F32)],
        compiler_params=_cparams("parallel", "arbitrary"),
        name="rwkv_prep",
    )(proj, proj, proj, proj, shift0, *ws)


def _rwkv_rec_kernel(nkk_ref, w_ref, b_ref, k_ref, r_ref, v_ref, s0_ref, rk_ref, g_ref, beta_ref,
                     y_ref, s_out, state, zbuf, *, nb, n_pairs, lc):
    c = pl.program_id(1)

    @pl.when(c == 0)
    def _():
        state[...] = s0_ref[...]

    ones = _block_ones(LANES)
    ri = lax.broadcasted_iota(jnp.int32, (HEAD_DIM, LANES), 0)
    ci = lax.broadcasted_iota(jnp.int32, (HEAD_DIM, LANES), 1)
    diag = (ci % HEAD_DIM == ri).astype(F32)
    hsum = lambda x: jnp.dot(x.astype(BF16), ones, preferred_element_type=F32)
    chains = [(bb, p) for bb in range(nb) for p in range(n_pairs)]

    def steps(t8, ss):
        t0 = pl.multiple_of(t8 * SUBLANES, SUBLANES)
        out = []
        for (bb, p), s in zip(chains, ss):
            lanes = slice(p * LANES, (p + 1) * LANES)
            tile = lambda ref: ref[bb, pl.ds(t0, SUBLANES), lanes]
            nkk, w, b, k, r, v = (tile(ref) for ref in (nkk_ref, w_ref, b_ref, k_ref, r_ref, v_ref))
            for i in range(SUBLANES):
                row = lambda x: x[i:i + 1, :]
                sa = hsum(s * row(nkk))
                vc = hsum(diag * row(v))
                s = s * row(w) + sa * row(b) + vc * row(k)
                y2 = hsum(s * row(r)) * diag
                z0 = pl.multiple_of((t0 + i) * SUBLANES, SUBLANES)
                zbuf[bb, pl.ds(z0, SUBLANES), lanes] = jnp.sum(
                    y2.reshape(HEAD_DIM // SUBLANES, SUBLANES, LANES), axis=0)
            out.append(s)
        return tuple(out)

    ss = lax.fori_loop(0, lc // SUBLANES, steps, tuple(state[bb, p] for bb, p in chains))
    for (bb, p), s in zip(chains, ss):
        state[bb, p] = s

    @pl.when(c == pl.num_programs(1) - 1)
    def _():
        s_out[...] = state[...]

    d = n_pairs * LANES
    ones_d = _block_ones(d)
    inv_n = 1.0 / HEAD_DIM
    for bb in range(nb):
        y = jnp.sum(zbuf[bb].reshape(lc, SUBLANES, d), axis=1)
        mean = _head_sum(y, ones_d) * inv_n
        yc = y - mean
        var = _head_sum(yc * yc, ones_d) * inv_n
        yn = yc * lax.rsqrt(var + GN_EPS) * g_ref[...] + beta_ref[...]
        bonus = _head_sum(r_ref[bb] * k_ref[bb] * rk_ref[...], ones_d) * v_ref[bb]
        y_ref[bb] = yn + bonus


def _rwkv_rec(prep, s0, r_k, lnx_g, lnx_b, nb):
    n_b, n_l, d = prep[0].shape
    n_pairs = d // LANES
    lc = min(n_l, 256)
    assert n_b % nb == 0 and n_l % lc == 0
    spec = pl.BlockSpec((nb, lc, d), lambda g, c: (g, c, 0))
    sspec = pl.BlockSpec((nb, n_pairs, HEAD_DIM, LANES), lambda g, c: (g, 0, 0, 0))
    vec = pl.BlockSpec((1, d), lambda g, c: (0, 0))
    return pl.pallas_call(
        functools.partial(_rwkv_rec_kernel, nb=nb, n_pairs=n_pairs, lc=lc),
        grid=(n_b // nb, n_l // lc),
        in_specs=[spec] * 6 + [sspec, vec, vec, vec],
        out_specs=[spec, sspec],
        out_shape=[jax.ShapeDtypeStruct((n_b, n_l, d), F32),
                   jax.ShapeDtypeStruct((n_b, n_pairs, HEAD_DIM, LANES), F32)],
        scratch_shapes=[pltpu.VMEM((nb, n_pairs, HEAD_DIM, LANES), F32),
                        pltpu.VMEM((nb, lc * SUBLANES, d), F32)],
        compiler_params=_cparams("parallel", "arbitrary"),
        name="rwkv_rec",
    )(*prep, s0, r_k, lnx_g, lnx_b)


def _pack_pairs(s):
    n_b, n_h, n, _ = s.shape
    return s.reshape(n_b, n_h // 2, 2, n, n).transpose(0, 1, 3, 2, 4).reshape(n_b, n_h // 2, n, 2 * n)


def _unpack_pairs(s):
    n_b, n_p, n, _ = s.shape
    return s.reshape(n_b, n_p, n, 2, n).transpose(0, 1, 3, 2, 4).reshape(n_b, 2 * n_p, n, n)


def _mix_kernel(x_ref, ys_ref, gs_ref, ya_ref, ga_ref, yc_ref, gc_ref, wg_ref, bg_ref, wo_ref, lg_ref, lb_ref,
                y_ref, yb_ref, *, alpha, d_ssm, d_att):
    silu = lambda g: g * jax.nn.sigmoid(g)
    dot = lambda a, b: jnp.dot(a.astype(BF16), b, preferred_element_type=F32)
    ys = ys_ref[...]
    z = 0.5 * ys * (1.0 + jnp.tanh(math.sqrt(2.0 / math.pi) * (ys + 0.044715 * (ys * ys * ys))))
    m_s = z * jax.nn.sigmoid(dot(z, wg_ref[...]) + bg_ref[...]) * silu(gs_ref[...])
    m_a = ya_ref[...] * silu(ga_ref[...])
    m_c = yc_ref[...] * silu(gc_ref[...])
    out = (dot(m_s, wo_ref[0:d_ssm, :]) + dot(m_a, wo_ref[d_ssm:d_ssm + d_att, :])
           + dot(m_c, wo_ref[d_ssm + d_att:, :]))
    h = alpha * x_ref[...] + out
    mu = jnp.mean(h, axis=-1, keepdims=True)
    hc = h - mu
    var = jnp.mean(hc * hc, axis=-1, keepdims=True)
    y = hc * lax.rsqrt(var + LN_EPS) * lg_ref[...] + lb_ref[...]
    y_ref[...] = y
    yb_ref[...] = y.astype(BF16)


def _mix(x, proj, ys, ya, yc, lw, alpha, gs_col, ga_col, gc_col):
    m, d = x.shape
    d_ssm, d_att, d_rw = ys.shape[1], ya.shape[1], yc.shape[1]
    tm = min(m, 256)
    assert m % tm == 0
    row = lambda w, c: pl.BlockSpec((tm, w), lambda i: (i, c))
    full = lambda a: pl.BlockSpec(a.shape, lambda i: (0,) * a.ndim)
    ws = [lw["w_glu"], lw["b_glu"], lw["w_out"], lw["ln_g"], lw["ln_b"]]
    return pl.pallas_call(
        functools.partial(_mix_kernel, alpha=alpha, d_ssm=d_ssm, d_att=d_att),
        grid=(m // tm,),
        in_specs=[row(d, 0), row(d_ssm, 0), row(d_ssm, gs_col), row(d_att, 0), row(d_att, ga_col),
                  row(d_rw, 0), row(d_rw, gc_col)] + [full(a) for a in ws],
        out_specs=[row(d, 0), row(d, 0)],
        out_shape=[jax.ShapeDtypeStruct((m, d), F32), jax.ShapeDtypeStruct((m, d), BF16)],
        compiler_params=_cparams("parallel"),
        name="mix_out",
    )(x, ys, proj, ya, proj, yc, proj, *ws)


def _layer(x, xb, st, lw, s5p, alpha):
    n_b, n_l, d_model = x.shape
    d_ssm = lw["w_glu"].shape[0]
    n_h = lw["att_rel_bias"].shape[0]
    d_att = n_h * HEAD_DIM
    d_rw = lw["rwkv_w0"].shape[1]
    n_rh = d_rw // HEAD_DIM
    n_blk = d_ssm // LANES
    proj = _in_proj(xb.reshape(n_b * n_l, d_model), lw["w_in"]).reshape(n_b, n_l, -1)
    q0 = 2 * d_ssm
    r0 = q0 + 4 * d_att
    assert q0 % d_att == 0 and r0 % d_rw == 0 and d_ssm == d_rw

    if st is None:
        zeros = jnp.zeros((n_b, n_blk, 1, S5_BLOCK), F32)
        h0r = h0i = zeros
        wkv0 = jnp.zeros((n_b, n_rh // 2, HEAD_DIM, LANES), F32)
        shift0 = jnp.zeros((n_b, 4, d_rw), F32)
    else:
        k_cache, v_cache, h0r, h0i, wkv0, shift0 = st
        h0r = h0r.reshape(n_b, n_blk, 1, S5_BLOCK)
        h0i = h0i.reshape(n_b, n_blk, 1, S5_BLOCK)
        wkv0 = _pack_pairs(wkv0)
        shift0 = shift0.reshape(n_b, 4, d_rw)

    s5p = dict(s5p)
    s5p["pr"], s5p["pi"] = s5p["pows"][n_l // S5_SEGMENTS]
    ys, h_re, h_im = _s5(proj, s5p, h0r, h0i)

    kq = proj[:, :, q0 + d_att:q0 + 2 * d_att]
    vq = proj[:, :, q0 + 2 * d_att:q0 + 3 * d_att]
    if st is None:
        pl0 = q0 // LANES
        ya = _att_prompt(proj, _prompt_bias(lw["att_rel_bias"]), pl0, pl0 + d_att // LANES,
                         pl0 + 2 * d_att // LANES, n_h // 2)
        n_keep = min(ATT_REACH, n_l)
        k_rows, v_rows = kq[:, n_l - n_keep:], vq[:, n_l - n_keep:]
    else:
        n_w = k_cache.shape[1]
        bias_c, bias_n = _step_bias(lw["att_rel_bias"], n_w, n_l)
        c0 = q0 // d_att
        ya = _att_step(proj, k_cache.reshape(n_b, n_w, d_att), v_cache.reshape(n_b, n_w, d_att),
                       bias_c, bias_n, c0, c0 + 1, c0 + 2, n_h)
        k_rows, v_rows = kq, vq
    k_rows = k_rows.reshape(n_b, -1, n_h, HEAD_DIM)
    v_rows = v_rows.reshape(n_b, -1, n_h, HEAD_DIM)

    rw = {"mu": lw["rwkv_mu"], "w0": lw["rwkv_w0"], "w1": lw["rwkv_w1"], "w2": lw["rwkv_w2"],
          "a0": lw["rwkv_a0"], "a1": lw["rwkv_a1"], "a2": lw["rwkv_a2"], "k_k": lw["rwkv_k_k"], "k_a": lw["rwkv_k_a"]}
    prep = _rwkv_prep(proj, shift0, rw, r0 // d_rw)
    yc, wkv = _rwkv_rec(prep, wkv0, lw["rwkv_r_k"], lw["rwkv_lnx_g"], lw["rwkv_lnx_b"], nb=2)
    shift = proj[:, n_l - 1, r0:r0 + 4 * d_rw]

    m = n_b * n_l
    y, yb = _mix(x.reshape(m, d_model), proj.reshape(m, -1), ys.reshape(m, d_ssm), ya.reshape(m, d_att),
                 yc.reshape(m, d_rw), lw, alpha, 1, (q0 + 3 * d_att) // d_att, (r0 + 4 * d_rw) // d_rw)
    n_g = d_ssm // SSM_GROUP
    states = (k_rows, v_rows, h_re.reshape(n_b, n_g, SSM_STATE), h_im.reshape(n_b, n_g, SSM_STATE),
              _unpack_pairs(wkv), shift)
    return y.reshape(n_b, n_l, d_model), yb.reshape(n_b, n_l, d_model), states


def kernel(x_prompt, x_sample, cache_att_k, cache_att_v, state_ssm_re, state_ssm_im, state_rwkv, state_rwkv_shift, w_in, ssm_lam_re, ssm_lam_im, ssm_log_dt, ssm_b_re, ssm_b_im, ssm_c_re, ssm_c_im, ssm_d, ssm_w_glu, ssm_b_glu, att_rel_bias, rwkv_mu, rwkv_w0, rwkv_w1, rwkv_w2, rwkv_a0, rwkv_a1, rwkv_a2, rwkv_k_k, rwkv_k_a, rwkv_r_k, rwkv_lnx_g, rwkv_lnx_b, w_out, ln_g, ln_b):
    depth = w_in.shape[0]
    alpha = (2.0 * depth) ** 0.25
    y_p, y_s = x_prompt, x_sample
    yb_p, yb_s = x_prompt.astype(BF16), x_sample.astype(BF16)
    seg_lens = sorted({x_prompt.shape[1] // S5_SEGMENTS, x_sample.shape[1] // S5_SEGMENTS})
    p_st, s_st = [], []
    row = lambda a: a.reshape(1, -1)
    for l in range(depth):
        lw = {"w_in": w_in[l].astype(BF16), "w_glu": ssm_w_glu[l].astype(BF16), "b_glu": row(ssm_b_glu[l]),
              "att_rel_bias": att_rel_bias[l], "rwkv_mu": rwkv_mu[l], "rwkv_w0": row(rwkv_w0[l]),
              "rwkv_w1": rwkv_w1[l].astype(BF16), "rwkv_w2": rwkv_w2[l].astype(BF16), "rwkv_a0": row(rwkv_a0[l]),
              "rwkv_a1": rwkv_a1[l].astype(BF16), "rwkv_a2": rwkv_a2[l].astype(BF16), "rwkv_k_k": row(rwkv_k_k[l]),
              "rwkv_k_a": row(rwkv_k_a[l]), "rwkv_r_k": row(rwkv_r_k[l]), "rwkv_lnx_g": row(rwkv_lnx_g[l]),
              "rwkv_lnx_b": row(rwkv_lnx_b[l]), "w_out": w_out[l].astype(BF16), "ln_g": row(ln_g[l]),
              "ln_b": row(ln_b[l])}
        s5p = _s5_params(ssm_lam_re[l], ssm_lam_im[l], ssm_log_dt[l], ssm_b_re[l], ssm_b_im[l], ssm_c_re[l],
                         ssm_c_im[l], ssm_d[l], seg_lens)
        y_p, yb_p, st_p = _layer(y_p, yb_p, None, lw, s5p, alpha)
        y_s, yb_s, st_s = _layer(y_s, yb_s, (cache_att_k[l], cache_att_v[l], state_ssm_re[l], state_ssm_im[l],
                                             state_rwkv[l], state_rwkv_shift[l]), lw, s5p, alpha)
        p_st.append(st_p)
        s_st.append(st_s)
    stacked = lambda states, i: jnp.stack([st[i] for st in states], axis=0)
    return (y_p, y_s) + tuple(stacked(p_st, i) for i in range(6)) + tuple(stacked(s_st, i) for i in range(6))
```

```python
import functools
import math

import jax
import jax.numpy as jnp
import numpy as np
from jax import lax
from jax.experimental import pallas as pl
from jax.experimental.pallas import tpu as pltpu

F32 = jnp.float32
BF16 = jnp.bfloat16

LANES = 128
SUBLANES = 8
VMEM_LIMIT = 56 * 1024 * 1024

CHUNK = 64
LEFT_CHUNKS = 8
ATT_REACH = LEFT_CHUNKS * CHUNK
REL_CLIP = 128
HEAD_DIM = 64
SSM_GROUP = 16
SSM_STATE = 64
RWKV_LORA = 64
NEG_INF = -1e30
GN_EPS = 64e-5
LN_EPS = 1e-5

Q_BLOCK = 4 * CHUNK
K_WINDOW = Q_BLOCK + ATT_REACH
S5_SEGMENTS = SUBLANES
S5_BLOCK = 512


def _cparams(*sem):
    return pltpu.CompilerParams(dimension_semantics=sem, vmem_limit_bytes=VMEM_LIMIT)


def _block_ones(n):
    r = lax.broadcasted_iota(jnp.int32, (n, n), 0) // HEAD_DIM
    c = lax.broadcasted_iota(jnp.int32, (n, n), 1) // HEAD_DIM
    return (r == c).astype(BF16)


def _head_sum(x, ones):
    hi = x.astype(BF16)
    lo = (x - hi.astype(F32)).astype(BF16)
    return jnp.dot(hi, ones, preferred_element_type=F32) + jnp.dot(lo, ones, preferred_element_type=F32)


def _matmul_kernel(x_ref, w_ref, o_ref):
    o_ref[...] = jnp.dot(x_ref[...], w_ref[...], preferred_element_type=F32)


def _in_proj(x, w):
    m, k = x.shape
    n = w.shape[1]
    tm = min(m, 1024)
    tn = 768
    assert m % tm == 0 and n % tn == 0
    return pl.pallas_call(
        _matmul_kernel,
        grid=(n // tn, m // tm),
        in_specs=[pl.BlockSpec((tm, k), lambda j, i: (i, 0)),
                  pl.BlockSpec((k, tn), lambda j, i: (0, j))],
        out_specs=pl.BlockSpec((tm, tn), lambda j, i: (i, j)),
        out_shape=jax.ShapeDtypeStruct((m, n), F32),
        compiler_params=_cparams("parallel", "parallel"),
        name="in_proj",
    )(x, w)


def _s5_kernel(u_ref, wbr_ref, wbi_ref, ar_ref, ai_ref, pr_ref, pi_ref, h0r_ref, h0i_ref,
               wcr_ref, wci_ref, d_ref, y_ref, hr_out, hi_out, *scratch, seg):
    nlb = (len(scratch) - 1) // 2
    sr, si, up = scratch[:nlb], scratch[nlb:2 * nlb], scratch[2 * nlb]
    lb = lambda j: slice(j * LANES, (j + 1) * LANES)
    rows = lambda i: pl.ds(pl.multiple_of(i * S5_SEGMENTS, S5_SEGMENTS), S5_SEGMENTS)
    strided = lambda i: pl.ds(i, S5_SEGMENTS, stride=seg)
    unroll = min(seg, 8)

    def interleave(i, c):
        up[rows(i), :] = u_ref[0, strided(i), :]
        return c

    lax.fori_loop(0, seg, interleave, 0, unroll=unroll)
    u = up[...]
    ub = u.astype(BF16)
    for j in range(nlb):
        sr[j][...] = jnp.dot(ub, wbr_ref[0, :, lb(j)], preferred_element_type=F32)
        si[j][...] = jnp.dot(ub, wbi_ref[0, :, lb(j)], preferred_element_type=F32)
    bcast = lambda ref, j: jnp.broadcast_to(ref[0, :, lb(j)], (S5_SEGMENTS, LANES))
    ar = [bcast(ar_ref, j) for j in range(nlb)]
    ai = [bcast(ai_ref, j) for j in range(nlb)]

    def advance(i, hr, hi):
        nr = [ar[j] * hr[j] - ai[j] * hi[j] + sr[j][rows(i), :] for j in range(nlb)]
        ni = [ar[j] * hi[j] + ai[j] * hr[j] + si[j][rows(i), :] for j in range(nlb)]
        return nr, ni

    def body1(i, c):
        nr, ni = advance(i, c[:nlb], c[nlb:])
        return tuple(nr + ni)

    zero = jnp.zeros((S5_SEGMENTS, LANES), F32)
    ends = lax.fori_loop(0, seg, body1, (zero,) * (2 * nlb), unroll=unroll)

    starts = []
    for j in range(nlb):
        er, ei = ends[j], ends[nlb + j]
        pr, pi = pr_ref[0, :, lb(j)], pi_ref[0, :, lb(j)]
        cr, ci = h0r_ref[0, 0, :, lb(j)], h0i_ref[0, 0, :, lb(j)]
        start_r, start_i = [cr], [ci]
        for s in range(S5_SEGMENTS - 1):
            cr, ci = (pr * cr - pi * ci + er[s:s + 1, :], pr * ci + pi * cr + ei[s:s + 1, :])
            start_r.append(cr)
            start_i.append(ci)
        starts.append((jnp.concatenate(start_r, axis=0), jnp.concatenate(start_i, axis=0)))

    def body2(i, c):
        nr, ni = advance(i, c[:nlb], c[nlb:])
        for j in range(nlb):
            sr[j][rows(i), :] = nr[j]
            si[j][rows(i), :] = ni[j]
        return tuple(nr + ni)

    last = lax.fori_loop(0, seg, body2, tuple(s[0] for s in starts) + tuple(s[1] for s in starts), unroll=unroll)
    y = d_ref[...] * u
    for j in range(nlb):
        hr_out[0, 0, :, lb(j)] = last[j][S5_SEGMENTS - 1:, :]
        hi_out[0, 0, :, lb(j)] = last[nlb + j][S5_SEGMENTS - 1:, :]
        y += (jnp.dot(sr[j][...].astype(BF16), wcr_ref[0, lb(j), :], preferred_element_type=F32)
              - jnp.dot(si[j][...].astype(BF16), wci_ref[0, lb(j), :], preferred_element_type=F32))
    up[...] = y

    def deinterleave(i, c):
        y_ref[0, strided(i), :] = up[rows(i), :]
        return c

    lax.fori_loop(0, seg, deinterleave, 0, unroll=unroll)


def _s5(proj, prm, h0r, h0i):
    n_b, n_l, _ = proj.shape
    nblk = prm["wbr"].shape[0]
    cin = prm["wbr"].shape[1]
    seg = n_l // S5_SEGMENTS
    assert seg * S5_SEGMENTS == n_l and cin == LANES
    wspec = lambda shape: pl.BlockSpec((1,) + shape, lambda b, s: (s, 0, 0))
    st_spec = pl.BlockSpec((1, 1, 1, S5_BLOCK), lambda b, s: (b, s, 0, 0))
    return pl.pallas_call(
        functools.partial(_s5_kernel, seg=seg),
        grid=(n_b, nblk),
        in_specs=[pl.BlockSpec((1, n_l, cin), lambda b, s: (b, 0, s)),
                  wspec((cin, S5_BLOCK)), wspec((cin, S5_BLOCK)),
                  wspec((1, S5_BLOCK)), wspec((1, S5_BLOCK)), wspec((1, S5_BLOCK)), wspec((1, S5_BLOCK)),
                  st_spec, st_spec,
                  wspec((S5_BLOCK, cin)), wspec((S5_BLOCK, cin)),
                  pl.BlockSpec((1, cin), lambda b, s: (0, s))],
        out_specs=[pl.BlockSpec((1, n_l, cin), lambda b, s: (b, 0, s)), st_spec, st_spec],
        out_shape=[jax.ShapeDtypeStruct((n_b, n_l, nblk * cin), F32),
                   jax.ShapeDtypeStruct((n_b, nblk, 1, S5_BLOCK), F32),
                   jax.ShapeDtypeStruct((n_b, nblk, 1, S5_BLOCK), F32)],
        scratch_shapes=[pltpu.VMEM((n_l, LANES), F32)] * (2 * S5_BLOCK // LANES + 1),
        compiler_params=_cparams("parallel", "parallel"),
        name="s5_scan",
    )(proj, prm["wbr"], prm["wbi"], prm["ar"], prm["ai"], prm["pr"], prm["pi"], h0r, h0i,
      prm["wcr"], prm["wci"], prm["d"])


def _s5_params(lam_re, lam_im, log_dt, b_re, b_im, c_re, c_im, d_skip, seg_lens):
    n_g, n_p = lam_re.shape
    dt = jnp.exp(log_dt)[:, None]
    e = jnp.exp(lam_re * dt)
    ab_re, ab_im = e * jnp.cos(lam_im * dt), e * jnp.sin(lam_im * dt)
    den = lam_re * lam_re + lam_im * lam_im
    nr, ni = ab_re - 1.0, ab_im
    q_re = (nr * lam_re + ni * lam_im) / den
    q_im = (ni * lam_re - nr * lam_im) / den
    bb_re = q_re[..., None] * b_re - q_im[..., None] * b_im
    bb_im = q_re[..., None] * b_im + q_im[..., None] * b_re
    gpb = S5_BLOCK // n_p
    nblk = n_g // gpb
    eye = jnp.eye(gpb, dtype=F32)

    def pack_b(t):
        t = t.reshape(nblk, gpb, n_p, SSM_GROUP)
        return jnp.einsum("sgpc,gh->sgchp", t, eye).reshape(nblk, gpb * SSM_GROUP, gpb * n_p).astype(BF16)

    def pack_c(t):
        t = t.reshape(nblk, gpb, SSM_GROUP, n_p)
        return jnp.einsum("sgcp,gh->shpgc", t, eye).reshape(nblk, gpb * n_p, gpb * SSM_GROUP).astype(BF16)

    flat = lambda t: t.reshape(nblk, 1, gpb * n_p)
    out = {"wbr": pack_b(bb_re), "wbi": pack_b(bb_im), "wcr": pack_c(c_re), "wci": pack_c(c_im),
           "ar": flat(ab_re), "ai": flat(ab_im), "d": d_skip.reshape(1, -1), "pows": {}}
    for seg in seg_lens:
        assert seg & (seg - 1) == 0
        pr, pi = ab_re, ab_im
        for _ in range(int(math.log2(seg))):
            pr, pi = pr * pr - pi * pi, 2.0 * pr * pi
        out["pows"][seg] = (flat(pr), flat(pi))
    return out


def _att_prompt_kernel(q_ref, k_ref, v_ref, bias_ref, o_ref, kpad, vpad, *, n_l, scale):
    zpad = jnp.zeros((ATT_REACH, LANES), BF16)
    kpad[0:ATT_REACH, :] = zpad
    vpad[0:ATT_REACH, :] = zpad
    kpad[ATT_REACH:, :] = k_ref[0].astype(BF16)
    vpad[ATT_REACH:, :] = v_ref[0].astype(BF16)
    lane = lax.broadcasted_iota(jnp.int32, (1, LANES), 1)
    first = lane < HEAD_DIM
    col = lax.broadcasted_iota(jnp.int32, (1, K_WINDOW), 1)

    def body(qb, carry):
        r0 = pl.multiple_of(qb * Q_BLOCK, Q_BLOCK)
        q = q_ref[0, pl.ds(r0, Q_BLOCK), :] * scale
        kw = kpad[pl.ds(r0, K_WINDOW), :]
        vw = vpad[pl.ds(r0, K_WINDOW), :]
        before_start = (col + r0) < ATT_REACH
        outs = []
        for h in range(2):
            sel = first if h == 0 else jnp.logical_not(first)
            qh = jnp.where(sel, q, 0.0).astype(BF16)
            s = lax.dot_general(qh, kw, (((1,), (1,)), ((), ())), preferred_element_type=F32)
            s = jnp.where(before_start, NEG_INF, s + bias_ref[0, h])
            m = jnp.max(s, axis=-1, keepdims=True)
            p = jnp.exp(s - m)
            l = jnp.sum(p, axis=-1, keepdims=True)
            outs.append(jnp.dot(p.astype(BF16), vw, preferred_element_type=F32) / l)
        o_ref[0, pl.ds(r0, Q_BLOCK), :] = jnp.where(first, outs[0], outs[1])
        return carry

    lax.fori_loop(0, n_l // Q_BLOCK, body, 0)


def _att_prompt(proj, bias, q_col, k_col, v_col, n_pairs):
    n_b, n_l, _ = proj.shape
    assert n_l % Q_BLOCK == 0
    spec = lambda c0: pl.BlockSpec((1, n_l, LANES), lambda b, p: (b, 0, c0 + p))
    return pl.pallas_call(
        functools.partial(_att_prompt_kernel, n_l=n_l, scale=HEAD_DIM ** -0.5),
        grid=(n_b, n_pairs),
        in_specs=[spec(q_col), spec(k_col), spec(v_col),
                  pl.BlockSpec((1, 2, Q_BLOCK, K_WINDOW), lambda b, p: (p, 0, 0, 0))],
        out_specs=pl.BlockSpec((1, n_l, LANES), lambda b, p: (b, 0, p)),
        out_shape=jax.ShapeDtypeStruct((n_b, n_l, n_pairs * LANES), F32),
        scratch_shapes=[pltpu.VMEM((n_l + ATT_REACH, LANES), BF16), pltpu.VMEM((n_l + ATT_REACH, LANES), BF16)],
        compiler_params=_cparams("parallel", "parallel"),
        name="att_prompt",
    )(proj, proj, proj, bias)


def _bias_kernel(g_ref, bp_ref, bs_ref):
    g = g_ref[0]
    n_q, n_k = bp_ref.shape[1], bp_ref.shape[2]
    toep = pltpu.roll(jnp.broadcast_to(g, (n_q, g.shape[1])), 0, 1, stride=1, stride_axis=0)[:, :n_k]
    qc = lax.broadcasted_iota(jnp.int32, (n_q, n_k), 0) // CHUNK
    kc = lax.broadcasted_iota(jnp.int32, (n_q, n_k), 1) // CHUNK
    bp_ref[0] = jnp.where((kc >= qc) & (kc <= qc + LEFT_CHUNKS), toep, NEG_INF)
    n_s, n_ws = bs_ref.shape[1], bs_ref.shape[2]
    bs_ref[0] = pltpu.roll(jnp.broadcast_to(g, (n_s, g.shape[1])), 0, 1, stride=1, stride_axis=0)[:, :n_ws]


def _rel_bias(table, n_s, n_w):
    n_h = table.shape[0]
    assert n_w == ATT_REACH
    width = 1024
    assert width >= K_WINDOW + Q_BLOCK and width % LANES == 0
    d = np.arange(width)
    d = np.where(d < width - Q_BLOCK, d, d - width)
    idx = np.clip(ATT_REACH - d, -REL_CLIP, REL_CLIP) + REL_CLIP
    diag = table[:, idx].astype(F32).reshape(n_h, 1, width)
    return pl.pallas_call(
        _bias_kernel,
        grid=(n_h,),
        in_specs=[pl.BlockSpec((1, 1, width), lambda h: (h, 0, 0))],
        out_specs=[pl.BlockSpec((1, Q_BLOCK, K_WINDOW), lambda h: (h, 0, 0)),
                   pl.BlockSpec((1, n_s, n_w + n_s), lambda h: (h, 0, 0))],
        out_shape=[jax.ShapeDtypeStruct((n_h, Q_BLOCK, K_WINDOW), F32),
                   jax.ShapeDtypeStruct((n_h, n_s, n_w + n_s), F32)],
        compiler_params=_cparams("parallel"),
        name="rel_bias",
    )(diag)


def _att_step_kernel(q_ref, kn_ref, vn_ref, kc_ref, vc_ref, bc_ref, bn_ref, o_ref, *, n_h, scale):
    n_s = q_ref.shape[1]
    d = q_ref.shape[2]
    q = q_ref[0] * scale
    qt = jnp.concatenate([q] * n_h, axis=0)
    rh = lax.broadcasted_iota(jnp.int32, (n_h * n_s, d), 0) // n_s
    ch = lax.broadcasted_iota(jnp.int32, (n_h * n_s, d), 1) // HEAD_DIM
    qbd = jnp.where(rh == ch, qt, 0.0).astype(BF16)
    nt = (((1,), (1,)), ((), ()))
    s1 = lax.dot_general(qbd, kc_ref[0].astype(BF16), nt, preferred_element_type=F32) + bc_ref[...]
    s2 = lax.dot_general(qbd, kn_ref[0].astype(BF16), nt, preferred_element_type=F32) + bn_ref[...]
    m = jnp.maximum(jnp.max(s1, axis=-1, keepdims=True), jnp.max(s2, axis=-1, keepdims=True))
    p1 = jnp.exp(s1 - m)
    p2 = jnp.exp(s2 - m)
    l = jnp.sum(p1, axis=-1, keepdims=True) + jnp.sum(p2, axis=-1, keepdims=True)
    o = (jnp.dot(p1.astype(BF16), vc_ref[0].astype(BF16), preferred_element_type=F32)
         + jnp.dot(p2.astype(BF16), vn_ref[0].astype(BF16), preferred_element_type=F32)) / l
    first = lax.broadcasted_iota(jnp.int32, (1, LANES), 1) < HEAD_DIM
    for j in range(d // LANES):
        blk = o[:, j * LANES:(j + 1) * LANES]
        lo = blk[(2 * j) * n_s:(2 * j + 1) * n_s, :]
        hi = blk[(2 * j + 1) * n_s:(2 * j + 2) * n_s, :]
        o_ref[0, :, j * LANES:(j + 1) * LANES] = jnp.where(first, lo, hi)


def _att_step(proj, k_cache, v_cache, bias_c, bias_n, q_col, k_col, v_col, n_h):
    n_b, n_s, _ = proj.shape
    n_w, d = k_cache.shape[1], k_cache.shape[2]
    spec = lambda c0: pl.BlockSpec((1, n_s, d), lambda b: (b, 0, c0))
    cspec = pl.BlockSpec((1, n_w, d), lambda b: (b, 0, 0))
    return pl.pallas_call(
        functools.partial(_att_step_kernel, n_h=n_h, scale=HEAD_DIM ** -0.5),
        grid=(n_b,),
        in_specs=[spec(q_col), spec(k_col), spec(v_col), cspec, cspec,
                  pl.BlockSpec((n_h * n_s, n_w), lambda b: (0, 0)),
                  pl.BlockSpec((n_h * n_s, n_s), lambda b: (0, 0))],
        out_specs=pl.BlockSpec((1, n_s, d), lambda b: (b, 0, 0)),
        out_shape=jax.ShapeDtypeStruct((n_b, n_s, d), F32),
        compiler_params=_cparams("parallel"),
        name="att_step",
    )(proj, proj, proj, k_cache, v_cache, bias_c, bias_n)


def _rwkv_prep_kernel(r_ref, k_ref, v_ref, u_ref, sh_ref, mu_ref, w0_ref, w1_ref, w2_ref, a0_ref, a1_ref, a2_ref,
                      kk_ref, ka_ref, nkk_out, w_out, b_out, k_out, r_out, v_out, prev):
    j = pl.program_id(1)

    @pl.when(j == 0)
    def _():
        prev[...] = sh_ref[0]

    n_r = r_ref.shape[1]
    row0 = lax.broadcasted_iota(jnp.int32, (n_r, 1), 0) == 0

    def delta(x, i):
        sh = jnp.where(row0, prev[i:i + 1, :], pltpu.roll(x, 1, axis=0))
        prev[i:i + 1, :] = x[n_r - 1:, :]
        return sh - x

    rp, kp, vp, up = r_ref[0], k_ref[0], v_ref[0], u_ref[0]
    du = delta(up, 3)
    r = rp + delta(rp, 0) * mu_ref[0:1, :]
    k = kp + delta(kp, 1) * mu_ref[1:2, :]
    v = vp + delta(vp, 2) * mu_ref[2:3, :]
    xw = up + du * mu_ref[3:4, :]
    xa = up + du * mu_ref[4:5, :]
    dot = lambda a, b: jnp.dot(a.astype(BF16), b, preferred_element_type=F32)
    z = -(w0_ref[...] + dot(jnp.tanh(dot(xw, w1_ref[...])), w2_ref[...]))
    softplus = jnp.maximum(z, 0.0) + jnp.log1p(jnp.exp(-jnp.abs(z)))
    w = -softplus - 0.5
    decay = jnp.exp(-jnp.exp(w))
    a = jax.nn.sigmoid(a0_ref[...] + dot(dot(xa, a1_ref[...]), a2_ref[...]))
    kk = k * kk_ref[...]
    ss = _head_sum(kk * kk, _block_ones(kk.shape[1]))
    kk = kk * lax.rsqrt(jnp.maximum(ss, 1e-24))
    nkk_out[0] = -kk
    w_out[0] = decay
    b_out[0] = kk * a
    k_out[0] = k * (1.0 + (a - 1.0) * ka_ref[...])
    r_out[0] = r
    v_out[0] = v


def _rwkv_prep(proj, shift0, lw, col0):
    n_b, n_l, _ = proj.shape
    d = shift0.shape[2]
    tr = min(n_l, 512)
    assert n_l % tr == 0
    spec = lambda c: pl.BlockSpec((1, tr, d), lambda b, j: (b, j, c))
    full = lambda a: pl.BlockSpec(a.shape, lambda b, j: (0,) * a.ndim)
    ws = [lw["mu"], lw["w0"], lw["w1"], lw["w2"], lw["a0"], lw["a1"], lw["a2"], lw["k_k"], lw["k_a"]]
    ospec = pl.BlockSpec((1, tr, d), lambda b, j: (b, j, 0))
    return pl.pallas_call(
        _rwkv_prep_kernel,
        grid=(n_b, n_l // tr),
        in_specs=[spec(col0), spec(col0 + 1), spec(col0 + 2), spec(col0 + 3),
                  pl.BlockSpec((1, 4, d), lambda b, j: (b, 0, 0))] + [full(a) for a in ws],
        out_specs=[ospec] * 6,
        out_shape=[jax.ShapeDtypeStruct((n_b, n_l, d), F32)] * 6,
        scratch_shapes=[pltpu.VMEM((4, d), F32)],
        compiler_params=_cparams("parallel", "arbitrary"),
        name="rwkv_prep",
    )(proj, proj, proj, proj, shift0, *ws)


def _rwkv_rec_kernel(nkk_ref, w_ref, b_ref, k_ref, r_ref, v_ref, s0_ref, rk_ref, g_ref, beta_ref,
                     y_ref, s_out, state, zbuf, *, nb, n_pairs, lc):
    c = pl.program_id(1)

    @pl.when(c == 0)
    def _():
        state[...] = s0_ref[...]

    ones = _block_ones(LANES)
    ri = lax.broadcasted_iota(jnp.int32, (HEAD_DIM, LANES), 0)
    ci = lax.broadcasted_iota(jnp.int32, (HEAD_DIM, LANES), 1)
    diag = (ci % HEAD_DIM == ri).astype(F32)
    hsum = lambda x: jnp.dot(x.astype(BF16), ones, preferred_element_type=F32)
    chains = [(bb, p) for bb in range(nb) for p in range(n_pairs)]

    def steps(t8, ss):
        t0 = pl.multiple_of(t8 * SUBLANES, SUBLANES)
        ss = list(ss)
        lanes = [slice(p * LANES, (p + 1) * LANES) for _, p in chains]
        tiles = [[ref[bb, pl.ds(t0, SUBLANES), lanes[c]] for ref in (nkk_ref, w_ref, b_ref, k_ref, r_ref, v_ref)]
                 for c, (bb, _) in enumerate(chains)]
        for i in range(SUBLANES):
            row = lambda x: x[i:i + 1, :]
            sa = [hsum(ss[c] * row(tiles[c][0])) for c in range(len(chains))]
            vc = [hsum(diag * row(tiles[c][5])) for c in range(len(chains))]
            for c in range(len(chains)):
                _, w, b, k, _, _ = tiles[c]
                ss[c] = ss[c] * row(w) + sa[c] * row(b) + vc[c] * row(k)
            y2 = [hsum(ss[c] * row(tiles[c][4])) * diag for c in range(len(chains))]
            z0 = pl.multiple_of((t0 + i) * SUBLANES, SUBLANES)
            for c, (bb, _) in enumerate(chains):
                zbuf[bb, pl.ds(z0, SUBLANES), lanes[c]] = jnp.sum(
                    y2[c].reshape(HEAD_DIM // SUBLANES, SUBLANES, LANES), axis=0)
        return tuple(ss)

    ss = lax.fori_loop(0, lc // SUBLANES, steps, tuple(state[bb, p] for bb, p in chains))
    for (bb, p), s in zip(chains, ss):
        state[bb, p] = s

    @pl.when(c == pl.num_programs(1) - 1)
    def _():
        s_out[...] = state[...]

    d = n_pairs * LANES
    ones_d = _block_ones(d)
    inv_n = 1.0 / HEAD_DIM
    grp = LANES // SUBLANES
    pick = (lax.broadcasted_iota(jnp.int32, (grp, LANES), 1) // SUBLANES
            == lax.broadcasted_iota(jnp.int32, (grp, LANES), 0)).astype(BF16)
    for bb in range(nb):
        parts = []
        for g in range(lc // grp):
            z = zbuf[bb, g * LANES:(g + 1) * LANES, :]
            hi = z.astype(BF16)
            lo = (z - hi.astype(F32)).astype(BF16)
            parts.append(jnp.dot(pick, hi, preferred_element_type=F32) + jnp.dot(pick, lo, preferred_element_type=F32))
        y = jnp.concatenate(parts, axis=0) if len(parts) > 1 else parts[0]
        mean = _head_sum(y, ones_d) * inv_n
        yc = y - mean
        var = _head_sum(yc * yc, ones_d) * inv_n
        yn = yc * lax.rsqrt(var + GN_EPS) * g_ref[...] + beta_ref[...]
        bonus = _head_sum(r_ref[bb] * k_ref[bb] * rk_ref[...], ones_d) * v_ref[bb]
        y_ref[bb] = yn + bonus


def _rwkv_rec(prep, s0, r_k, lnx_g, lnx_b, nb):
    n_b, n_l, d = prep[0].shape
    n_pairs = d // LANES
    lc = min(n_l, 256)
    assert n_b % nb == 0 and n_l % lc == 0
    spec = pl.BlockSpec((nb, lc, d), lambda g, c: (g, c, 0))
    sspec = pl.BlockSpec((nb, n_pairs, HEAD_DIM, LANES), lambda g, c: (g, 0, 0, 0))
    vec = pl.BlockSpec((1, d), lambda g, c: (0, 0))
    return pl.pallas_call(
        functools.partial(_rwkv_rec_kernel, nb=nb, n_pairs=n_pairs, lc=lc),
        grid=(n_b // nb, n_l // lc),
        in_specs=[spec] * 6 + [sspec, vec, vec, vec],
        out_specs=[spec, sspec],
        out_shape=[jax.ShapeDtypeStruct((n_b, n_l, d), F32),
                   jax.ShapeDtypeStruct((n_b, n_pairs, HEAD_DIM, LANES), F32)],
        scratch_shapes=[pltpu.VMEM((nb, n_pairs, HEAD_DIM, LANES), F32),
                        pltpu.VMEM((nb, lc * SUBLANES, d), F32)],
        compiler_params=_cparams("parallel", "arbitrary"),
        name="rwkv_rec",
    )(*prep, s0, r_k, lnx_g, lnx_b)


def _pack_pairs(s):
    n_b, n_h, n, _ = s.shape
    return s.reshape(n_b, n_h // 2, 2, n, n).transpose(0, 1, 3, 2, 4).reshape(n_b, n_h // 2, n, 2 * n)


def _unpack_pairs(s):
    n_b, n_p, n, _ = s.shape
    return s.reshape(n_b, n_p, n, 2, n).transpose(0, 1, 3, 2, 4).reshape(n_b, 2 * n_p, n, n)


def _mix_kernel(x_ref, ys_ref, gs_ref, ya_ref, ga_ref, yc_ref, gc_ref, wg_ref, bg_ref, wo_ref, lg_ref, lb_ref,
                y_ref, yb_ref, *, alpha, d_ssm, d_att):
    silu = lambda g: g * jax.nn.sigmoid(g)
    dot = lambda a, b: jnp.dot(a.astype(BF16), b, preferred_element_type=F32)
    ys = ys_ref[...]
    z = 0.5 * ys * (1.0 + jnp.tanh(math.sqrt(2.0 / math.pi) * (ys + 0.044715 * (ys * ys * ys))))
    m_s = z * jax.nn.sigmoid(dot(z, wg_ref[...]) + bg_ref[...]) * silu(gs_ref[...])
    m_a = ya_ref[...] * silu(ga_ref[...])
    m_c = yc_ref[...] * silu(gc_ref[...])
    out = (dot(m_s, wo_ref[0:d_ssm, :]) + dot(m_a, wo_ref[d_ssm:d_ssm + d_att, :])
           + dot(m_c, wo_ref[d_ssm + d_att:, :]))
    h = alpha * x_ref[...] + out
    mu = jnp.mean(h, axis=-1, keepdims=True)
    hc = h - mu
    var = jnp.mean(hc * hc, axis=-1, keepdims=True)
    y = hc * lax.rsqrt(var + LN_EPS) * lg_ref[...] + lb_ref[...]
    y_ref[...] = y
    yb_ref[...] = y.astype(BF16)


def _mix(x, proj, ys, ya, yc, lw, alpha, gs_col, ga_col, gc_col):
    m, d = x.shape
    d_ssm, d_att, d_rw = ys.shape[1], ya.shape[1], yc.shape[1]
    tm = min(m, 256)
    assert m % tm == 0
    row = lambda w, c: pl.BlockSpec((tm, w), lambda i: (i, c))
    full = lambda a: pl.BlockSpec(a.shape, lambda i: (0,) * a.ndim)
    ws = [lw["w_glu"], lw["b_glu"], lw["w_out"], lw["ln_g"], lw["ln_b"]]
    return pl.pallas_call(
        functools.partial(_mix_kernel, alpha=alpha, d_ssm=d_ssm, d_att=d_att),
        grid=(m // tm,),
        in_specs=[row(d, 0), row(d_ssm, 0), row(d_ssm, gs_col), row(d_att, 0), row(d_att, ga_col),
                  row(d_rw, 0), row(d_rw, gc_col)] + [full(a) for a in ws],
        out_specs=[row(d, 0), row(d, 0)],
        out_shape=[jax.ShapeDtypeStruct((m, d), F32), jax.ShapeDtypeStruct((m, d), BF16)],
        compiler_params=_cparams("parallel"),
        name="mix_out",
    )(x, ys, proj, ya, proj, yc, proj, *ws)


def _layer(x, xb, st, lw, s5p, alpha):
    n_b, n_l, d_model = x.shape
    d_ssm = lw["w_glu"].shape[0]
    n_h = lw["att_rel_bias"].shape[0]
    d_att = n_h * HEAD_DIM
    d_rw = lw["rwkv_w0"].shape[1]
    n_rh = d_rw // HEAD_DIM
    n_blk = d_ssm // LANES
    proj = _in_proj(xb.reshape(n_b * n_l, d_model), lw["w_in"]).reshape(n_b, n_l, -1)
    q0 = 2 * d_ssm
    r0 = q0 + 4 * d_att
    assert q0 % d_att == 0 and r0 % d_rw == 0 and d_ssm == d_rw

    if st is None:
        zeros = jnp.zeros((n_b, n_blk, 1, S5_BLOCK), F32)
        h0r = h0i = zeros
        wkv0 = jnp.zeros((n_b, n_rh // 2, HEAD_DIM, LANES), F32)
        shift0 = jnp.zeros((n_b, 4, d_rw), F32)
    else:
        k_cache, v_cache, h0r, h0i, wkv0, shift0 = st
        h0r = h0r.reshape(n_b, n_blk, 1, S5_BLOCK)
        h0i = h0i.reshape(n_b, n_blk, 1, S5_BLOCK)
        wkv0 = _pack_pairs(wkv0)
        shift0 = shift0.reshape(n_b, 4, d_rw)

    s5p = dict(s5p)
    s5p["pr"], s5p["pi"] = s5p["pows"][n_l // S5_SEGMENTS]
    ys, h_re, h_im = _s5(proj, s5p, h0r, h0i)

    kq = proj[:, :, q0 + d_att:q0 + 2 * d_att]
    vq = proj[:, :, q0 + 2 * d_att:q0 + 3 * d_att]
    if st is None:
        pl0 = q0 // LANES
        bias_p = lw["bias_prompt"].reshape(n_h // 2, 2, Q_BLOCK, K_WINDOW)
        ya = _att_prompt(proj, bias_p, pl0, pl0 + d_att // LANES, pl0 + 2 * d_att // LANES, n_h // 2)
        n_keep = min(ATT_REACH, n_l)
        k_rows, v_rows = kq[:, n_l - n_keep:], vq[:, n_l - n_keep:]
    else:
        n_w = k_cache.shape[1]
        bias_s = lw["bias_step"].reshape(n_h * n_l, n_w + n_l)
        bias_c, bias_n = bias_s[:, :n_w], bias_s[:, n_w:]
        c0 = q0 // d_att
        ya = _att_step(proj, k_cache.reshape(n_b, n_w, d_att), v_cache.reshape(n_b, n_w, d_att),
                       bias_c, bias_n, c0, c0 + 1, c0 + 2, n_h)
        k_rows, v_rows = kq, vq
    k_rows = k_rows.reshape(n_b, -1, n_h, HEAD_DIM)
    v_rows = v_rows.reshape(n_b, -1, n_h, HEAD_DIM)

    rw = {"mu": lw["rwkv_mu"], "w0": lw["rwkv_w0"], "w1": lw["rwkv_w1"], "w2": lw["rwkv_w2"],
          "a0": lw["rwkv_a0"], "a1": lw["rwkv_a1"], "a2": lw["rwkv_a2"], "k_k": lw["rwkv_k_k"], "k_a": lw["rwkv_k_a"]}
    prep = _rwkv_prep(proj, shift0, rw, r0 // d_rw)
    yc, wkv = _rwkv_rec(prep, wkv0, lw["rwkv_r_k"], lw["rwkv_lnx_g"], lw["rwkv_lnx_b"], nb=2)
    shift = proj[:, n_l - 1, r0:r0 + 4 * d_rw]

    m = n_b * n_l
    y, yb = _mix(x.reshape(m, d_model), proj.reshape(m, -1), ys.reshape(m, d_ssm), ya.reshape(m, d_att),
                 yc.reshape(m, d_rw), lw, alpha, 1, (q0 + 3 * d_att) // d_att, (r0 + 4 * d_rw) // d_rw)
    n_g = d_ssm // SSM_GROUP
    states = (k_rows, v_rows, h_re.reshape(n_b, n_g, SSM_STATE), h_im.reshape(n_b, n_g, SSM_STATE),
              _unpack_pairs(wkv), shift)
    return y.reshape(n_b, n_l, d_model), yb.reshape(n_b, n_l, d_model), states


def kernel(x_prompt, x_sample, cache_att_k, cache_att_v, state_ssm_re, state_ssm_im, state_rwkv, state_rwkv_shift, w_in, ssm_lam_re, ssm_lam_im, ssm_log_dt, ssm_b_re, ssm_b_im, ssm_c_re, ssm_c_im, ssm_d, ssm_w_glu, ssm_b_glu, att_rel_bias, rwkv_mu, rwkv_w0, rwkv_w1, rwkv_w2, rwkv_a0, rwkv_a1, rwkv_a2, rwkv_k_k, rwkv_k_a, rwkv_r_k, rwkv_lnx_g, rwkv_lnx_b, w_out, ln_g, ln_b):
    depth = w_in.shape[0]
    alpha = (2.0 * depth) ** 0.25
    y_p, y_s = x_prompt, x_sample
    yb_p, yb_s = x_prompt.astype(BF16), x_sample.astype(BF16)
    seg_lens = sorted({x_prompt.shape[1] // S5_SEGMENTS, x_sample.shape[1] // S5_SEGMENTS})
    p_st, s_st = [], []
    row = lambda a: a.reshape(1, -1)
    for l in range(depth):
        lw = {"w_in": w_in[l].astype(BF16), "w_glu": ssm_w_glu[l].astype(BF16), "b_glu": row(ssm_b_glu[l]),
              "att_rel_bias": att_rel_bias[l], "rwkv_mu": rwkv_mu[l], "rwkv_w0": row(rwkv_w0[l]),
              "rwkv_w1": rwkv_w1[l].astype(BF16), "rwkv_w2": rwkv_w2[l].astype(BF16), "rwkv_a0": row(rwkv_a0[l]),
              "rwkv_a1": rwkv_a1[l].astype(BF16), "rwkv_a2": rwkv_a2[l].astype(BF16), "rwkv_k_k": row(rwkv_k_k[l]),
              "rwkv_k_a": row(rwkv_k_a[l]), "rwkv_r_k": row(rwkv_r_k[l]), "rwkv_lnx_g": row(rwkv_lnx_g[l]),
              "rwkv_lnx_b": row(rwkv_lnx_b[l]), "w_out": w_out[l].astype(BF16), "ln_g": row(ln_g[l]),
              "ln_b": row(ln_b[l])}
        lw["bias_prompt"], lw["bias_step"] = _rel_bias(att_rel_bias[l], x_sample.shape[1], cache_att_k.shape[2])
        s5p = _s5_params(ssm_lam_re[l], ssm_lam_im[l], ssm_log_dt[l], ssm_b_re[l], ssm_b_im[l], ssm_c_re[l],
                         ssm_c_im[l], ssm_d[l], seg_lens)
        y_p, yb_p, st_p = _layer(y_p, yb_p, None, lw, s5p, alpha)
        y_s, yb_s, st_s = _layer(y_s, yb_s, (cache_att_k[l], cache_att_v[l], state_ssm_re[l], state_ssm_im[l],
                                             state_rwkv[l], state_rwkv_shift[l]), lw, s5p, alpha)
        p_st.append(st_p)
        s_st.append(st_s)
    stacked = lambda states, i: jnp.stack([st[i] for st in states], axis=0)
    return (y_p, y_s) + tuple(stacked(p_st, i) for i in range(6)) + tuple(stacked(s_st, i) for i in range(6))
```

```python
import functools
import math

import jax
import jax.numpy as jnp
import numpy as np
from jax import lax
from jax.experimental import pallas as pl
from jax.experimental.pallas import tpu as pltpu

F32 = jnp.float32
BF16 = jnp.bfloat16

LANES = 128
SUBLANES = 8
VMEM_LIMIT = 56 * 1024 * 1024

CHUNK = 64
LEFT_CHUNKS = 8
ATT_REACH = LEFT_CHUNKS * CHUNK
REL_CLIP = 128
HEAD_DIM = 64
SSM_GROUP = 16
SSM_STATE = 64
RWKV_LORA = 64
NEG_INF = -1e30
GN_EPS = 64e-5
LN_EPS = 1e-5

RWKV_CHUNK = 64
RWKV_GROUP = 8
Q_BLOCK = 4 * CHUNK
K_WINDOW = Q_BLOCK + ATT_REACH
S5_SEGMENTS = SUBLANES
S5_BLOCK = 512


def _cparams(*sem):
    return pltpu.CompilerParams(dimension_semantics=sem, vmem_limit_bytes=VMEM_LIMIT)


def _block_ones(n):
    r = lax.broadcasted_iota(jnp.int32, (n, n), 0) // HEAD_DIM
    c = lax.broadcasted_iota(jnp.int32, (n, n), 1) // HEAD_DIM
    return (r == c).astype(BF16)


def _head_sum(x, ones):
    hi = x.astype(BF16)
    lo = (x - hi.astype(F32)).astype(BF16)
    return jnp.dot(hi, ones, preferred_element_type=F32) + jnp.dot(lo, ones, preferred_element_type=F32)


def _matmul_kernel(x_ref, w_ref, o_ref, wb):
    @pl.when(pl.program_id(1) == 0)
    def _():
        wb[...] = w_ref[0].astype(BF16)

    o_ref[...] = jnp.dot(x_ref[...], wb[...], preferred_element_type=F32)


def _cast_kernel(x_ref, o_ref):
    o_ref[...] = x_ref[...].astype(o_ref.dtype)


def _to_bf16(x):
    m, k = x.shape
    tm = min(m, 1024)
    assert m % tm == 0
    return pl.pallas_call(
        _cast_kernel,
        grid=(m // tm,),
        in_specs=[pl.BlockSpec((tm, k), lambda i: (i, 0))],
        out_specs=pl.BlockSpec((tm, k), lambda i: (i, 0)),
        out_shape=jax.ShapeDtypeStruct((m, k), BF16),
        compiler_params=_cparams("parallel"),
        name="to_bf16",
    )(x)


def _in_proj(x, w, layer):
    m, k = x.shape
    n = w.shape[2]
    tm = min(m, 1024)
    tn = 768
    assert m % tm == 0 and n % tn == 0 and x.dtype == BF16
    return pl.pallas_call(
        _matmul_kernel,
        grid=(n // tn, m // tm),
        in_specs=[pl.BlockSpec((tm, k), lambda j, i: (i, 0)),
                  pl.BlockSpec((1, k, tn), lambda j, i: (layer, 0, j))],
        out_specs=pl.BlockSpec((tm, tn), lambda j, i: (i, j)),
        out_shape=jax.ShapeDtypeStruct((m, n), F32),
        scratch_shapes=[pltpu.VMEM((k, tn), BF16)],
        compiler_params=_cparams("parallel", "arbitrary"),
        name="in_proj",
    )(x, w)


def _s5_kernel(u_ref, wbr_ref, wbi_ref, ar_ref, ai_ref, pr_ref, pi_ref, h0r_ref, h0i_ref,
               wcr_ref, wci_ref, d_ref, y_ref, hr_out, hi_out, *scratch, seg):
    nlb = (len(scratch) - 1) // 2
    sr, si, up = scratch[:nlb], scratch[nlb:2 * nlb], scratch[2 * nlb]
    lb = lambda j: slice(j * LANES, (j + 1) * LANES)
    rows = lambda i: pl.ds(pl.multiple_of(i * S5_SEGMENTS, S5_SEGMENTS), S5_SEGMENTS)
    strided = lambda i: pl.ds(i, S5_SEGMENTS, stride=seg)
    unroll = min(seg, 8)

    def interleave(i, c):
        up[rows(i), :] = u_ref[0, strided(i), :]
        return c

    lax.fori_loop(0, seg, interleave, 0, unroll=unroll)
    u = up[...]
    ub = u.astype(BF16)
    for j in range(nlb):
        sr[j][...] = jnp.dot(ub, wbr_ref[0, :, lb(j)], preferred_element_type=F32)
        si[j][...] = jnp.dot(ub, wbi_ref[0, :, lb(j)], preferred_element_type=F32)
    bcast = lambda ref, j: jnp.broadcast_to(ref[0, :, lb(j)], (S5_SEGMENTS, LANES))
    ar = [bcast(ar_ref, j) for j in range(nlb)]
    ai = [bcast(ai_ref, j) for j in range(nlb)]

    def advance(i, hr, hi):
        nr = [ar[j] * hr[j] - ai[j] * hi[j] + sr[j][rows(i), :] for j in range(nlb)]
        ni = [ar[j] * hi[j] + ai[j] * hr[j] + si[j][rows(i), :] for j in range(nlb)]
        return nr, ni

    def body1(i, c):
        nr, ni = advance(i, c[:nlb], c[nlb:])
        return tuple(nr + ni)

    zero = jnp.zeros((S5_SEGMENTS, LANES), F32)
    ends = lax.fori_loop(0, seg, body1, (zero,) * (2 * nlb), unroll=unroll)

    starts = []
    for j in range(nlb):
        er, ei = ends[j], ends[nlb + j]
        pr, pi = pr_ref[0, :, lb(j)], pi_ref[0, :, lb(j)]
        cr, ci = h0r_ref[0, 0, :, lb(j)], h0i_ref[0, 0, :, lb(j)]
        start_r, start_i = [cr], [ci]
        for s in range(S5_SEGMENTS - 1):
            cr, ci = (pr * cr - pi * ci + er[s:s + 1, :], pr * ci + pi * cr + ei[s:s + 1, :])
            start_r.append(cr)
            start_i.append(ci)
        starts.append((jnp.concatenate(start_r, axis=0), jnp.concatenate(start_i, axis=0)))

    def body2(i, c):
        nr, ni = advance(i, c[:nlb], c[nlb:])
        for j in range(nlb):
            sr[j][rows(i), :] = nr[j]
            si[j][rows(i), :] = ni[j]
        return tuple(nr + ni)

    last = lax.fori_loop(0, seg, body2, tuple(s[0] for s in starts) + tuple(s[1] for s in starts), unroll=unroll)
    y = d_ref[...] * u
    for j in range(nlb):
        hr_out[0, 0, :, lb(j)] = last[j][S5_SEGMENTS - 1:, :]
        hi_out[0, 0, :, lb(j)] = last[nlb + j][S5_SEGMENTS - 1:, :]
        y += (jnp.dot(sr[j][...].astype(BF16), wcr_ref[0, lb(j), :], preferred_element_type=F32)
              - jnp.dot(si[j][...].astype(BF16), wci_ref[0, lb(j), :], preferred_element_type=F32))
    up[...] = y

    def deinterleave(i, c):
        y_ref[0, strided(i), :] = up[rows(i), :]
        return c

    lax.fori_loop(0, seg, deinterleave, 0, unroll=unroll)


def _s5(proj, prm, h0r, h0i):
    n_b, n_l, _ = proj.shape
    nblk = prm["wbr"].shape[0]
    cin = prm["wbr"].shape[1]
    seg = n_l // S5_SEGMENTS
    assert seg * S5_SEGMENTS == n_l and cin == LANES
    wspec = lambda shape: pl.BlockSpec((1,) + shape, lambda b, s: (s, 0, 0))
    st_spec = pl.BlockSpec((1, 1, 1, S5_BLOCK), lambda b, s: (b, s, 0, 0))
    return pl.pallas_call(
        functools.partial(_s5_kernel, seg=seg),
        grid=(n_b, nblk),
        in_specs=[pl.BlockSpec((1, n_l, cin), lambda b, s: (b, 0, s)),
                  wspec((cin, S5_BLOCK)), wspec((cin, S5_BLOCK)),
                  wspec((1, S5_BLOCK)), wspec((1, S5_BLOCK)), wspec((1, S5_BLOCK)), wspec((1, S5_BLOCK)),
                  st_spec, st_spec,
                  wspec((S5_BLOCK, cin)), wspec((S5_BLOCK, cin)),
                  pl.BlockSpec((1, cin), lambda b, s: (0, s))],
        out_specs=[pl.BlockSpec((1, n_l, cin), lambda b, s: (b, 0, s)), st_spec, st_spec],
        out_shape=[jax.ShapeDtypeStruct((n_b, n_l, nblk * cin), F32),
                   jax.ShapeDtypeStruct((n_b, nblk, 1, S5_BLOCK), F32),
                   jax.ShapeDtypeStruct((n_b, nblk, 1, S5_BLOCK), F32)],
        scratch_shapes=[pltpu.VMEM((n_l, LANES), F32)] * (2 * S5_BLOCK // LANES + 1),
        compiler_params=_cparams("parallel", "parallel"),
        name="s5_scan",
    )(proj, prm["wbr"], prm["wbi"], prm["ar"], prm["ai"], prm["pr"], prm["pi"], h0r, h0i,
      prm["wcr"], prm["wci"], prm["d"])


def _s5_params(lam_re, lam_im, log_dt, b_re, b_im, c_re, c_im, d_skip, seg_lens):
    n_g, n_p = lam_re.shape
    dt = jnp.exp(log_dt)[:, None]
    e = jnp.exp(lam_re * dt)
    ab_re, ab_im = e * jnp.cos(lam_im * dt), e * jnp.sin(lam_im * dt)
    den = lam_re * lam_re + lam_im * lam_im
    nr, ni = ab_re - 1.0, ab_im
    q_re = (nr * lam_re + ni * lam_im) / den
    q_im = (ni * lam_re - nr * lam_im) / den
    bb_re = q_re[..., None] * b_re - q_im[..., None] * b_im
    bb_im = q_re[..., None] * b_im + q_im[..., None] * b_re
    gpb = S5_BLOCK // n_p
    nblk = n_g // gpb
    eye = jnp.eye(gpb, dtype=F32)

    def pack_b(t):
        t = t.reshape(nblk, gpb, n_p, SSM_GROUP)
        return jnp.einsum("sgpc,gh->sgchp", t, eye).reshape(nblk, gpb * SSM_GROUP, gpb * n_p).astype(BF16)

    def pack_c(t):
        t = t.reshape(nblk, gpb, SSM_GROUP, n_p)
        return jnp.einsum("sgcp,gh->shpgc", t, eye).reshape(nblk, gpb * n_p, gpb * SSM_GROUP).astype(BF16)

    flat = lambda t: t.reshape(nblk, 1, gpb * n_p)
    out = {"wbr": pack_b(bb_re), "wbi": pack_b(bb_im), "wcr": pack_c(c_re), "wci": pack_c(c_im),
           "ar": flat(ab_re), "ai": flat(ab_im), "d": d_skip.reshape(1, -1), "pows": {}}
    for seg in seg_lens:
        assert seg & (seg - 1) == 0
        pr, pi = ab_re, ab_im
        for _ in range(int(math.log2(seg))):
            pr, pi = pr * pr - pi * pi, 2.0 * pr * pi
        out["pows"][seg] = (flat(pr), flat(pi))
    return out


def _att_prompt_kernel(q_ref, k_ref, v_ref, bias_ref, o_ref, kpad, vpad, *, n_l, scale):
    zpad = jnp.zeros((ATT_REACH, LANES), BF16)
    kpad[0:ATT_REACH, :] = zpad
    vpad[0:ATT_REACH, :] = zpad
    kpad[ATT_REACH:, :] = k_ref[0].astype(BF16)
    vpad[ATT_REACH:, :] = v_ref[0].astype(BF16)
    lane = lax.broadcasted_iota(jnp.int32, (1, LANES), 1)
    heads = (lane < HEAD_DIM, lane >= HEAD_DIM)
    col = lax.broadcasted_iota(jnp.int32, (1, K_WINDOW), 1)
    nt = (((1,), (1,)), ((), ()))

    def block(r0, n_masked):
        q = q_ref[0, pl.ds(r0, Q_BLOCK), :] * scale
        kw = kpad[pl.ds(r0, K_WINDOW), :]
        vw = vpad[pl.ds(r0, K_WINDOW), :]
        s = [lax.dot_general(jnp.where(hd, q, 0.0).astype(BF16), kw, nt, preferred_element_type=F32) + bias_ref[0, h]
             for h, hd in enumerate(heads)]
        if n_masked:
            s = [jnp.where(col < n_masked, NEG_INF, x) for x in s]
        m = [jnp.max(x, axis=-1, keepdims=True) for x in s]
        p = [jnp.exp(x - mx) for x, mx in zip(s, m)]
        l = [jnp.sum(x, axis=-1, keepdims=True) for x in p]
        o = [jnp.dot(x.astype(BF16), vw, preferred_element_type=F32) / lx for x, lx in zip(p, l)]
        o_ref[0, pl.ds(r0, Q_BLOCK), :] = jnp.where(heads[0], o[0], o[1])

    n_blocks = n_l // Q_BLOCK
    n_first = min(n_blocks, ATT_REACH // Q_BLOCK)
    for qb in range(n_first):
        block(qb * Q_BLOCK, ATT_REACH - qb * Q_BLOCK)

    def body(qb, carry):
        block(pl.multiple_of(qb * Q_BLOCK, Q_BLOCK), 0)
        return carry

    lax.fori_loop(n_first, n_blocks, body, 0, unroll=2)


def _att_prompt(proj, bias, q_col, k_col, v_col, n_pairs):
    n_b, n_l, _ = proj.shape
    assert n_l % Q_BLOCK == 0
    spec = lambda c0: pl.BlockSpec((1, n_l, LANES), lambda b, p: (b, 0, c0 + p))
    return pl.pallas_call(
        functools.partial(_att_prompt_kernel, n_l=n_l, scale=HEAD_DIM ** -0.5),
        grid=(n_b, n_pairs),
        in_specs=[spec(q_col), spec(k_col), spec(v_col),
                  pl.BlockSpec((1, 2, Q_BLOCK, K_WINDOW), lambda b, p: (p, 0, 0, 0))],
        out_specs=pl.BlockSpec((1, n_l, LANES), lambda b, p: (b, 0, p)),
        out_shape=jax.ShapeDtypeStruct((n_b, n_l, n_pairs * LANES), F32),
        scratch_shapes=[pltpu.VMEM((n_l + ATT_REACH, LANES), BF16), pltpu.VMEM((n_l + ATT_REACH, LANES), BF16)],
        compiler_params=_cparams("parallel", "parallel"),
        name="att_prompt",
    )(proj, proj, proj, bias)


def _bias_kernel(g_ref, bp_ref, bs_ref):
    g = g_ref[0]
    n_q, n_k = bp_ref.shape[1], bp_ref.shape[2]
    toep = pltpu.roll(jnp.broadcast_to(g, (n_q, g.shape[1])), 0, 1, stride=1, stride_axis=0)[:, :n_k]
    qc = lax.broadcasted_iota(jnp.int32, (n_q, n_k), 0) // CHUNK
    kc = lax.broadcasted_iota(jnp.int32, (n_q, n_k), 1) // CHUNK
    bp_ref[0] = jnp.where((kc >= qc) & (kc <= qc + LEFT_CHUNKS), toep, NEG_INF)
    n_s, n_ws = bs_ref.shape[1], bs_ref.shape[2]
    bs_ref[0] = pltpu.roll(jnp.broadcast_to(g, (n_s, g.shape[1])), 0, 1, stride=1, stride_axis=0)[:, :n_ws]


def _rel_bias(table, n_s, n_w):
    n_h = table.shape[0]
    assert n_w == ATT_REACH
    width = 1024
    assert width >= K_WINDOW + Q_BLOCK and width % LANES == 0
    d = np.arange(width)
    d = np.where(d < width - Q_BLOCK, d, d - width)
    idx = np.clip(ATT_REACH - d, -REL_CLIP, REL_CLIP) + REL_CLIP
    diag = table[:, idx].astype(F32).reshape(n_h, 1, width)
    return pl.pallas_call(
        _bias_kernel,
        grid=(n_h,),
        in_specs=[pl.BlockSpec((1, 1, width), lambda h: (h, 0, 0))],
        out_specs=[pl.BlockSpec((1, Q_BLOCK, K_WINDOW), lambda h: (h, 0, 0)),
                   pl.BlockSpec((1, n_s, n_w + n_s), lambda h: (h, 0, 0))],
        out_shape=[jax.ShapeDtypeStruct((n_h, Q_BLOCK, K_WINDOW), F32),
                   jax.ShapeDtypeStruct((n_h, n_s, n_w + n_s), F32)],
        compiler_params=_cparams("parallel"),
        name="rel_bias",
    )(diag)


def _att_step_kernel(q_ref, kn_ref, vn_ref, kc_ref, vc_ref, bc_ref, bn_ref, o_ref, *, n_h, scale):
    n_s = q_ref.shape[1]
    d = q_ref.shape[2]
    q = q_ref[0] * scale
    qt = jnp.concatenate([q] * n_h, axis=0)
    rh = lax.broadcasted_iota(jnp.int32, (n_h * n_s, d), 0) // n_s
    ch = lax.broadcasted_iota(jnp.int32, (n_h * n_s, d), 1) // HEAD_DIM
    qbd = jnp.where(rh == ch, qt, 0.0).astype(BF16)
    nt = (((1,), (1,)), ((), ()))
    s1 = lax.dot_general(qbd, kc_ref[0].astype(BF16), nt, preferred_element_type=F32) + bc_ref[...]
    s2 = lax.dot_general(qbd, kn_ref[0].astype(BF16), nt, preferred_element_type=F32) + bn_ref[...]
    m = jnp.maximum(jnp.max(s1, axis=-1, keepdims=True), jnp.max(s2, axis=-1, keepdims=True))
    p1 = jnp.exp(s1 - m)
    p2 = jnp.exp(s2 - m)
    l = jnp.sum(p1, axis=-1, keepdims=True) + jnp.sum(p2, axis=-1, keepdims=True)
    o = (jnp.dot(p1.astype(BF16), vc_ref[0].astype(BF16), preferred_element_type=F32)
         + jnp.dot(p2.astype(BF16), vn_ref[0].astype(BF16), preferred_element_type=F32)) / l
    first = lax.broadcasted_iota(jnp.int32, (1, LANES), 1) < HEAD_DIM
    for j in range(d // LANES):
        blk = o[:, j * LANES:(j + 1) * LANES]
        lo = blk[(2 * j) * n_s:(2 * j + 1) * n_s, :]
        hi = blk[(2 * j + 1) * n_s:(2 * j + 2) * n_s, :]
        o_ref[0, :, j * LANES:(j + 1) * LANES] = jnp.where(first, lo, hi)


def _att_step(proj, k_cache, v_cache, bias_c, bias_n, q_col, k_col, v_col, n_h):
    n_b, n_s, _ = proj.shape
    n_w, d = k_cache.shape[1], k_cache.shape[2]
    spec = lambda c0: pl.BlockSpec((1, n_s, d), lambda b: (b, 0, c0))
    cspec = pl.BlockSpec((1, n_w, d), lambda b: (b, 0, 0))
    return pl.pallas_call(
        functools.partial(_att_step_kernel, n_h=n_h, scale=HEAD_DIM ** -0.5),
        grid=(n_b,),
        in_specs=[spec(q_col), spec(k_col), spec(v_col), cspec, cspec,
                  pl.BlockSpec((n_h * n_s, n_w), lambda b: (0, 0)),
                  pl.BlockSpec((n_h * n_s, n_s), lambda b: (0, 0))],
        out_specs=pl.BlockSpec((1, n_s, d), lambda b: (b, 0, 0)),
        out_shape=jax.ShapeDtypeStruct((n_b, n_s, d), F32),
        compiler_params=_cparams("parallel"),
        name="att_step",
    )(proj, proj, proj, k_cache, v_cache, bias_c, bias_n)


def _rwkv_prep_kernel(r_ref, k_ref, v_ref, u_ref, sh_ref, mu_ref, w0_ref, w1_ref, w2_ref, a0_ref, a1_ref, a2_ref,
                      kk_ref, ka_ref, nkk_out, w_out, b_out, k_out, r_out, v_out, prev):
    j = pl.program_id(1)

    @pl.when(j == 0)
    def _():
        prev[...] = sh_ref[0]

    n_r = r_ref.shape[1]
    row0 = lax.broadcasted_iota(jnp.int32, (n_r, 1), 0) == 0

    def delta(x, i):
        sh = jnp.where(row0, prev[i:i + 1, :], pltpu.roll(x, 1, axis=0))
        prev[i:i + 1, :] = x[n_r - 1:, :]
        return sh - x

    rp, kp, vp, up = r_ref[0], k_ref[0], v_ref[0], u_ref[0]
    du = delta(up, 3)
    r = rp + delta(rp, 0) * mu_ref[0:1, :]
    k = kp + delta(kp, 1) * mu_ref[1:2, :]
    v = vp + delta(vp, 2) * mu_ref[2:3, :]
    xw = up + du * mu_ref[3:4, :]
    xa = up + du * mu_ref[4:5, :]
    dot = lambda a, b: jnp.dot(a.astype(BF16), b, preferred_element_type=F32)
    z = -(w0_ref[...] + dot(jnp.tanh(dot(xw, w1_ref[...])), w2_ref[...]))
    softplus = jnp.maximum(z, 0.0) + jnp.log1p(jnp.exp(-jnp.abs(z)))
    w = -softplus - 0.5
    a = jax.nn.sigmoid(a0_ref[...] + dot(dot(xa, a1_ref[...]), a2_ref[...]))
    kk = k * kk_ref[...]
    ss = _head_sum(kk * kk, _block_ones(kk.shape[1]))
    kk = kk * lax.rsqrt(jnp.maximum(ss, 1e-24))
    nkk_out[0] = -kk
    w_out[0] = -jnp.exp(w)
    b_out[0] = kk * a
    k_out[0] = k * (1.0 + (a - 1.0) * ka_ref[...])
    r_out[0] = r
    v_out[0] = v


def _rwkv_prep(proj, shift0, lw, col0):
    n_b, n_l, _ = proj.shape
    d = shift0.shape[2]
    tr = min(n_l, 512)
    assert n_l % tr == 0
    spec = lambda c: pl.BlockSpec((1, tr, d), lambda b, j: (b, j, c))
    full = lambda a: pl.BlockSpec(a.shape, lambda b, j: (0,) * a.ndim)
    ws = [lw["mu"], lw["w0"], lw["w1"], lw["w2"], lw["a0"], lw["a1"], lw["a2"], lw["k_k"], lw["k_a"]]
    ospec = pl.BlockSpec((1, tr, d), lambda b, j: (b, j, 0))
    return pl.pallas_call(
        _rwkv_prep_kernel,
        grid=(n_b, n_l // tr),
        in_specs=[spec(col0), spec(col0 + 1), spec(col0 + 2), spec(col0 + 3),
                  pl.BlockSpec((1, 4, d), lambda b, j: (b, 0, 0))] + [full(a) for a in ws],
        out_specs=[ospec] * 6,
        out_shape=[jax.ShapeDtypeStruct((n_b, n_l, d), F32)] * 6,
        scratch_shapes=[pltpu.VMEM((4, d), F32)],
        compiler_params=_cparams("parallel", "arbitrary"),
        name="rwkv_prep",
    )(proj, proj, proj, proj, shift0, *ws)


def _rwkv_rec_kernel(nkk_ref, lw_ref, b_ref, k_ref, r_ref, v_ref, s0_ref, rk_ref, g_ref, beta_ref,
                     y_ref, s_out, state, *, nbk, nch, c):
    tb = pl.program_id(2)

    @pl.when(tb == 0)
    def _():
        state[...] = s0_ref[:, 0]

    c2 = 2 * c
    lane = lax.broadcasted_iota(jnp.int32, (1, LANES), 1)
    m0 = (lane < HEAD_DIM).astype(F32)
    m1 = 1.0 - m0
    by_head = lambda x: jnp.concatenate([x * m0, x * m1], axis=0)
    mm = lambda a, b: jnp.dot(a.astype(BF16), b.astype(BF16), preferred_element_type=F32)
    mm_nt = lambda a, b: lax.dot_general(a.astype(BF16), b.astype(BF16), (((1,), (1,)), ((), ())),
                                         preferred_element_type=F32)
    ii = lambda shape, d: lax.broadcasted_iota(jnp.int32, shape, d)
    tri = (ii((c, c), 0) >= ii((c, c), 1)).astype(BF16)
    rr, cc = ii((c2, c2), 0), ii((c2, c2), 1)
    strict = (rr // c == cc // c) & (cc % c < rr % c)
    incl = ii((c, c2), 1) % c <= ii((c, c2), 0)
    eye_t = (rr == cc).astype(F32)
    eye_s = (ii((LANES, LANES), 0) == ii((LANES, LANES), 1)).astype(F32)
    probs = [(bb, ci) for ci in range(nch) for bb in range(nbk)]
    rows = lambda ref, q: ref[q[0], q[1] * c:(q[1] + 1) * c, :]
    each = lambda fn, *lists: [fn(*xs) for xs in zip(*lists)]

    lw = [rows(lw_ref, q) for q in probs]

    def cumsum(x):
        hi = x.astype(BF16)
        lo = (x - hi.astype(F32)).astype(BF16)
        return jnp.dot(tri, hi, preferred_element_type=F32) + jnp.dot(tri, lo, preferred_element_type=F32)

    cum = each(cumsum, lw)
    tot = [x[c - 1:c, :] for x in cum]
    g_c = [jnp.exp(x) for x in tot]
    a_t = each(lambda q, x, l: rows(nkk_ref, q) * jnp.exp(x - l), probs, cum, lw)
    r_t = each(lambda q, x: rows(r_ref, q) * jnp.exp(x), probs, cum)
    ginv = [jnp.exp(-x) for x in cum]
    ghat = each(lambda x, t: jnp.exp(t - x), cum, tot)
    kx = [rows(k_ref, q) for q in probs]
    bx = [rows(b_ref, q) for q in probs]
    a_bd = [by_head(x) for x in a_t]
    lhs = each(lambda a, r: jnp.concatenate([a, r], axis=0), a_bd, r_t)
    sc_k = each(lambda l, k, gi: mm_nt(l, by_head(k * gi)), lhs, kx, ginv)
    sc_b = each(lambda l, b, gi: mm_nt(l, by_head(b * gi)), lhs, bx, ginv)
    l_ak = [jnp.where(strict, x[:c2], 0.0) for x in sc_k]
    l_ab = [jnp.where(strict, x[:c2], 0.0) for x in sc_b]
    m_rk = [jnp.where(incl, x[c2:], 0.0) for x in sc_k]
    m_rb = [jnp.where(incl, x[c2:], 0.0) for x in sc_b]
    inv = [eye_t + x for x in l_ab]
    pw = l_ab
    for _ in range(int(math.log2(c)) - 1):
        pw = each(lambda x: mm(x, x), pw)
        inv = each(lambda p, x: p + mm(p, x), inv, pw)
    v_bd = [by_head(rows(v_ref, q)) for q in probs]
    x_in = each(lambda a, l, v: jnp.concatenate([a, mm(l, v)], axis=1), a_bd, l_ak, v_bd)
    uu = each(mm, inv, x_in)
    u_a = [x[:, :LANES] for x in uu]
    u_b = [x[:, LANES:] for x in uu]
    bh_t = each(lambda b, gh: by_head(b * gh).T, bx, ghat)
    kh_t = each(lambda k, gh: by_head(k * gh).T, kx, ghat)
    g_mat = each(lambda gc, bt, ua: eye_s * gc + mm(bt, ua), g_c, bh_t, u_a)
    h_mat = each(lambda kt, v, bt, ub: mm(kt, v) + mm(bt, ub), kh_t, v_bd, bh_t, u_b)
    y_a = each(lambda r, m, ua: r + mm(m, ua), r_t, m_rb, u_a)
    y_b = each(lambda mk, v, mb, ub: mm(mk, v) + mm(mb, ub), m_rk, v_bd, m_rb, u_b)

    s_cur = [state[bb] for bb in range(nbk)]
    ys = {}
    for i, (bb, ci) in enumerate(probs):
        ys[(bb, ci)] = mm(y_a[i], s_cur[bb]) + y_b[i]
        s_cur[bb] = mm(g_mat[i], s_cur[bb]) + h_mat[i]
    for bb in range(nbk):
        state[bb] = s_cur[bb]

    @pl.when(tb == pl.num_programs(2) - 1)
    def _():
        s_out[:, 0] = state[...]

    ones = _block_ones(LANES)
    inv_n = 1.0 / HEAD_DIM
    for bb in range(nbk):
        y = jnp.concatenate([ys[(bb, ci)] for ci in range(nch)], axis=0) if nch > 1 else ys[(bb, 0)]
        mean = _head_sum(y, ones) * inv_n
        yc = y - mean
        var = _head_sum(yc * yc, ones) * inv_n
        yn = yc * lax.rsqrt(var + GN_EPS) * g_ref[...] + beta_ref[...]
        bonus = _head_sum(r_ref[bb] * k_ref[bb] * rk_ref[...], ones) * v_ref[bb]
        y_ref[bb] = yn + bonus


def _rwkv_rec(prep, s0, r_k, lnx_g, lnx_b):
    n_b, n_l, d = prep[0].shape
    n_pairs = d // LANES
    c = min(n_l, RWKV_CHUNK)
    nch = min(n_l // c, RWKV_GROUP)
    nbk = min(n_b, RWKV_GROUP // nch)
    lb = nch * c
    assert n_l % lb == 0 and n_b % nbk == 0 and c & (c - 1) == 0
    spec = pl.BlockSpec((nbk, lb, LANES), lambda g, p, t: (g, t, p))
    sspec = pl.BlockSpec((nbk, 1, LANES, LANES), lambda g, p, t: (g, p, 0, 0))
    vec = pl.BlockSpec((1, LANES), lambda g, p, t: (0, p))
    return pl.pallas_call(
        functools.partial(_rwkv_rec_kernel, nbk=nbk, nch=nch, c=c),
        grid=(n_b // nbk, n_pairs, n_l // lb),
        in_specs=[spec] * 6 + [sspec, vec, vec, vec],
        out_specs=[spec, sspec],
        out_shape=[jax.ShapeDtypeStruct((n_b, n_l, d), F32),
                   jax.ShapeDtypeStruct((n_b, n_pairs, LANES, LANES), F32)],
        scratch_shapes=[pltpu.VMEM((nbk, LANES, LANES), F32)],
        compiler_params=_cparams("parallel", "parallel", "arbitrary"),
        name="rwkv_rec",
    )(*prep, s0, r_k, lnx_g, lnx_b)


def _pack_pairs(s):
    n_b, n_h, n, _ = s.shape
    st = jnp.swapaxes(s, -1, -2).reshape(n_b, n_h // 2, 2, n, n)
    z = jnp.zeros_like(st[:, :, 0])
    return jnp.concatenate([jnp.concatenate([st[:, :, 0], z], axis=-1),
                            jnp.concatenate([z, st[:, :, 1]], axis=-1)], axis=-2)


def _unpack_pairs(s):
    n_b, n_p, n2, _ = s.shape
    n = n2 // 2
    st = jnp.stack([s[:, :, :n, :n], s[:, :, n:, n:]], axis=2)
    return jnp.swapaxes(st, -1, -2).reshape(n_b, 2 * n_p, n, n)


def _mix_kernel(x_ref, ys_ref, gs_ref, ya_ref, ga_ref, yc_ref, gc_ref, wg_ref, bg_ref, wo_ref, lg_ref, lb_ref,
                y_ref, yb_ref, *, alpha, d_ssm, d_att):
    silu = lambda g: g * jax.nn.sigmoid(g)
    dot = lambda a, b: jnp.dot(a.astype(BF16), b, preferred_element_type=F32)
    ys = ys_ref[...]
    z = 0.5 * ys * (1.0 + jnp.tanh(math.sqrt(2.0 / math.pi) * (ys + 0.044715 * (ys * ys * ys))))
    m_s = z * jax.nn.sigmoid(dot(z, wg_ref[...]) + bg_ref[...]) * silu(gs_ref[...])
    m_a = ya_ref[...] * silu(ga_ref[...])
    m_c = yc_ref[...] * silu(gc_ref[...])
    out = (dot(m_s, wo_ref[0:d_ssm, :]) + dot(m_a, wo_ref[d_ssm:d_ssm + d_att, :])
           + dot(m_c, wo_ref[d_ssm + d_att:, :]))
    h = alpha * x_ref[...] + out
    mu = jnp.mean(h, axis=-1, keepdims=True)
    hc = h - mu
    var = jnp.mean(hc * hc, axis=-1, keepdims=True)
    y = hc * lax.rsqrt(var + LN_EPS) * lg_ref[...] + lb_ref[...]
    y_ref[...] = y
    yb_ref[...] = y.astype(BF16)


def _mix(x, proj, ys, ya, yc, lw, alpha, gs_col, ga_col, gc_col):
    m, d = x.shape
    d_ssm, d_att, d_rw = ys.shape[1], ya.shape[1], yc.shape[1]
    tm = min(m, 256)
    assert m % tm == 0
    row = lambda w, c: pl.BlockSpec((tm, w), lambda i: (i, c))
    full = lambda a: pl.BlockSpec(a.shape, lambda i: (0,) * a.ndim)
    ws = [lw["w_glu"], lw["b_glu"], lw["w_out"], lw["ln_g"], lw["ln_b"]]
    return pl.pallas_call(
        functools.partial(_mix_kernel, alpha=alpha, d_ssm=d_ssm, d_att=d_att),
        grid=(m // tm,),
        in_specs=[row(d, 0), row(d_ssm, 0), row(d_ssm, gs_col), row(d_att, 0), row(d_att, ga_col),
                  row(d_rw, 0), row(d_rw, gc_col)] + [full(a) for a in ws],
        out_specs=[row(d, 0), row(d, 0)],
        out_shape=[jax.ShapeDtypeStruct((m, d), F32), jax.ShapeDtypeStruct((m, d), BF16)],
        compiler_params=_cparams("parallel"),
        name="mix_out",
    )(x, ys, proj, ya, proj, yc, proj, *ws)


def _layer(x, xb, st, lw, s5p, alpha):
    n_b, n_l, d_model = x.shape
    d_ssm = lw["w_glu"].shape[0]
    n_h = lw["att_rel_bias"].shape[0]
    d_att = n_h * HEAD_DIM
    d_rw = lw["rwkv_w0"].shape[1]
    n_rh = d_rw // HEAD_DIM
    n_blk = d_ssm // LANES
    xb = _to_bf16(x.reshape(n_b * n_l, d_model)) if xb is None else xb.reshape(n_b * n_l, d_model)
    proj = _in_proj(xb, lw["w_in"], lw["layer"]).reshape(n_b, n_l, -1)
    q0 = 2 * d_ssm
    r0 = q0 + 4 * d_att
    assert q0 % d_att == 0 and r0 % d_rw == 0 and d_ssm == d_rw

    if st is None:
        zeros = jnp.zeros((n_b, n_blk, 1, S5_BLOCK), F32)
        h0r = h0i = zeros
        wkv0 = jnp.zeros((n_b, n_rh // 2, LANES, LANES), F32)
        shift0 = jnp.zeros((n_b, 4, d_rw), F32)
    else:
        k_cache, v_cache, h0r, h0i, wkv0, shift0 = st
        h0r = h0r.reshape(n_b, n_blk, 1, S5_BLOCK)
        h0i = h0i.reshape(n_b, n_blk, 1, S5_BLOCK)
        wkv0 = _pack_pairs(wkv0)
        shift0 = shift0.reshape(n_b, 4, d_rw)

    s5p = dict(s5p)
    s5p["pr"], s5p["pi"] = s5p["pows"][n_l // S5_SEGMENTS]
    ys, h_re, h_im = _s5(proj, s5p, h0r, h0i)

    n_keep = min(ATT_REACH, n_l)
    k_rows = proj[:, n_l - n_keep:, q0 + d_att:q0 + 2 * d_att].reshape(n_b, n_keep, n_h, HEAD_DIM)
    v_rows = proj[:, n_l - n_keep:, q0 + 2 * d_att:q0 + 3 * d_att].reshape(n_b, n_keep, n_h, HEAD_DIM)
    if st is None:
        pl0 = q0 // LANES
        bias_p = lw["bias_prompt"].reshape(n_h // 2, 2, Q_BLOCK, K_WINDOW)
        ya = _att_prompt(proj, bias_p, pl0, pl0 + d_att // LANES, pl0 + 2 * d_att // LANES, n_h // 2)
    else:
        n_w = k_cache.shape[1]
        bias_s = lw["bias_step"].reshape(n_h * n_l, n_w + n_l)
        bias_c, bias_n = bias_s[:, :n_w], bias_s[:, n_w:]
        c0 = q0 // d_att
        ya = _att_step(proj, k_cache.reshape(n_b, n_w, d_att), v_cache.reshape(n_b, n_w, d_att),
                       bias_c, bias_n, c0, c0 + 1, c0 + 2, n_h)

    rw = {"mu": lw["rwkv_mu"], "w0": lw["rwkv_w0"], "w1": lw["rwkv_w1"], "w2": lw["rwkv_w2"],
          "a0": lw["rwkv_a0"], "a1": lw["rwkv_a1"], "a2": lw["rwkv_a2"], "k_k": lw["rwkv_k_k"], "k_a": lw["rwkv_k_a"]}
    prep = _rwkv_prep(proj, shift0, rw, r0 // d_rw)
    yc, wkv = _rwkv_rec(prep, wkv0, lw["rwkv_r_k"], lw["rwkv_lnx_g"], lw["rwkv_lnx_b"])
    shift = proj[:, n_l - 1, r0:r0 + 4 * d_rw]

    m = n_b * n_l
    y, yb = _mix(x.reshape(m, d_model), proj.reshape(m, -1), ys.reshape(m, d_ssm), ya.reshape(m, d_att),
                 yc.reshape(m, d_rw), lw, alpha, 1, (q0 + 3 * d_att) // d_att, (r0 + 4 * d_rw) // d_rw)
    n_g = d_ssm // SSM_GROUP
    states = (k_rows, v_rows, h_re.reshape(n_b, n_g, SSM_STATE), h_im.reshape(n_b, n_g, SSM_STATE),
              _unpack_pairs(wkv), shift)
    return y.reshape(n_b, n_l, d_model), yb.reshape(n_b, n_l, d_model), states


def kernel(x_prompt, x_sample, cache_att_k, cache_att_v, state_ssm_re, state_ssm_im, state_rwkv, state_rwkv_shift, w_in, ssm_lam_re, ssm_lam_im, ssm_log_dt, ssm_b_re, ssm_b_im, ssm_c_re, ssm_c_im, ssm_d, ssm_w_glu, ssm_b_glu, att_rel_bias, rwkv_mu, rwkv_w0, rwkv_w1, rwkv_w2, rwkv_a0, rwkv_a1, rwkv_a2, rwkv_k_k, rwkv_k_a, rwkv_r_k, rwkv_lnx_g, rwkv_lnx_b, w_out, ln_g, ln_b):
    depth = w_in.shape[0]
    alpha = (2.0 * depth) ** 0.25
    y_p, y_s = x_prompt, x_sample
    yb_p = yb_s = None
    seg_lens = sorted({x_prompt.shape[1] // S5_SEGMENTS, x_sample.shape[1] // S5_SEGMENTS})
    p_st, s_st = [], []
    row = lambda a: a.reshape(1, -1)
    for l in range(depth):
        lw = {"w_in": w_in, "layer": l, "w_glu": ssm_w_glu[l].astype(BF16), "b_glu": row(ssm_b_glu[l]),
              "att_rel_bias": att_rel_bias[l], "rwkv_mu": rwkv_mu[l], "rwkv_w0": row(rwkv_w0[l]),
              "rwkv_w1": rwkv_w1[l].astype(BF16), "rwkv_w2": rwkv_w2[l].astype(BF16), "rwkv_a0": row(rwkv_a0[l]),
              "rwkv_a1": rwkv_a1[l].astype(BF16), "rwkv_a2": rwkv_a2[l].astype(BF16), "rwkv_k_k": row(rwkv_k_k[l]),
              "rwkv_k_a": row(rwkv_k_a[l]), "rwkv_r_k": row(rwkv_r_k[l]), "rwkv_lnx_g": row(rwkv_lnx_g[l]),
              "rwkv_lnx_b": row(rwkv_lnx_b[l]), "w_out": w_out[l].astype(BF16), "ln_g": row(ln_g[l]),
              "ln_b": row(ln_b[l])}
        lw["bias_prompt"], lw["bias_step"] = _rel_bias(att_rel_bias[l], x_sample.shape[1], cache_att_k.shape[2])
        s5p = _s5_params(ssm_lam_re[l], ssm_lam_im[l], ssm_log_dt[l], ssm_b_re[l], ssm_b_im[l], ssm_c_re[l],
                         ssm_c_im[l], ssm_d[l], seg_lens)
        y_p, yb_p, st_p = _layer(y_p, yb_p, None, lw, s5p, alpha)
        y_s, yb_s, st_s = _layer(y_s, yb_s, (cache_att_k[l], cache_att_v[l], state_ssm_re[l], state_ssm_im[l],
                                             state_rwkv[l], state_rwkv_shift[l]), lw, s5p, alpha)
        p_st.append(st_p)
        s_st.append(st_s)
    stacked = lambda states, i: jnp.stack([st[i] for st in states], axis=0)
    return (y_p, y_s) + tuple(stacked(p_st, i) for i in range(6)) + tuple(stacked(s_st, i) for i in range(6))
```

```python
import functools
import math

import jax
import jax.numpy as jnp
import numpy as np
from jax import lax
from jax.experimental import pallas as pl
from jax.experimental.pallas import tpu as pltpu

F32 = jnp.float32
BF16 = jnp.bfloat16

LANES = 128
SUBLANES = 8
VMEM_LIMIT = 56 * 1024 * 1024

CHUNK = 64
LEFT_CHUNKS = 8
ATT_REACH = LEFT_CHUNKS * CHUNK
REL_CLIP = 128
HEAD_DIM = 64
SSM_GROUP = 16
SSM_STATE = 64
RWKV_LORA = 64
NEG_INF = -1e30
GN_EPS = 64e-5
LN_EPS = 1e-5

RWKV_CHUNK = 64
RWKV_GROUP = 16
Q_BLOCK = 4 * CHUNK
K_WINDOW = Q_BLOCK + ATT_REACH
S5_SEGMENTS = SUBLANES
S5_BLOCK = 512


def _cparams(*sem):
    return pltpu.CompilerParams(dimension_semantics=sem, vmem_limit_bytes=VMEM_LIMIT)


def _block_ones(n):
    r = lax.broadcasted_iota(jnp.int32, (n, n), 0) // HEAD_DIM
    c = lax.broadcasted_iota(jnp.int32, (n, n), 1) // HEAD_DIM
    return (r == c).astype(BF16)


def _head_sum(x, ones):
    hi = x.astype(BF16)
    lo = (x - hi.astype(F32)).astype(BF16)
    return jnp.dot(hi, ones, preferred_element_type=F32) + jnp.dot(lo, ones, preferred_element_type=F32)


def _matmul_kernel(x_ref, w_ref, o_ref, wb):
    @pl.when(pl.program_id(1) == 0)
    def _():
        wb[...] = w_ref[0].astype(BF16)

    o_ref[...] = jnp.dot(x_ref[...], wb[...], preferred_element_type=F32)


def _cast_kernel(x_ref, o_ref):
    o_ref[...] = x_ref[...].astype(o_ref.dtype)


def _to_bf16(x):
    m, k = x.shape
    tm = min(m, 1024)
    assert m % tm == 0
    return pl.pallas_call(
        _cast_kernel,
        grid=(m // tm,),
        in_specs=[pl.BlockSpec((tm, k), lambda i: (i, 0))],
        out_specs=pl.BlockSpec((tm, k), lambda i: (i, 0)),
        out_shape=jax.ShapeDtypeStruct((m, k), BF16),
        compiler_params=_cparams("parallel"),
        name="to_bf16",
    )(x)


def _in_proj(x, w, layer):
    m, k = x.shape
    n = w.shape[2]
    tm = min(m, 1024)
    tn = 768
    assert m % tm == 0 and n % tn == 0 and x.dtype == BF16
    return pl.pallas_call(
        _matmul_kernel,
        grid=(n // tn, m // tm),
        in_specs=[pl.BlockSpec((tm, k), lambda j, i: (i, 0)),
                  pl.BlockSpec((1, k, tn), lambda j, i: (layer, 0, j))],
        out_specs=pl.BlockSpec((tm, tn), lambda j, i: (i, j)),
        out_shape=jax.ShapeDtypeStruct((m, n), F32),
        scratch_shapes=[pltpu.VMEM((k, tn), BF16)],
        compiler_params=_cparams("parallel", "arbitrary"),
        name="in_proj",
    )(x, w)


def _s5_kernel(u_ref, wbr_ref, wbi_ref, ar_ref, ai_ref, pr_ref, pi_ref, h0r_ref, h0i_ref,
               wcr_ref, wci_ref, d_ref, y_ref, hr_out, hi_out, *scratch, seg, nb):
    nlb = (len(scratch) - 1) // 2
    sr, si, up = scratch[:nlb], scratch[nlb:2 * nlb], scratch[2 * nlb]
    n_l = seg * S5_SEGMENTS
    lb = lambda j: slice(j * LANES, (j + 1) * LANES)
    aligned = lambda r: r if isinstance(r, int) else pl.multiple_of(r, S5_SEGMENTS)
    rows = lambda bb, i: pl.ds(aligned(bb * n_l + i * S5_SEGMENTS), S5_SEGMENTS)
    strided = lambda i: pl.ds(i, S5_SEGMENTS, stride=seg)
    streams = range(nb)

    def loop(body, init):
        if seg <= 8:
            for i in range(seg):
                init = body(i, init)
            return init
        return lax.fori_loop(0, seg, body, init, unroll=8)

    def interleave(i, c):
        for bb in streams:
            up[rows(bb, i), :] = u_ref[bb, strided(i), :]
        return c

    loop(interleave, 0)
    u = up[...]
    ub = u.astype(BF16)
    for j in range(nlb):
        sr[j][...] = jnp.dot(ub, wbr_ref[0, :, lb(j)], preferred_element_type=F32)
        si[j][...] = jnp.dot(ub, wbi_ref[0, :, lb(j)], preferred_element_type=F32)
    bcast = lambda ref, j: jnp.broadcast_to(ref[0, :, lb(j)], (S5_SEGMENTS, LANES))
    ar = [bcast(ar_ref, j) for j in range(nlb)]
    ai = [bcast(ai_ref, j) for j in range(nlb)]
    chains = [(bb, j) for bb in streams for j in range(nlb)]
    n_ch = len(chains)

    def advance(i, c, store):
        nr, ni = [], []
        for q, (bb, j) in enumerate(chains):
            hr, hi = c[q], c[n_ch + q]
            r = ar[j] * hr - ai[j] * hi + sr[j][rows(bb, i), :]
            m = ar[j] * hi + ai[j] * hr + si[j][rows(bb, i), :]
            if store:
                sr[j][rows(bb, i), :] = r
                si[j][rows(bb, i), :] = m
            nr.append(r)
            ni.append(m)
        return tuple(nr + ni)

    zero = jnp.zeros((S5_SEGMENTS, LANES), F32)
    ends = loop(lambda i, c: advance(i, c, False), (zero,) * (2 * n_ch))

    starts_r, starts_i = [], []
    for q, (bb, j) in enumerate(chains):
        er, ei = ends[q], ends[n_ch + q]
        pr, pi = pr_ref[0, :, lb(j)], pi_ref[0, :, lb(j)]
        cr, ci = h0r_ref[bb, 0, :, lb(j)], h0i_ref[bb, 0, :, lb(j)]
        start_r, start_i = [cr], [ci]
        for s in range(S5_SEGMENTS - 1):
            cr, ci = (pr * cr - pi * ci + er[s:s + 1, :], pr * ci + pi * cr + ei[s:s + 1, :])
            start_r.append(cr)
            start_i.append(ci)
        starts_r.append(jnp.concatenate(start_r, axis=0))
        starts_i.append(jnp.concatenate(start_i, axis=0))

    last = loop(lambda i, c: advance(i, c, True), tuple(starts_r + starts_i))
    y = d_ref[...] * u
    for q, (bb, j) in enumerate(chains):
        hr_out[bb, 0, :, lb(j)] = last[q][S5_SEGMENTS - 1:, :]
        hi_out[bb, 0, :, lb(j)] = last[n_ch + q][S5_SEGMENTS - 1:, :]
    for j in range(nlb):
        y += (jnp.dot(sr[j][...].astype(BF16), wcr_ref[0, lb(j), :], preferred_element_type=F32)
              - jnp.dot(si[j][...].astype(BF16), wci_ref[0, lb(j), :], preferred_element_type=F32))
    up[...] = y

    def deinterleave(i, c):
        for bb in streams:
            y_ref[bb, strided(i), :] = up[rows(bb, i), :]
        return c

    loop(deinterleave, 0)


def _s5(proj, prm, layer, h0r, h0i):
    n_b, n_l, _ = proj.shape
    nblk = prm["nblk"]
    cin = prm["wbr"].shape[1]
    b0 = layer * nblk
    seg = n_l // S5_SEGMENTS
    nb = n_b if seg <= 8 else 1
    assert seg * S5_SEGMENTS == n_l and cin == LANES
    wspec = lambda shape: pl.BlockSpec((1,) + shape, lambda b, s: (b0 + s, 0, 0))
    st_spec = pl.BlockSpec((nb, 1, 1, S5_BLOCK), lambda b, s: (b, s, 0, 0))
    return pl.pallas_call(
        functools.partial(_s5_kernel, seg=seg, nb=nb),
        grid=(n_b // nb, nblk),
        in_specs=[pl.BlockSpec((nb, n_l, cin), lambda b, s: (b, 0, s)),
                  wspec((cin, S5_BLOCK)), wspec((cin, S5_BLOCK)),
                  wspec((1, S5_BLOCK)), wspec((1, S5_BLOCK)), wspec((1, S5_BLOCK)), wspec((1, S5_BLOCK)),
                  st_spec, st_spec,
                  wspec((S5_BLOCK, cin)), wspec((S5_BLOCK, cin)),
                  pl.BlockSpec((1, cin), lambda b, s: (0, b0 + s))],
        out_specs=[pl.BlockSpec((nb, n_l, cin), lambda b, s: (b, 0, s)), st_spec, st_spec],
        out_shape=[jax.ShapeDtypeStruct((n_b, n_l, nblk * cin), F32),
                   jax.ShapeDtypeStruct((n_b, nblk, 1, S5_BLOCK), F32),
                   jax.ShapeDtypeStruct((n_b, nblk, 1, S5_BLOCK), F32)],
        scratch_shapes=[pltpu.VMEM((nb * n_l, LANES), F32)] * (2 * S5_BLOCK // LANES + 1),
        compiler_params=_cparams("parallel", "parallel"),
        name="s5_scan",
    )(proj, prm["wbr"], prm["wbi"], prm["ar"], prm["ai"], *prm["pows"][seg], h0r, h0i,
      prm["wcr"], prm["wci"], prm["d"])


def _s5_params(lam_re, lam_im, log_dt, b_re, b_im, c_re, c_im, d_skip, seg_lens):
    depth = lam_re.shape[0]
    merge = lambda t: t.reshape((-1,) + t.shape[2:])
    lam_re, lam_im, log_dt, b_re, b_im, c_re, c_im = map(merge, (lam_re, lam_im, log_dt, b_re, b_im, c_re, c_im))
    n_g, n_p = lam_re.shape
    dt = jnp.exp(log_dt)[:, None]
    e = jnp.exp(lam_re * dt)
    ab_re, ab_im = e * jnp.cos(lam_im * dt), e * jnp.sin(lam_im * dt)
    den = lam_re * lam_re + lam_im * lam_im
    nr, ni = ab_re - 1.0, ab_im
    q_re = (nr * lam_re + ni * lam_im) / den
    q_im = (ni * lam_re - nr * lam_im) / den
    bb_re = q_re[..., None] * b_re - q_im[..., None] * b_im
    bb_im = q_re[..., None] * b_im + q_im[..., None] * b_re
    gpb = S5_BLOCK // n_p
    nblk = n_g // gpb
    eye = jnp.eye(gpb, dtype=F32)

    def pack_b(t):
        t = t.reshape(nblk, gpb, n_p, SSM_GROUP)
        return jnp.einsum("sgpc,gh->sgchp", t, eye).reshape(nblk, gpb * SSM_GROUP, gpb * n_p).astype(BF16)

    def pack_c(t):
        t = t.reshape(nblk, gpb, SSM_GROUP, n_p)
        return jnp.einsum("sgcp,gh->shpgc", t, eye).reshape(nblk, gpb * n_p, gpb * SSM_GROUP).astype(BF16)

    flat = lambda t: t.reshape(nblk, 1, gpb * n_p)
    out = {"wbr": pack_b(bb_re), "wbi": pack_b(bb_im), "wcr": pack_c(c_re), "wci": pack_c(c_im),
           "ar": flat(ab_re), "ai": flat(ab_im), "d": d_skip.reshape(1, -1), "pows": {}, "nblk": nblk // depth}
    for seg in seg_lens:
        assert seg & (seg - 1) == 0
        pr, pi = ab_re, ab_im
        for _ in range(int(math.log2(seg))):
            pr, pi = pr * pr - pi * pi, 2.0 * pr * pi
        out["pows"][seg] = (flat(pr), flat(pi))
    return out


def _att_prompt_kernel(q_ref, k_ref, v_ref, bias_ref, o_ref, kpad, vpad, *, n_l, scale):
    zpad = jnp.zeros((ATT_REACH, LANES), BF16)
    kpad[0:ATT_REACH, :] = zpad
    vpad[0:ATT_REACH, :] = zpad
    kpad[ATT_REACH:, :] = k_ref[0].astype(BF16)
    vpad[ATT_REACH:, :] = v_ref[0].astype(BF16)
    lane = lax.broadcasted_iota(jnp.int32, (1, LANES), 1)
    heads = (lane < HEAD_DIM, lane >= HEAD_DIM)
    col = lax.broadcasted_iota(jnp.int32, (1, K_WINDOW), 1)
    nt = (((1,), (1,)), ((), ()))

    def block(r0, n_masked):
        q = q_ref[0, pl.ds(r0, Q_BLOCK), :] * scale
        kw = kpad[pl.ds(r0, K_WINDOW), :]
        vw = vpad[pl.ds(r0, K_WINDOW), :]
        s = [lax.dot_general(jnp.where(hd, q, 0.0).astype(BF16), kw, nt, preferred_element_type=F32) + bias_ref[0, h]
             for h, hd in enumerate(heads)]
        if n_masked:
            s = [jnp.where(col < n_masked, NEG_INF, x) for x in s]
        m = [jnp.max(x, axis=-1, keepdims=True) for x in s]
        p = [jnp.exp(x - mx) for x, mx in zip(s, m)]
        l = [jnp.sum(x, axis=-1, keepdims=True) for x in p]
        o = [jnp.dot(x.astype(BF16), vw, preferred_element_type=F32) / lx for x, lx in zip(p, l)]
        o_ref[0, pl.ds(r0, Q_BLOCK), :] = jnp.where(heads[0], o[0], o[1])

    n_blocks = n_l // Q_BLOCK
    n_first = min(n_blocks, ATT_REACH // Q_BLOCK)
    for qb in range(n_first):
        block(qb * Q_BLOCK, ATT_REACH - qb * Q_BLOCK)

    def body(qb, carry):
        block(pl.multiple_of(qb * Q_BLOCK, Q_BLOCK), 0)
        return carry

    lax.fori_loop(n_first, n_blocks, body, 0, unroll=2)


def _att_prompt(proj, bias, layer, q_col, k_col, v_col, n_pairs):
    n_b, n_l, _ = proj.shape
    assert n_l % Q_BLOCK == 0
    p0 = layer * n_pairs
    spec = lambda c0: pl.BlockSpec((1, n_l, LANES), lambda b, p: (b, 0, c0 + p))
    return pl.pallas_call(
        functools.partial(_att_prompt_kernel, n_l=n_l, scale=HEAD_DIM ** -0.5),
        grid=(n_b, n_pairs),
        in_specs=[spec(q_col), spec(k_col), spec(v_col),
                  pl.BlockSpec((1, 2, Q_BLOCK, K_WINDOW), lambda b, p: (p0 + p, 0, 0, 0))],
        out_specs=pl.BlockSpec((1, n_l, LANES), lambda b, p: (b, 0, p)),
        out_shape=jax.ShapeDtypeStruct((n_b, n_l, n_pairs * LANES), F32),
        scratch_shapes=[pltpu.VMEM((n_l + ATT_REACH, LANES), BF16), pltpu.VMEM((n_l + ATT_REACH, LANES), BF16)],
        compiler_params=_cparams("parallel", "parallel"),
        name="att_prompt",
    )(proj, proj, proj, bias)


def _bias_kernel(g_ref, bp_ref, bs_ref):
    g = g_ref[0]
    n_q, n_k = bp_ref.shape[1], bp_ref.shape[2]
    toep = pltpu.roll(jnp.broadcast_to(g, (n_q, g.shape[1])), 0, 1, stride=1, stride_axis=0)[:, :n_k]
    qc = lax.broadcasted_iota(jnp.int32, (n_q, n_k), 0) // CHUNK
    kc = lax.broadcasted_iota(jnp.int32, (n_q, n_k), 1) // CHUNK
    bp_ref[0] = jnp.where((kc >= qc) & (kc <= qc + LEFT_CHUNKS), toep, NEG_INF)
    n_s, n_ws = bs_ref.shape[1], bs_ref.shape[2]
    bs_ref[0] = pltpu.roll(jnp.broadcast_to(g, (n_s, g.shape[1])), 0, 1, stride=1, stride_axis=0)[:, :n_ws]


def _rel_bias(table, n_s, n_w):
    n_h = table.shape[0]
    assert n_w == ATT_REACH
    width = 1024
    assert width >= K_WINDOW + Q_BLOCK and width % LANES == 0
    d = np.arange(width)
    d = np.where(d < width - Q_BLOCK, d, d - width)
    idx = np.clip(ATT_REACH - d, -REL_CLIP, REL_CLIP) + REL_CLIP
    diag = table[:, idx].astype(F32).reshape(n_h, 1, width)
    return pl.pallas_call(
        _bias_kernel,
        grid=(n_h,),
        in_specs=[pl.BlockSpec((1, 1, width), lambda h: (h, 0, 0))],
        out_specs=[pl.BlockSpec((1, Q_BLOCK, K_WINDOW), lambda h: (h, 0, 0)),
                   pl.BlockSpec((1, n_s, n_w + n_s), lambda h: (h, 0, 0))],
        out_shape=[jax.ShapeDtypeStruct((n_h, Q_BLOCK, K_WINDOW), F32),
                   jax.ShapeDtypeStruct((n_h, n_s, n_w + n_s), F32)],
        compiler_params=_cparams("parallel"),
        name="rel_bias",
    )(diag)


def _att_step_kernel(q_ref, kn_ref, vn_ref, kc_ref, vc_ref, bc_ref, bn_ref, o_ref, *, n_h, scale):
    n_s = q_ref.shape[1]
    d = q_ref.shape[2]
    q = q_ref[0] * scale
    qt = jnp.concatenate([q] * n_h, axis=0)
    rh = lax.broadcasted_iota(jnp.int32, (n_h * n_s, d), 0) // n_s
    ch = lax.broadcasted_iota(jnp.int32, (n_h * n_s, d), 1) // HEAD_DIM
    qbd = jnp.where(rh == ch, qt, 0.0).astype(BF16)
    nt = (((1,), (1,)), ((), ()))
    s1 = lax.dot_general(qbd, kc_ref[0, 0].astype(BF16), nt, preferred_element_type=F32) + bc_ref[...]
    s2 = lax.dot_general(qbd, kn_ref[0].astype(BF16), nt, preferred_element_type=F32) + bn_ref[...]
    m = jnp.maximum(jnp.max(s1, axis=-1, keepdims=True), jnp.max(s2, axis=-1, keepdims=True))
    p1 = jnp.exp(s1 - m)
    p2 = jnp.exp(s2 - m)
    l = jnp.sum(p1, axis=-1, keepdims=True) + jnp.sum(p2, axis=-1, keepdims=True)
    o = (jnp.dot(p1.astype(BF16), vc_ref[0, 0].astype(BF16), preferred_element_type=F32)
         + jnp.dot(p2.astype(BF16), vn_ref[0].astype(BF16), preferred_element_type=F32)) / l
    first = lax.broadcasted_iota(jnp.int32, (1, LANES), 1) < HEAD_DIM
    for j in range(d // LANES):
        blk = o[:, j * LANES:(j + 1) * LANES]
        lo = blk[(2 * j) * n_s:(2 * j + 1) * n_s, :]
        hi = blk[(2 * j + 1) * n_s:(2 * j + 2) * n_s, :]
        o_ref[0, :, j * LANES:(j + 1) * LANES] = jnp.where(first, lo, hi)


def _att_step(proj, k_cache, v_cache, layer, bias_c, bias_n, q_col, k_col, v_col, n_h):
    n_b, n_s, _ = proj.shape
    n_w, d = k_cache.shape[2], k_cache.shape[3]
    spec = lambda c0: pl.BlockSpec((1, n_s, d), lambda b: (b, 0, c0))
    cspec = pl.BlockSpec((1, 1, n_w, d), lambda b: (layer, b, 0, 0))
    bspec = lambda a: pl.BlockSpec((None,) + a.shape[1:], lambda b: (layer, 0, 0))
    return pl.pallas_call(
        functools.partial(_att_step_kernel, n_h=n_h, scale=HEAD_DIM ** -0.5),
        grid=(n_b,),
        in_specs=[spec(q_col), spec(k_col), spec(v_col), cspec, cspec, bspec(bias_c), bspec(bias_n)],
        out_specs=pl.BlockSpec((1, n_s, d), lambda b: (b, 0, 0)),
        out_shape=jax.ShapeDtypeStruct((n_b, n_s, d), F32),
        compiler_params=_cparams("parallel"),
        name="att_step",
    )(proj, proj, proj, k_cache, v_cache, bias_c, bias_n)


def _rwkv_prep_kernel(r_ref, k_ref, v_ref, u_ref, sh_ref, mu_ref, w0_ref, w1_ref, w2_ref, a0_ref, a1_ref, a2_ref,
                      kk_ref, ka_ref, nkk_out, w_out, b_out, k_out, r_out, v_out, prev):
    j = pl.program_id(1)

    @pl.when(j == 0)
    def _():
        prev[...] = sh_ref[0]

    n_r = r_ref.shape[1]
    row0 = lax.broadcasted_iota(jnp.int32, (n_r, 1), 0) == 0

    def delta(x, i):
        sh = jnp.where(row0, prev[i:i + 1, :], pltpu.roll(x, 1, axis=0))
        prev[i:i + 1, :] = x[n_r - 1:, :]
        return sh - x

    rp, kp, vp, up = r_ref[0], k_ref[0], v_ref[0], u_ref[0]
    du = delta(up, 3)
    r = rp + delta(rp, 0) * mu_ref[0:1, :]
    k = kp + delta(kp, 1) * mu_ref[1:2, :]
    v = vp + delta(vp, 2) * mu_ref[2:3, :]
    xw = up + du * mu_ref[3:4, :]
    xa = up + du * mu_ref[4:5, :]
    dot = lambda a, b: jnp.dot(a.astype(BF16), b, preferred_element_type=F32)
    z = -(w0_ref[...] + dot(jnp.tanh(dot(xw, w1_ref[...])), w2_ref[...]))
    softplus = jnp.maximum(z, 0.0) + jnp.log1p(jnp.exp(-jnp.abs(z)))
    w = -softplus - 0.5
    a = jax.nn.sigmoid(a0_ref[...] + dot(dot(xa, a1_ref[...]), a2_ref[...]))
    kk = k * kk_ref[...]
    ss = _head_sum(kk * kk, _block_ones(kk.shape[1]))
    kk = kk * lax.rsqrt(jnp.maximum(ss, 1e-24))
    nkk_out[0] = -kk
    w_out[0] = -jnp.exp(w)
    b_out[0] = kk * a
    k_out[0] = k * (1.0 + (a - 1.0) * ka_ref[...])
    r_out[0] = r
    v_out[0] = v


def _layer_spec(a, layer):
    return pl.BlockSpec((None,) + a.shape[1:], lambda *_: (layer,) + (0,) * (a.ndim - 1))


def _rwkv_prep(proj, shift0, lw, col0):
    n_b, n_l, _ = proj.shape
    d = shift0.shape[2]
    tr = min(n_l, 512)
    assert n_l % tr == 0
    spec = lambda c: pl.BlockSpec((1, tr, d), lambda b, j: (b, j, c))
    full = lambda a: _layer_spec(a, lw["layer"])
    ws = [lw["rwkv_mu"], lw["rwkv_w0"], lw["rwkv_w1"], lw["rwkv_w2"], lw["rwkv_a0"], lw["rwkv_a1"], lw["rwkv_a2"],
          lw["rwkv_k_k"], lw["rwkv_k_a"]]
    ospec = pl.BlockSpec((1, tr, d), lambda b, j: (b, j, 0))
    return pl.pallas_call(
        _rwkv_prep_kernel,
        grid=(n_b, n_l // tr),
        in_specs=[spec(col0), spec(col0 + 1), spec(col0 + 2), spec(col0 + 3),
                  pl.BlockSpec((1, 4, d), lambda b, j: (b, 0, 0))] + [full(a) for a in ws],
        out_specs=[ospec] * 6,
        out_shape=[jax.ShapeDtypeStruct((n_b, n_l, d), F32)] * 6,
        scratch_shapes=[pltpu.VMEM((4, d), F32)],
        compiler_params=_cparams("parallel", "arbitrary"),
        name="rwkv_prep",
    )(proj, proj, proj, proj, shift0, *ws)


def _rwkv_rec_kernel(nkk_ref, lw_ref, b_ref, k_ref, r_ref, v_ref, s0_ref, rk_ref, g_ref, beta_ref,
                     y_ref, s_out, state, *, nbk, nch, c):
    tb = pl.program_id(2)

    @pl.when(tb == 0)
    def _():
        state[...] = s0_ref[:, 0]

    c2 = 2 * c
    lane = lax.broadcasted_iota(jnp.int32, (1, LANES), 1)
    m0 = (lane < HEAD_DIM).astype(F32)
    m1 = 1.0 - m0
    by_head = lambda x: jnp.concatenate([x * m0, x * m1], axis=0)
    mm = lambda a, b: jnp.dot(a.astype(BF16), b.astype(BF16), preferred_element_type=F32)
    mm_nt = lambda a, b: lax.dot_general(a.astype(BF16), b.astype(BF16), (((1,), (1,)), ((), ())),
                                         preferred_element_type=F32)
    ii = lambda shape, d: lax.broadcasted_iota(jnp.int32, shape, d)
    tri = (ii((c, c), 0) >= ii((c, c), 1)).astype(BF16)
    rr, cc = ii((c2, c2), 0), ii((c2, c2), 1)
    strict = (rr // c == cc // c) & (cc % c < rr % c)
    incl = ii((c, c2), 1) % c <= ii((c, c2), 0)
    eye_t = (rr == cc).astype(F32)
    eye_s = (ii((LANES, LANES), 0) == ii((LANES, LANES), 1)).astype(F32)
    probs = [(bb, ci) for ci in range(nch) for bb in range(nbk)]
    rows = lambda ref, q: ref[q[0], q[1] * c:(q[1] + 1) * c, :]
    each = lambda fn, *lists: [fn(*xs) for xs in zip(*lists)]

    lw = [rows(lw_ref, q) for q in probs]

    def cumsum(x):
        hi = x.astype(BF16)
        lo = (x - hi.astype(F32)).astype(BF16)
        return jnp.dot(tri, hi, preferred_element_type=F32) + jnp.dot(tri, lo, preferred_element_type=F32)

    cum = each(cumsum, lw)
    tot = [x[c - 1:c, :] for x in cum]
    g_c = [jnp.exp(x) for x in tot]
    a_t = each(lambda q, x, l: rows(nkk_ref, q) * jnp.exp(x - l), probs, cum, lw)
    r_t = each(lambda q, x: rows(r_ref, q) * jnp.exp(x), probs, cum)
    ginv = [jnp.exp(-x) for x in cum]
    ghat = each(lambda x, t: jnp.exp(t - x), cum, tot)
    kx = [rows(k_ref, q) for q in probs]
    bx = [rows(b_ref, q) for q in probs]
    a_bd = [by_head(x) for x in a_t]
    lhs = each(lambda a, r: jnp.concatenate([a, r], axis=0), a_bd, r_t)
    sc_k = each(lambda l, k, gi: mm_nt(l, by_head(k * gi)), lhs, kx, ginv)
    sc_b = each(lambda l, b, gi: mm_nt(l, by_head(b * gi)), lhs, bx, ginv)
    l_ak = [jnp.where(strict, x[:c2], 0.0) for x in sc_k]
    l_ab = [jnp.where(strict, x[:c2], 0.0) for x in sc_b]
    m_rk = [jnp.where(incl, x[c2:], 0.0) for x in sc_k]
    m_rb = [jnp.where(incl, x[c2:], 0.0) for x in sc_b]
    inv = [eye_t + x for x in l_ab]
    pw = l_ab
    for _ in range(int(math.log2(c)) - 1):
        pw = each(lambda x: mm(x, x), pw)
        inv = each(lambda p, x: p + mm(p, x), inv, pw)
    v_bd = [by_head(rows(v_ref, q)) for q in probs]
    x_in = each(lambda a, l, v: jnp.concatenate([a, mm(l, v)], axis=1), a_bd, l_ak, v_bd)
    uu = each(mm, inv, x_in)
    u_a = [x[:, :LANES] for x in uu]
    u_b = [x[:, LANES:] for x in uu]
    bh_t = each(lambda b, gh: by_head(b * gh).T, bx, ghat)
    kh_t = each(lambda k, gh: by_head(k * gh).T, kx, ghat)
    g_mat = each(lambda gc, bt, ua: eye_s * gc + mm(bt, ua), g_c, bh_t, u_a)
    h_mat = each(lambda kt, v, bt, ub: mm(kt, v) + mm(bt, ub), kh_t, v_bd, bh_t, u_b)
    y_a = each(lambda r, m, ua: r + mm(m, ua), r_t, m_rb, u_a)
    y_b = each(lambda mk, v, mb, ub: mm(mk, v) + mm(mb, ub), m_rk, v_bd, m_rb, u_b)

    s_cur = [state[bb] for bb in range(nbk)]
    ys = {}
    for i, (bb, ci) in enumerate(probs):
        ys[(bb, ci)] = mm(y_a[i], s_cur[bb]) + y_b[i]
        s_cur[bb] = mm(g_mat[i], s_cur[bb]) + h_mat[i]
    for bb in range(nbk):
        state[bb] = s_cur[bb]

    @pl.when(tb == pl.num_programs(2) - 1)
    def _():
        s_out[:, 0] = state[...]

    ones = _block_ones(LANES)
    inv_n = 1.0 / HEAD_DIM
    for bb in range(nbk):
        y = jnp.concatenate([ys[(bb, ci)] for ci in range(nch)], axis=0) if nch > 1 else ys[(bb, 0)]
        mean = _head_sum(y, ones) * inv_n
        yc = y - mean
        var = _head_sum(yc * yc, ones) * inv_n
        yn = yc * lax.rsqrt(var + GN_EPS) * g_ref[...] + beta_ref[...]
        bonus = _head_sum(r_ref[bb] * k_ref[bb] * rk_ref[...], ones) * v_ref[bb]
        y_ref[bb] = yn + bonus


def _rwkv_rec(prep, s0, layer, r_k, lnx_g, lnx_b):
    n_b, n_l, d = prep[0].shape
    n_pairs = d // LANES
    c = min(n_l, RWKV_CHUNK)
    nch = min(n_l // c, RWKV_GROUP // 2)
    nbk = min(n_b, RWKV_GROUP // nch)
    lb = nch * c
    assert n_l % lb == 0 and n_b % nbk == 0 and c & (c - 1) == 0
    spec = pl.BlockSpec((nbk, lb, LANES), lambda g, p, t: (g, t, p))
    sspec = pl.BlockSpec((nbk, 1, LANES, LANES), lambda g, p, t: (g, p, 0, 0))
    vec = pl.BlockSpec((None, 1, LANES), lambda g, p, t: (layer, 0, p))
    return pl.pallas_call(
        functools.partial(_rwkv_rec_kernel, nbk=nbk, nch=nch, c=c),
        grid=(n_b // nbk, n_pairs, n_l // lb),
        in_specs=[spec] * 6 + [sspec, vec, vec, vec],
        out_specs=[spec, sspec],
        out_shape=[jax.ShapeDtypeStruct((n_b, n_l, d), F32),
                   jax.ShapeDtypeStruct((n_b, n_pairs, LANES, LANES), F32)],
        scratch_shapes=[pltpu.VMEM((nbk, LANES, LANES), F32)],
        compiler_params=_cparams("parallel", "parallel", "arbitrary"),
        name="rwkv_rec",
    )(*prep, s0, r_k, lnx_g, lnx_b)


def _pack_pairs(s):
    n_b, n_h, n, _ = s.shape
    st = jnp.swapaxes(s, -1, -2).reshape(n_b, n_h // 2, 2, n, n)
    z = jnp.zeros_like(st[:, :, 0])
    return jnp.concatenate([jnp.concatenate([st[:, :, 0], z], axis=-1),
                            jnp.concatenate([z, st[:, :, 1]], axis=-1)], axis=-2)


def _unpack_pairs(s):
    n_b, n_p, n2, _ = s.shape
    n = n2 // 2
    st = jnp.stack([s[:, :, :n, :n], s[:, :, n:, n:]], axis=2)
    return jnp.swapaxes(st, -1, -2).reshape(n_b, 2 * n_p, n, n)


def _mix_kernel(x_ref, ys_ref, gs_ref, ya_ref, ga_ref, yc_ref, gc_ref, wg_ref, bg_ref, wo_ref, lg_ref, lb_ref,
                y_ref, yb_ref, *, alpha, d_ssm, d_att):
    silu = lambda g: g * jax.nn.sigmoid(g)
    dot = lambda a, b: jnp.dot(a.astype(BF16), b, preferred_element_type=F32)
    ys = ys_ref[...]
    z = 0.5 * ys * (1.0 + jnp.tanh(math.sqrt(2.0 / math.pi) * (ys + 0.044715 * (ys * ys * ys))))
    m_s = z * jax.nn.sigmoid(dot(z, wg_ref[...]) + bg_ref[...]) * silu(gs_ref[...])
    m_a = ya_ref[...] * silu(ga_ref[...])
    m_c = yc_ref[...] * silu(gc_ref[...])
    out = (dot(m_s, wo_ref[0:d_ssm, :]) + dot(m_a, wo_ref[d_ssm:d_ssm + d_att, :])
           + dot(m_c, wo_ref[d_ssm + d_att:, :]))
    h = alpha * x_ref[...] + out
    mu = jnp.mean(h, axis=-1, keepdims=True)
    hc = h - mu
    var = jnp.mean(hc * hc, axis=-1, keepdims=True)
    y = hc * lax.rsqrt(var + LN_EPS) * lg_ref[...] + lb_ref[...]
    y_ref[...] = y
    yb_ref[...] = y.astype(BF16)


def _mix(x, proj, ys, ya, yc, lw, alpha, gs_col, ga_col, gc_col):
    m, d = x.shape
    d_ssm, d_att, d_rw = ys.shape[1], ya.shape[1], yc.shape[1]
    tm = min(m, 256)
    assert m % tm == 0
    row = lambda w, c: pl.BlockSpec((tm, w), lambda i: (i, c))
    full = lambda a: _layer_spec(a, lw["layer"])
    ws = [lw["w_glu"], lw["b_glu"], lw["w_out"], lw["ln_g"], lw["ln_b"]]
    return pl.pallas_call(
        functools.partial(_mix_kernel, alpha=alpha, d_ssm=d_ssm, d_att=d_att),
        grid=(m // tm,),
        in_specs=[row(d, 0), row(d_ssm, 0), row(d_ssm, gs_col), row(d_att, 0), row(d_att, ga_col),
                  row(d_rw, 0), row(d_rw, gc_col)] + [full(a) for a in ws],
        out_specs=[row(d, 0), row(d, 0)],
        out_shape=[jax.ShapeDtypeStruct((m, d), F32), jax.ShapeDtypeStruct((m, d), BF16)],
        compiler_params=_cparams("parallel"),
        name="mix_out",
    )(x, ys, proj, ya, proj, yc, proj, *ws)


def _layer(x, xb, st, lw, s5p, alpha):
    n_b, n_l, d_model = x.shape
    d_ssm = lw["w_glu"].shape[1]
    n_h = lw["n_att_heads"]
    d_att = n_h * HEAD_DIM
    d_rw = lw["rwkv_w0"].shape[2]
    n_rh = d_rw // HEAD_DIM
    n_blk = d_ssm // LANES
    xb = _to_bf16(x.reshape(n_b * n_l, d_model)) if xb is None else xb.reshape(n_b * n_l, d_model)
    proj = _in_proj(xb, lw["w_in"], lw["layer"]).reshape(n_b, n_l, -1)
    q0 = 2 * d_ssm
    r0 = q0 + 4 * d_att
    assert q0 % d_att == 0 and r0 % d_rw == 0 and d_ssm == d_rw

    if st is None:
        zeros = jnp.zeros((n_b, n_blk, 1, S5_BLOCK), F32)
        h0r = h0i = zeros
        wkv0 = jnp.zeros((n_b, n_rh // 2, LANES, LANES), F32)
        shift0 = jnp.zeros((n_b, 4, d_rw), F32)
    else:
        k_cache, v_cache, h0r, h0i, wkv0, shift0 = st
        h0r = h0r.reshape(n_b, n_blk, 1, S5_BLOCK)
        h0i = h0i.reshape(n_b, n_blk, 1, S5_BLOCK)
        wkv0 = _pack_pairs(wkv0)
        shift0 = shift0.reshape(n_b, 4, d_rw)

    ys, h_re, h_im = _s5(proj, s5p, lw["layer"], h0r, h0i)

    n_keep = min(ATT_REACH, n_l)
    k_rows = proj[:, n_l - n_keep:, q0 + d_att:q0 + 2 * d_att].reshape(n_b, n_keep, n_h, HEAD_DIM)
    v_rows = proj[:, n_l - n_keep:, q0 + 2 * d_att:q0 + 3 * d_att].reshape(n_b, n_keep, n_h, HEAD_DIM)
    if st is None:
        pl0 = q0 // LANES
        ya = _att_prompt(proj, lw["bias_prompt"], lw["layer"], pl0, pl0 + d_att // LANES, pl0 + 2 * d_att // LANES,
                         n_h // 2)
    else:
        c0 = q0 // d_att
        ya = _att_step(proj, k_cache, v_cache, lw["layer"], lw["bias_cache"], lw["bias_new"], c0, c0 + 1, c0 + 2, n_h)

    prep = _rwkv_prep(proj, shift0, lw, r0 // d_rw)
    yc, wkv = _rwkv_rec(prep, wkv0, lw["layer"], lw["rwkv_r_k"], lw["rwkv_lnx_g"], lw["rwkv_lnx_b"])
    shift = proj[:, n_l - 1, r0:r0 + 4 * d_rw]

    m = n_b * n_l
    y, yb = _mix(x.reshape(m, d_model), proj.reshape(m, -1), ys.reshape(m, d_ssm), ya.reshape(m, d_att),
                 yc.reshape(m, d_rw), lw, alpha, 1, (q0 + 3 * d_att) // d_att, (r0 + 4 * d_rw) // d_rw)
    n_g = d_ssm // SSM_GROUP
    states = (k_rows, v_rows, h_re.reshape(n_b, n_g, SSM_STATE), h_im.reshape(n_b, n_g, SSM_STATE),
              _unpack_pairs(wkv), shift)
    return y.reshape(n_b, n_l, d_model), yb.reshape(n_b, n_l, d_model), states


def kernel(x_prompt, x_sample, cache_att_k, cache_att_v, state_ssm_re, state_ssm_im, state_rwkv, state_rwkv_shift, w_in, ssm_lam_re, ssm_lam_im, ssm_log_dt, ssm_b_re, ssm_b_im, ssm_c_re, ssm_c_im, ssm_d, ssm_w_glu, ssm_b_glu, att_rel_bias, rwkv_mu, rwkv_w0, rwkv_w1, rwkv_w2, rwkv_a0, rwkv_a1, rwkv_a2, rwkv_k_k, rwkv_k_a, rwkv_r_k, rwkv_lnx_g, rwkv_lnx_b, w_out, ln_g, ln_b):
    depth = w_in.shape[0]
    alpha = (2.0 * depth) ** 0.25
    y_p, y_s = x_prompt, x_sample
    yb_p = yb_s = None
    seg_lens = sorted({x_prompt.shape[1] // S5_SEGMENTS, x_sample.shape[1] // S5_SEGMENTS})
    p_st, s_st = [], []
    n_d, n_sb, n_w = cache_att_k.shape[:3]
    n_s = x_sample.shape[1]
    n_h = att_rel_bias.shape[1]
    k_cache = cache_att_k.reshape(n_d, n_sb, n_w, -1)
    v_cache = cache_att_v.reshape(n_d, n_sb, n_w, -1)
    row = lambda a: a.reshape(depth, 1, -1)
    bias_p, bias_s = _rel_bias(att_rel_bias.reshape(depth * n_h, -1), n_s, n_w)
    bias_s = bias_s.reshape(depth, n_h * n_s, n_w + n_s)
    params = {"w_in": w_in, "n_att_heads": n_h, "w_glu": ssm_w_glu.astype(BF16), "b_glu": row(ssm_b_glu),
              "rwkv_mu": rwkv_mu, "rwkv_w0": row(rwkv_w0), "rwkv_w1": rwkv_w1.astype(BF16),
              "rwkv_w2": rwkv_w2.astype(BF16), "rwkv_a0": row(rwkv_a0), "rwkv_a1": rwkv_a1.astype(BF16),
              "rwkv_a2": rwkv_a2.astype(BF16), "rwkv_k_k": row(rwkv_k_k), "rwkv_k_a": row(rwkv_k_a),
              "rwkv_r_k": row(rwkv_r_k), "rwkv_lnx_g": row(rwkv_lnx_g), "rwkv_lnx_b": row(rwkv_lnx_b),
              "w_out": w_out.astype(BF16), "ln_g": row(ln_g), "ln_b": row(ln_b),
              "bias_prompt": bias_p.reshape(depth * n_h // 2, 2, Q_BLOCK, K_WINDOW),
              "bias_cache": bias_s[:, :, :n_w], "bias_new": bias_s[:, :, n_w:]}
    s5p = _s5_params(ssm_lam_re, ssm_lam_im, ssm_log_dt, ssm_b_re, ssm_b_im, ssm_c_re, ssm_c_im, ssm_d, seg_lens)
    for l in range(depth):
        lw = dict(params, layer=l)
        y_p, yb_p, st_p = _layer(y_p, yb_p, None, lw, s5p, alpha)
        y_s, yb_s, st_s = _layer(y_s, yb_s, (k_cache, v_cache, state_ssm_re[l], state_ssm_im[l],
                                             state_rwkv[l], state_rwkv_shift[l]), lw, s5p, alpha)
        p_st.append(st_p)
        s_st.append(st_s)
    stacked = lambda states, i: jnp.stack([st[i] for st in states], axis=0)
    return (y_p, y_s) + tuple(stacked(p_st, i) for i in range(6)) + tuple(stacked(s_st, i) for i in range(6))
```

```python
import functools
import math

import jax
import jax.numpy as jnp
import numpy as np
from jax import lax
from jax.experimental import pallas as pl
from jax.experimental.pallas import tpu as pltpu

F32 = jnp.float32
BF16 = jnp.bfloat16

LANES = 128
SUBLANES = 8
VMEM_LIMIT = 56 * 1024 * 1024

CHUNK = 64
LEFT_CHUNKS = 8
ATT_REACH = LEFT_CHUNKS * CHUNK
REL_CLIP = 128
HEAD_DIM = 64
SSM_GROUP = 16
SSM_STATE = 64
RWKV_LORA = 64
NEG_INF = -1e30
GN_EPS = 64e-5
LN_EPS = 1e-5

RWKV_CHUNK = 64
RWKV_GROUP = 16
Q_BLOCK = 4 * CHUNK
K_WINDOW = Q_BLOCK + ATT_REACH
S5_SEGMENTS = SUBLANES
S5_BLOCK = 512


def _cparams(*sem):
    return pltpu.CompilerParams(dimension_semantics=sem, vmem_limit_bytes=VMEM_LIMIT)


def _block_ones(n):
    r = lax.broadcasted_iota(jnp.int32, (n, n), 0) // HEAD_DIM
    c = lax.broadcasted_iota(jnp.int32, (n, n), 1) // HEAD_DIM
    return (r == c).astype(BF16)


def _head_sum(x, ones):
    hi = x.astype(BF16)
    lo = (x - hi.astype(F32)).astype(BF16)
    return jnp.dot(hi, ones, preferred_element_type=F32) + jnp.dot(lo, ones, preferred_element_type=F32)


def _matmul_kernel(x_ref, xs_ref, w_ref, o_ref, os_ref, wb):
    @pl.when(pl.program_id(1) == 0)
    def _():
        wb[...] = w_ref[0].astype(BF16)

    o_ref[...] = jnp.dot(x_ref[...], wb[...], preferred_element_type=F32)

    @pl.when(pl.program_id(1) == pl.num_programs(1) - 1)
    def _():
        os_ref[...] = jnp.dot(xs_ref[...], wb[...], preferred_element_type=F32)


def _cast_kernel(x_ref, o_ref):
    o_ref[...] = x_ref[...].astype(o_ref.dtype)


def _to_bf16(x):
    m, k = x.shape
    tm = min(m, 1024)
    assert m % tm == 0
    return pl.pallas_call(
        _cast_kernel,
        grid=(m // tm,),
        in_specs=[pl.BlockSpec((tm, k), lambda i: (i, 0))],
        out_specs=pl.BlockSpec((tm, k), lambda i: (i, 0)),
        out_shape=jax.ShapeDtypeStruct((m, k), BF16),
        compiler_params=_cparams("parallel"),
        name="to_bf16",
    )(x)


def _in_proj(x, xs, w, layer):
    m, k = x.shape
    ms = xs.shape[0]
    n = w.shape[2]
    tm = min(m, 1024)
    tn = 768
    assert m % tm == 0 and n % tn == 0 and x.dtype == BF16 and xs.dtype == BF16
    return pl.pallas_call(
        _matmul_kernel,
        grid=(n // tn, m // tm),
        in_specs=[pl.BlockSpec((tm, k), lambda j, i: (i, 0)),
                  pl.BlockSpec((ms, k), lambda j, i: (0, 0)),
                  pl.BlockSpec((1, k, tn), lambda j, i: (layer, 0, j))],
        out_specs=[pl.BlockSpec((tm, tn), lambda j, i: (i, j)),
                   pl.BlockSpec((ms, tn), lambda j, i: (0, j))],
        out_shape=[jax.ShapeDtypeStruct((m, n), F32), jax.ShapeDtypeStruct((ms, n), F32)],
        scratch_shapes=[pltpu.VMEM((k, tn), BF16)],
        compiler_params=_cparams("parallel", "arbitrary"),
        name="in_proj",
    )(x, xs, w)


def _s5_kernel(u_ref, wb_ref, ar_ref, ai_ref, pr_ref, pi_ref, h0r_ref, h0i_ref,
               wc_ref, d_ref, y_ref, hr_out, hi_out, h, up, *, seg, nb):
    nlb = h.shape[1] // (2 * LANES)
    n_l = seg * S5_SEGMENTS
    lb = lambda j: slice(j * LANES, (j + 1) * LANES)
    re = lambda j: slice(2 * j * LANES, (2 * j + 1) * LANES)
    im = lambda j: slice((2 * j + 1) * LANES, (2 * j + 2) * LANES)
    aligned = lambda r: r if isinstance(r, int) else pl.multiple_of(r, S5_SEGMENTS)
    rows = lambda bb, i: pl.ds(aligned(bb * n_l + i * S5_SEGMENTS), S5_SEGMENTS)
    strided = lambda i: pl.ds(i, S5_SEGMENTS, stride=seg)
    streams = range(nb)

    def loop(body, init):
        if seg <= 8:
            for i in range(seg):
                init = body(i, init)
            return init
        return lax.fori_loop(0, seg, body, init, unroll=8)

    def interleave(i, c):
        for bb in streams:
            up[rows(bb, i), :] = u_ref[bb, strided(i), :]
        return c

    loop(interleave, 0)
    u = up[...]
    ub = u.astype(BF16)
    h[...] = jnp.dot(ub, wb_ref[0], preferred_element_type=F32)
    bcast = lambda ref, j: jnp.broadcast_to(ref[0, :, lb(j)], (S5_SEGMENTS, LANES))
    ar = [bcast(ar_ref, j) for j in range(nlb)]
    ai = [bcast(ai_ref, j) for j in range(nlb)]
    chains = [(bb, j) for bb in streams for j in range(nlb)]
    n_ch = len(chains)

    def advance(i, c, store):
        nr, ni = [], []
        for q, (bb, j) in enumerate(chains):
            hr, hi = c[q], c[n_ch + q]
            r = ar[j] * hr - ai[j] * hi + h[rows(bb, i), re(j)]
            m = ar[j] * hi + ai[j] * hr + h[rows(bb, i), im(j)]
            if store:
                h[rows(bb, i), re(j)] = r
                h[rows(bb, i), im(j)] = m
            nr.append(r)
            ni.append(m)
        return tuple(nr + ni)

    zero = jnp.zeros((S5_SEGMENTS, LANES), F32)
    ends = loop(lambda i, c: advance(i, c, False), (zero,) * (2 * n_ch))

    starts_r, starts_i = [], []
    for q, (bb, j) in enumerate(chains):
        er, ei = ends[q], ends[n_ch + q]
        pr, pi = pr_ref[0, :, lb(j)], pi_ref[0, :, lb(j)]
        cr, ci = h0r_ref[bb, 0, :, lb(j)], h0i_ref[bb, 0, :, lb(j)]
        start_r, start_i = [cr], [ci]
        for s in range(S5_SEGMENTS - 1):
            cr, ci = (pr * cr - pi * ci + er[s:s + 1, :], pr * ci + pi * cr + ei[s:s + 1, :])
            start_r.append(cr)
            start_i.append(ci)
        starts_r.append(jnp.concatenate(start_r, axis=0))
        starts_i.append(jnp.concatenate(start_i, axis=0))

    last = loop(lambda i, c: advance(i, c, True), tuple(starts_r + starts_i))
    y = d_ref[...] * u
    for q, (bb, j) in enumerate(chains):
        hr_out[bb, 0, :, lb(j)] = last[q][S5_SEGMENTS - 1:, :]
        hi_out[bb, 0, :, lb(j)] = last[n_ch + q][S5_SEGMENTS - 1:, :]
    y += jnp.dot(h[...].astype(BF16), wc_ref[0], preferred_element_type=F32)
    up[...] = y

    def deinterleave(i, c):
        for bb in streams:
            y_ref[bb, strided(i), :] = up[rows(bb, i), :]
        return c

    loop(deinterleave, 0)


def _s5(proj, prm, layer, h0r, h0i):
    n_b, n_l, _ = proj.shape
    nblk = prm["nblk"]
    cin = prm["wb"].shape[1]
    b0 = layer * nblk
    seg = n_l // S5_SEGMENTS
    nb = n_b if seg <= 8 else 1
    assert seg * S5_SEGMENTS == n_l and cin == LANES
    wspec = lambda shape: pl.BlockSpec((1,) + shape, lambda b, s: (b0 + s, 0, 0))
    st_spec = pl.BlockSpec((nb, 1, 1, S5_BLOCK), lambda b, s: (b, s, 0, 0))
    return pl.pallas_call(
        functools.partial(_s5_kernel, seg=seg, nb=nb),
        grid=(n_b // nb, nblk),
        in_specs=[pl.BlockSpec((nb, n_l, cin), lambda b, s: (b, 0, s)),
                  wspec((cin, 2 * S5_BLOCK)),
                  wspec((1, S5_BLOCK)), wspec((1, S5_BLOCK)), wspec((1, S5_BLOCK)), wspec((1, S5_BLOCK)),
                  st_spec, st_spec,
                  wspec((2 * S5_BLOCK, cin)),
                  pl.BlockSpec((1, cin), lambda b, s: (0, b0 + s))],
        out_specs=[pl.BlockSpec((nb, n_l, cin), lambda b, s: (b, 0, s)), st_spec, st_spec],
        out_shape=[jax.ShapeDtypeStruct((n_b, n_l, nblk * cin), F32),
                   jax.ShapeDtypeStruct((n_b, nblk, 1, S5_BLOCK), F32),
                   jax.ShapeDtypeStruct((n_b, nblk, 1, S5_BLOCK), F32)],
        scratch_shapes=[pltpu.VMEM((nb * n_l, 2 * S5_BLOCK), F32), pltpu.VMEM((nb * n_l, LANES), F32)],
        compiler_params=_cparams("parallel", "parallel"),
        name="s5_scan",
    )(proj, prm["wb"], prm["ar"], prm["ai"], *prm["pows"][seg], h0r, h0i, prm["wc"], prm["d"])


def _s5_params(lam_re, lam_im, log_dt, b_re, b_im, c_re, c_im, d_skip, seg_lens):
    depth = lam_re.shape[0]
    merge = lambda t: t.reshape((-1,) + t.shape[2:])
    lam_re, lam_im, log_dt, b_re, b_im, c_re, c_im = map(merge, (lam_re, lam_im, log_dt, b_re, b_im, c_re, c_im))
    n_g, n_p = lam_re.shape
    dt = jnp.exp(log_dt)[:, None]
    e = jnp.exp(lam_re * dt)
    ab_re, ab_im = e * jnp.cos(lam_im * dt), e * jnp.sin(lam_im * dt)
    den = lam_re * lam_re + lam_im * lam_im
    nr, ni = ab_re - 1.0, ab_im
    q_re = (nr * lam_re + ni * lam_im) / den
    q_im = (ni * lam_re - nr * lam_im) / den
    bb_re = q_re[..., None] * b_re - q_im[..., None] * b_im
    bb_im = q_re[..., None] * b_im + q_im[..., None] * b_re
    gpb = S5_BLOCK // n_p
    nblk = n_g // gpb
    eye = jnp.eye(gpb, dtype=F32)

    def pack_b(t):
        t = t.reshape(nblk, gpb, n_p, SSM_GROUP)
        return jnp.einsum("sgpc,gh->sgchp", t, eye).reshape(nblk, gpb * SSM_GROUP, gpb * n_p).astype(BF16)

    def pack_c(t):
        t = t.reshape(nblk, gpb, SSM_GROUP, n_p)
        return jnp.einsum("sgcp,gh->shpgc", t, eye).reshape(nblk, gpb * n_p, gpb * SSM_GROUP).astype(BF16)

    flat = lambda t: t.reshape(nblk, 1, gpb * n_p)
    nlb = gpb * n_p // LANES
    wb = jnp.stack([pack_b(bb_re).reshape(nblk, -1, nlb, LANES), pack_b(bb_im).reshape(nblk, -1, nlb, LANES)], axis=3)
    wc = jnp.stack([pack_c(c_re).reshape(nblk, nlb, LANES, -1), -pack_c(c_im).reshape(nblk, nlb, LANES, -1)], axis=2)
    out = {"wb": wb.reshape(nblk, -1, 2 * gpb * n_p), "wc": wc.reshape(nblk, 2 * gpb * n_p, -1),
           "ar": flat(ab_re), "ai": flat(ab_im), "d": d_skip.reshape(1, -1), "pows": {}, "nblk": nblk // depth}
    for seg in seg_lens:
        assert seg & (seg - 1) == 0
        pr, pi = ab_re, ab_im
        for _ in range(int(math.log2(seg))):
            pr, pi = pr * pr - pi * pi, 2.0 * pr * pi
        out["pows"][seg] = (flat(pr), flat(pi))
    return out


def _att_prompt_kernel(q_ref, k_ref, v_ref, bias_ref, o_ref, kpad, vpad, *, n_l, scale):
    zpad = jnp.zeros((ATT_REACH, LANES), BF16)
    kpad[0:ATT_REACH, :] = zpad
    vpad[0:ATT_REACH, :] = zpad
    kpad[ATT_REACH:, :] = k_ref[0].astype(BF16)
    vpad[ATT_REACH:, :] = v_ref[0].astype(BF16)
    lane = lax.broadcasted_iota(jnp.int32, (1, LANES), 1)
    heads = (lane < HEAD_DIM, lane >= HEAD_DIM)
    col = lax.broadcasted_iota(jnp.int32, (1, K_WINDOW), 1)
    nt = (((1,), (1,)), ((), ()))

    def block(r0, n_masked):
        q = q_ref[0, pl.ds(r0, Q_BLOCK), :] * scale
        kw = kpad[pl.ds(r0, K_WINDOW), :]
        vw = vpad[pl.ds(r0, K_WINDOW), :]
        s = [lax.dot_general(jnp.where(hd, q, 0.0).astype(BF16), kw, nt, preferred_element_type=F32) + bias_ref[0, h]
             for h, hd in enumerate(heads)]
        if n_masked:
            s = [jnp.where(col < n_masked, NEG_INF, x) for x in s]
        m = [jnp.max(x, axis=-1, keepdims=True) for x in s]
        p = [jnp.exp(x - mx) for x, mx in zip(s, m)]
        l = [jnp.sum(x, axis=-1, keepdims=True) for x in p]
        o = [jnp.dot(x.astype(BF16), vw, preferred_element_type=F32) / lx for x, lx in zip(p, l)]
        o_ref[0, pl.ds(r0, Q_BLOCK), :] = jnp.where(heads[0], o[0], o[1])

    n_blocks = n_l // Q_BLOCK
    n_first = min(n_blocks, ATT_REACH // Q_BLOCK)
    for qb in range(n_first):
        block(qb * Q_BLOCK, ATT_REACH - qb * Q_BLOCK)

    def body(qb, carry):
        block(pl.multiple_of(qb * Q_BLOCK, Q_BLOCK), 0)
        return carry

    lax.fori_loop(n_first, n_blocks, body, 0, unroll=2)


def _att_prompt(proj, bias, layer, q_col, k_col, v_col, n_pairs):
    n_b, n_l, _ = proj.shape
    assert n_l % Q_BLOCK == 0
    p0 = layer * n_pairs
    spec = lambda c0: pl.BlockSpec((1, n_l, LANES), lambda b, p: (b, 0, c0 + p))
    return pl.pallas_call(
        functools.partial(_att_prompt_kernel, n_l=n_l, scale=HEAD_DIM ** -0.5),
        grid=(n_b, n_pairs),
        in_specs=[spec(q_col), spec(k_col), spec(v_col),
                  pl.BlockSpec((1, 2, Q_BLOCK, K_WINDOW), lambda b, p: (p0 + p, 0, 0, 0))],
        out_specs=pl.BlockSpec((1, n_l, LANES), lambda b, p: (b, 0, p)),
        out_shape=jax.ShapeDtypeStruct((n_b, n_l, n_pairs * LANES), F32),
        scratch_shapes=[pltpu.VMEM((n_l + ATT_REACH, LANES), BF16), pltpu.VMEM((n_l + ATT_REACH, LANES), BF16)],
        compiler_params=_cparams("parallel", "parallel"),
        name="att_prompt",
    )(proj, proj, proj, bias)


def _bias_kernel(g_ref, bp_ref, bs_ref):
    g = g_ref[0]
    n_q, n_k = bp_ref.shape[1], bp_ref.shape[2]
    toep = pltpu.roll(jnp.broadcast_to(g, (n_q, g.shape[1])), 0, 1, stride=1, stride_axis=0)[:, :n_k]
    qc = lax.broadcasted_iota(jnp.int32, (n_q, n_k), 0) // CHUNK
    kc = lax.broadcasted_iota(jnp.int32, (n_q, n_k), 1) // CHUNK
    bp_ref[0] = jnp.where((kc >= qc) & (kc <= qc + LEFT_CHUNKS), toep, NEG_INF)
    n_s, n_ws = bs_ref.shape[1], bs_ref.shape[2]
    bs_ref[0] = pltpu.roll(jnp.broadcast_to(g, (n_s, g.shape[1])), 0, 1, stride=1, stride_axis=0)[:, :n_ws]


def _rel_bias(table, n_s, n_w):
    n_h = table.shape[0]
    assert n_w == ATT_REACH
    width = 1024
    assert width >= K_WINDOW + Q_BLOCK and width % LANES == 0
    d = np.arange(width)
    d = np.where(d < width - Q_BLOCK, d, d - width)
    idx = np.clip(ATT_REACH - d, -REL_CLIP, REL_CLIP) + REL_CLIP
    diag = table[:, idx].astype(F32).reshape(n_h, 1, width)
    return pl.pallas_call(
        _bias_kernel,
        grid=(n_h,),
        in_specs=[pl.BlockSpec((1, 1, width), lambda h: (h, 0, 0))],
        out_specs=[pl.BlockSpec((1, Q_BLOCK, K_WINDOW), lambda h: (h, 0, 0)),
                   pl.BlockSpec((1, n_s, n_w + n_s), lambda h: (h, 0, 0))],
        out_shape=[jax.ShapeDtypeStruct((n_h, Q_BLOCK, K_WINDOW), F32),
                   jax.ShapeDtypeStruct((n_h, n_s, n_w + n_s), F32)],
        compiler_params=_cparams("parallel"),
        name="rel_bias",
    )(diag)


def _att_step_kernel(q_ref, kn_ref, vn_ref, kc_ref, vc_ref, bc_ref, bn_ref, o_ref, *, n_h, scale):
    n_s = q_ref.shape[1]
    d = q_ref.shape[2]
    q = q_ref[0] * scale
    qt = jnp.concatenate([q] * n_h, axis=0)
    rh = lax.broadcasted_iota(jnp.int32, (n_h * n_s, d), 0) // n_s
    ch = lax.broadcasted_iota(jnp.int32, (n_h * n_s, d), 1) // HEAD_DIM
    qbd = jnp.where(rh == ch, qt, 0.0).astype(BF16)
    nt = (((1,), (1,)), ((), ()))
    s1 = lax.dot_general(qbd, kc_ref[0, 0].astype(BF16), nt, preferred_element_type=F32) + bc_ref[...]
    s2 = lax.dot_general(qbd, kn_ref[0].astype(BF16), nt, preferred_element_type=F32) + bn_ref[...]
    m = jnp.maximum(jnp.max(s1, axis=-1, keepdims=True), jnp.max(s2, axis=-1, keepdims=True))
    p1 = jnp.exp(s1 - m)
    p2 = jnp.exp(s2 - m)
    l = jnp.sum(p1, axis=-1, keepdims=True) + jnp.sum(p2, axis=-1, keepdims=True)
    o = (jnp.dot(p1.astype(BF16), vc_ref[0, 0].astype(BF16), preferred_element_type=F32)
         + jnp.dot(p2.astype(BF16), vn_ref[0].astype(BF16), preferred_element_type=F32)) / l
    first = lax.broadcasted_iota(jnp.int32, (1, LANES), 1) < HEAD_DIM
    for j in range(d // LANES):
        blk = o[:, j * LANES:(j + 1) * LANES]
        lo = blk[(2 * j) * n_s:(2 * j + 1) * n_s, :]
        hi = blk[(2 * j + 1) * n_s:(2 * j + 2) * n_s, :]
        o_ref[0, :, j * LANES:(j + 1) * LANES] = jnp.where(first, lo, hi)


def _att_step(proj, k_cache, v_cache, layer, bias_c, bias_n, q_col, k_col, v_col, n_h):
    n_b, n_s, _ = proj.shape
    n_w, d = k_cache.shape[2], k_cache.shape[3]
    spec = lambda c0: pl.BlockSpec((1, n_s, d), lambda b: (b, 0, c0))
    cspec = pl.BlockSpec((1, 1, n_w, d), lambda b: (layer, b, 0, 0))
    bspec = lambda a: pl.BlockSpec((None,) + a.shape[1:], lambda b: (layer, 0, 0))
    return pl.pallas_call(
        functools.partial(_att_step_kernel, n_h=n_h, scale=HEAD_DIM ** -0.5),
        grid=(n_b,),
        in_specs=[spec(q_col), spec(k_col), spec(v_col), cspec, cspec, bspec(bias_c), bspec(bias_n)],
        out_specs=pl.BlockSpec((1, n_s, d), lambda b: (b, 0, 0)),
        out_shape=jax.ShapeDtypeStruct((n_b, n_s, d), F32),
        compiler_params=_cparams("parallel"),
        name="att_step",
    )(proj, proj, proj, k_cache, v_cache, bias_c, bias_n)


def _rwkv_prep_kernel(r_ref, k_ref, v_ref, u_ref, sh_ref, mu_ref, w0_ref, w1_ref, w2_ref, a0_ref, a1_ref, a2_ref,
                      kk_ref, ka_ref, nkk_out, w_out, b_out, k_out, r_out, v_out, prev):
    j = pl.program_id(1)

    @pl.when(j == 0)
    def _():
        prev[...] = sh_ref[0]

    n_r = r_ref.shape[1]
    row0 = lax.broadcasted_iota(jnp.int32, (n_r, 1), 0) == 0

    def delta(x, i):
        sh = jnp.where(row0, prev[i:i + 1, :], pltpu.roll(x, 1, axis=0))
        prev[i:i + 1, :] = x[n_r - 1:, :]
        return sh - x

    rp, kp, vp, up = r_ref[0], k_ref[0], v_ref[0], u_ref[0]
    du = delta(up, 3)
    r = rp + delta(rp, 0) * mu_ref[0:1, :]
    k = kp + delta(kp, 1) * mu_ref[1:2, :]
    v = vp + delta(vp, 2) * mu_ref[2:3, :]
    xw = up + du * mu_ref[3:4, :]
    xa = up + du * mu_ref[4:5, :]
    dot = lambda a, b: jnp.dot(a.astype(BF16), b, preferred_element_type=F32)
    z = -(w0_ref[...] + dot(jnp.tanh(dot(xw, w1_ref[...])), w2_ref[...]))
    softplus = jnp.maximum(z, 0.0) + jnp.log(1.0 + jnp.exp(-jnp.abs(z)))
    w = -softplus - 0.5
    a = jax.nn.sigmoid(a0_ref[...] + dot(dot(xa, a1_ref[...]), a2_ref[...]))
    kk = k * kk_ref[...]
    ss = _head_sum(kk * kk, _block_ones(kk.shape[1]))
    kk = kk * lax.rsqrt(jnp.maximum(ss, 1e-24))
    nkk_out[0] = -kk
    w_out[0] = -jnp.exp(w)
    b_out[0] = kk * a
    k_out[0] = k * (1.0 + (a - 1.0) * ka_ref[...])
    r_out[0] = r
    v_out[0] = v


def _layer_spec(a, layer):
    return pl.BlockSpec((None,) + a.shape[1:], lambda *_: (layer,) + (0,) * (a.ndim - 1),
                        pipeline_mode=pl.Buffered(1))


def _rwkv_prep(proj, shift0, lw, col0):
    n_b, n_l, _ = proj.shape
    d = shift0.shape[2]
    tr = min(n_l, 512)
    assert n_l % tr == 0
    spec = lambda c: pl.BlockSpec((1, tr, d), lambda b, j: (b, j, c))
    full = lambda a: _layer_spec(a, lw["layer"])
    ws = [lw["rwkv_mu"], lw["rwkv_w0"], lw["rwkv_w1"], lw["rwkv_w2"], lw["rwkv_a0"], lw["rwkv_a1"], lw["rwkv_a2"],
          lw["rwkv_k_k"], lw["rwkv_k_a"]]
    ospec = pl.BlockSpec((1, tr, d), lambda b, j: (b, j, 0))
    return pl.pallas_call(
        _rwkv_prep_kernel,
        grid=(n_b, n_l // tr),
        in_specs=[spec(col0), spec(col0 + 1), spec(col0 + 2), spec(col0 + 3),
                  pl.BlockSpec((1, 4, d), lambda b, j: (b, 0, 0))] + [full(a) for a in ws],
        out_specs=[ospec] * 6,
        out_shape=[jax.ShapeDtypeStruct((n_b, n_l, d), F32)] * 6,
        scratch_shapes=[pltpu.VMEM((4, d), F32)],
        compiler_params=_cparams("parallel", "arbitrary"),
        name="rwkv_prep",
    )(proj, proj, proj, proj, shift0, *ws)


def _rwkv_rec_kernel(nkk_ref, lw_ref, b_ref, k_ref, r_ref, v_ref, s0_ref, rk_ref, g_ref, beta_ref,
                     y_ref, s_out, state, *, nbk, nch, c):
    tb = pl.program_id(2)

    @pl.when(tb == 0)
    def _():
        state[...] = s0_ref[:, 0]

    c2 = 2 * c
    lane = lax.broadcasted_iota(jnp.int32, (1, LANES), 1)
    m0 = (lane < HEAD_DIM).astype(F32)
    m1 = 1.0 - m0
    by_head = lambda x: jnp.concatenate([x * m0, x * m1], axis=0)
    mm = lambda a, b: jnp.dot(a.astype(BF16), b.astype(BF16), preferred_element_type=F32)
    mm_nt = lambda a, b: lax.dot_general(a.astype(BF16), b.astype(BF16), (((1,), (1,)), ((), ())),
                                         preferred_element_type=F32)
    ii = lambda shape, d: lax.broadcasted_iota(jnp.int32, shape, d)
    tri = (ii((c, c), 0) >= ii((c, c), 1)).astype(BF16)
    rr, cc = ii((c2, c2), 0), ii((c2, c2), 1)
    strict = (rr // c == cc // c) & (cc % c < rr % c)
    incl = ii((c, c2), 1) % c <= ii((c, c2), 0)
    eye_t = (rr == cc).astype(F32)
    eye_s = (ii((LANES, LANES), 0) == ii((LANES, LANES), 1)).astype(F32)
    probs = [(bb, ci) for ci in range(nch) for bb in range(nbk)]
    rows = lambda ref, q: ref[q[0], q[1] * c:(q[1] + 1) * c, :]
    each = lambda fn, *lists: [fn(*xs) for xs in zip(*lists)]

    lw = [rows(lw_ref, q) for q in probs]

    def cumsum(x):
        hi = x.astype(BF16)
        lo = (x - hi.astype(F32)).astype(BF16)
        return jnp.dot(tri, hi, preferred_element_type=F32) + jnp.dot(tri, lo, preferred_element_type=F32)

    cum = each(cumsum, lw)
    tot = [x[c - 1:c, :] for x in cum]
    g_c = [jnp.exp(x) for x in tot]
    a_t = each(lambda q, x, l: rows(nkk_ref, q) * jnp.exp(x - l), probs, cum, lw)
    r_t = each(lambda q, x: rows(r_ref, q) * jnp.exp(x), probs, cum)
    ginv = [jnp.exp(-x) for x in cum]
    ghat = each(lambda x, t: jnp.exp(t - x), cum, tot)
    kx = [rows(k_ref, q) for q in probs]
    bx = [rows(b_ref, q) for q in probs]
    a_bd = [by_head(x) for x in a_t]
    lhs = each(lambda a, r: jnp.concatenate([a, r], axis=0), a_bd, r_t)
    sc_k = each(lambda l, k, gi: mm_nt(l, by_head(k * gi)), lhs, kx, ginv)
    sc_b = each(lambda l, b, gi: mm_nt(l, by_head(b * gi)), lhs, bx, ginv)
    l_ak = [jnp.where(strict, x[:c2], 0.0) for x in sc_k]
    l_ab = [jnp.where(strict, x[:c2], 0.0) for x in sc_b]
    m_rk = [jnp.where(incl, x[c2:], 0.0) for x in sc_k]
    m_rb = [jnp.where(incl, x[c2:], 0.0) for x in sc_b]
    col_t = ii((1, c2), 1)
    t0, t1 = (col_t < c).astype(F32), (col_t >= c).astype(F32)
    blockdiag = lambda x: jnp.concatenate([x * t0, x * t1], axis=0)
    pw = [x[:c] + x[c:] for x in l_ab]
    inv = [x + eye_t[:c] + eye_t[c:] for x in pw]
    for _ in range(int(math.log2(c)) - 1):
        pw = each(lambda x: mm(x, blockdiag(x)), pw)
        inv = each(lambda p, x: p + mm(p, blockdiag(x)), inv, pw)
    inv = [blockdiag(x) for x in inv]
    v_bd = [by_head(rows(v_ref, q)) for q in probs]
    x_in = each(lambda a, l, v: jnp.concatenate([a, mm(l, v)], axis=1), a_bd, l_ak, v_bd)
    uu = each(mm, inv, x_in)
    u_a = [x[:, :LANES] for x in uu]
    u_b = [x[:, LANES:] for x in uu]
    bh_t = each(lambda b, gh: by_head(b * gh).T, bx, ghat)
    kh_t = each(lambda k, gh: by_head(k * gh).T, kx, ghat)
    g_mat = each(lambda gc, bt, ua: eye_s * gc + mm(bt, ua), g_c, bh_t, u_a)
    h_mat = each(lambda kt, v, bt, ub: mm(kt, v) + mm(bt, ub), kh_t, v_bd, bh_t, u_b)
    y_a = each(lambda r, m, ua: r + mm(m, ua), r_t, m_rb, u_a)
    y_b = each(lambda mk, v, mb, ub: mm(mk, v) + mm(mb, ub), m_rk, v_bd, m_rb, u_b)

    s_cur = [state[bb] for bb in range(nbk)]
    ys = {}
    for i, (bb, ci) in enumerate(probs):
        ys[(bb, ci)] = mm(y_a[i], s_cur[bb]) + y_b[i]
        s_cur[bb] = mm(g_mat[i], s_cur[bb]) + h_mat[i]
    for bb in range(nbk):
        state[bb] = s_cur[bb]

    @pl.when(tb == pl.num_programs(2) - 1)
    def _():
        s_out[:, 0] = state[...]

    ones = _block_ones(LANES)
    inv_n = 1.0 / HEAD_DIM
    for bb in range(nbk):
        y = jnp.concatenate([ys[(bb, ci)] for ci in range(nch)], axis=0) if nch > 1 else ys[(bb, 0)]
        mean = _head_sum(y, ones) * inv_n
        yc = y - mean
        var = _head_sum(yc * yc, ones) * inv_n
        yn = yc * lax.rsqrt(var + GN_EPS) * g_ref[...] + beta_ref[...]
        bonus = _head_sum(r_ref[bb] * k_ref[bb] * rk_ref[...], ones) * v_ref[bb]
        y_ref[bb] = yn + bonus


def _rwkv_rec(prep, s0, layer, r_k, lnx_g, lnx_b):
    n_b, n_l, d = prep[0].shape
    n_pairs = d // LANES
    c = min(n_l, RWKV_CHUNK)
    nch = min(n_l // c, RWKV_GROUP // 2)
    nbk = min(n_b, RWKV_GROUP // nch)
    lb = nch * c
    assert n_l % lb == 0 and n_b % nbk == 0 and c & (c - 1) == 0
    spec = pl.BlockSpec((nbk, lb, LANES), lambda g, p, t: (g, t, p))
    sspec = pl.BlockSpec((nbk, 1, LANES, LANES), lambda g, p, t: (g, p, 0, 0))
    vec = pl.BlockSpec((None, 1, LANES), lambda g, p, t: (layer, 0, p))
    return pl.pallas_call(
        functools.partial(_rwkv_rec_kernel, nbk=nbk, nch=nch, c=c),
        grid=(n_b // nbk, n_pairs, n_l // lb),
        in_specs=[spec] * 6 + [sspec, vec, vec, vec],
        out_specs=[spec, sspec],
        out_shape=[jax.ShapeDtypeStruct((n_b, n_l, d), F32),
                   jax.ShapeDtypeStruct((n_b, n_pairs, LANES, LANES), F32)],
        scratch_shapes=[pltpu.VMEM((nbk, LANES, LANES), F32)],
        compiler_params=_cparams("parallel", "parallel", "arbitrary"),
        name="rwkv_rec",
    )(*prep, s0, r_k, lnx_g, lnx_b)


def _pack_pairs(s):
    n_b, n_h, n, _ = s.shape
    st = jnp.swapaxes(s, -1, -2).reshape(n_b, n_h // 2, 2, n, n)
    z = jnp.zeros_like(st[:, :, 0])
    return jnp.concatenate([jnp.concatenate([st[:, :, 0], z], axis=-1),
                            jnp.concatenate([z, st[:, :, 1]], axis=-1)], axis=-2)


def _unpack_pairs(s):
    n_b, n_p, n2, _ = s.shape
    n = n2 // 2
    st = jnp.stack([s[:, :, :n, :n], s[:, :, n:, n:]], axis=2)
    return jnp.swapaxes(st, -1, -2).reshape(n_b, 2 * n_p, n, n)


def _mix_kernel(x_ref, ys_ref, gs_ref, ya_ref, ga_ref, yc_ref, gc_ref, wg_ref, bg_ref, wo_ref, lg_ref, lb_ref,
                y_ref, *maybe_yb_ref, alpha, d_ssm, d_att):
    silu = lambda g: g * jax.nn.sigmoid(g)
    dot = lambda a, b: jnp.dot(a.astype(BF16), b, preferred_element_type=F32)
    n_r = x_ref.shape[0]
    n_part = 2 if n_r % 32 == 0 else 1
    for part in range(n_part):
        rows = slice(part * n_r // n_part, (part + 1) * n_r // n_part)
        ys = ys_ref[rows, :]
        z = 0.5 * ys * (1.0 + jnp.tanh(math.sqrt(2.0 / math.pi) * (ys + 0.044715 * (ys * ys * ys))))
        m_s = z * jax.nn.sigmoid(dot(z, wg_ref[...]) + bg_ref[...]) * silu(gs_ref[rows, :])
        m_a = ya_ref[rows, :] * silu(ga_ref[rows, :])
        m_c = yc_ref[rows, :] * silu(gc_ref[rows, :])
        out = (dot(m_s, wo_ref[0:d_ssm, :]) + dot(m_a, wo_ref[d_ssm:d_ssm + d_att, :])
               + dot(m_c, wo_ref[d_ssm + d_att:, :]))
        h = alpha * x_ref[rows, :] + out
        mu = jnp.mean(h, axis=-1, keepdims=True)
        hc = h - mu
        var = jnp.mean(hc * hc, axis=-1, keepdims=True)
        y = hc * lax.rsqrt(var + LN_EPS) * lg_ref[...] + lb_ref[...]
        y_ref[rows, :] = y
        for yb_ref in maybe_yb_ref:
            yb_ref[rows, :] = y.astype(BF16)


def _mix(x, proj, ys, ya, yc, lw, alpha, gs_col, ga_col, gc_col, want_bf16):
    m, d = x.shape
    d_ssm, d_att, d_rw = ys.shape[1], ya.shape[1], yc.shape[1]
    tm = min(m, 512)
    assert m % tm == 0
    row = lambda w, c: pl.BlockSpec((tm, w), lambda i: (i, c))
    full = lambda a: _layer_spec(a, lw["layer"])
    ws = [lw["w_glu"], lw["b_glu"], lw["w_out"], lw["ln_g"], lw["ln_b"]]
    return pl.pallas_call(
        functools.partial(_mix_kernel, alpha=alpha, d_ssm=d_ssm, d_att=d_att),
        grid=(m // tm,),
        in_specs=[row(d, 0), row(d_ssm, 0), row(d_ssm, gs_col), row(d_att, 0), row(d_att, ga_col),
                  row(d_rw, 0), row(d_rw, gc_col)] + [full(a) for a in ws],
        out_specs=[row(d, 0)] * (2 if want_bf16 else 1),
        out_shape=[jax.ShapeDtypeStruct((m, d), F32)] + [jax.ShapeDtypeStruct((m, d), BF16)] * want_bf16,
        compiler_params=_cparams("parallel"),
        name="mix_out",
    )(x, ys, proj, ya, proj, yc, proj, *ws)


def _layer(x, proj, st, lw, s5p, alpha):
    n_b, n_l, d_model = x.shape
    d_ssm = lw["w_glu"].shape[1]
    n_h = lw["n_att_heads"]
    d_att = n_h * HEAD_DIM
    d_rw = lw["rwkv_w0"].shape[2]
    n_rh = d_rw // HEAD_DIM
    n_blk = d_ssm // LANES
    proj = proj.reshape(n_b, n_l, -1)
    q0 = 2 * d_ssm
    r0 = q0 + 4 * d_att
    assert q0 % d_att == 0 and r0 % d_rw == 0 and d_ssm == d_rw

    if st is None:
        zeros = jnp.zeros((n_b, n_blk, 1, S5_BLOCK), F32)
        h0r = h0i = zeros
        wkv0 = jnp.zeros((n_b, n_rh // 2, LANES, LANES), F32)
        shift0 = jnp.zeros((n_b, 4, d_rw), F32)
    else:
        k_cache, v_cache, h0r, h0i, wkv0, shift0 = st
        h0r = h0r.reshape(n_b, n_blk, 1, S5_BLOCK)
        h0i = h0i.reshape(n_b, n_blk, 1, S5_BLOCK)
        wkv0 = _pack_pairs(wkv0)
        shift0 = shift0.reshape(n_b, 4, d_rw)

    ys, h_re, h_im = _s5(proj, s5p, lw["layer"], h0r, h0i)

    n_keep = min(ATT_REACH, n_l)
    k_rows = proj[:, n_l - n_keep:, q0 + d_att:q0 + 2 * d_att].reshape(n_b, n_keep, n_h, HEAD_DIM)
    v_rows = proj[:, n_l - n_keep:, q0 + 2 * d_att:q0 + 3 * d_att].reshape(n_b, n_keep, n_h, HEAD_DIM)
    if st is None:
        pl0 = q0 // LANES
        ya = _att_prompt(proj, lw["bias_prompt"], lw["layer"], pl0, pl0 + d_att // LANES, pl0 + 2 * d_att // LANES,
                         n_h // 2)
    else:
        c0 = q0 // d_att
        ya = _att_step(proj, k_cache, v_cache, lw["layer"], lw["bias_cache"], lw["bias_new"], c0, c0 + 1, c0 + 2, n_h)

    prep = _rwkv_prep(proj, shift0, lw, r0 // d_rw)
    yc, wkv = _rwkv_rec(prep, wkv0, lw["layer"], lw["rwkv_r_k"], lw["rwkv_lnx_g"], lw["rwkv_lnx_b"])
    shift = proj[:, n_l - 1, r0:r0 + 4 * d_rw]

    m = n_b * n_l
    outs = _mix(x.reshape(m, d_model), proj.reshape(m, -1), ys.reshape(m, d_ssm), ya.reshape(m, d_att),
                yc.reshape(m, d_rw), lw, alpha, 1, (q0 + 3 * d_att) // d_att, (r0 + 4 * d_rw) // d_rw,
                want_bf16=not lw["last"])
    n_g = d_ssm // SSM_GROUP
    states = (k_rows, v_rows, h_re.reshape(n_b, n_g, SSM_STATE), h_im.reshape(n_b, n_g, SSM_STATE),
              _unpack_pairs(wkv), shift)
    y, yb = outs if len(outs) == 2 else (outs[0], None)
    return y.reshape(n_b, n_l, d_model), yb, states


def kernel(x_prompt, x_sample, cache_att_k, cache_att_v, state_ssm_re, state_ssm_im, state_rwkv, state_rwkv_shift, w_in, ssm_lam_re, ssm_lam_im, ssm_log_dt, ssm_b_re, ssm_b_im, ssm_c_re, ssm_c_im, ssm_d, ssm_w_glu, ssm_b_glu, att_rel_bias, rwkv_mu, rwkv_w0, rwkv_w1, rwkv_w2, rwkv_a0, rwkv_a1, rwkv_a2, rwkv_k_k, rwkv_k_a, rwkv_r_k, rwkv_lnx_g, rwkv_lnx_b, w_out, ln_g, ln_b):
    depth = w_in.shape[0]
    alpha = (2.0 * depth) ** 0.25
    y_p, y_s = x_prompt, x_sample
    yb_p = yb_s = None
    seg_lens = sorted({x_prompt.shape[1] // S5_SEGMENTS, x_sample.shape[1] // S5_SEGMENTS})
    p_st, s_st = [], []
    n_d, n_sb, n_w = cache_att_k.shape[:3]
    n_s = x_sample.shape[1]
    n_h = att_rel_bias.shape[1]
    k_cache = cache_att_k.reshape(n_d, n_sb, n_w, -1)
    v_cache = cache_att_v.reshape(n_d, n_sb, n_w, -1)
    row = lambda a: a.reshape(depth, 1, -1)
    bias_p, bias_s = _rel_bias(att_rel_bias.reshape(depth * n_h, -1), n_s, n_w)
    bias_s = bias_s.reshape(depth, n_h * n_s, n_w + n_s)
    params = {"w_in": w_in, "n_att_heads": n_h, "w_glu": ssm_w_glu.astype(BF16), "b_glu": row(ssm_b_glu),
              "rwkv_mu": rwkv_mu, "rwkv_w0": row(rwkv_w0), "rwkv_w1": rwkv_w1.astype(BF16),
              "rwkv_w2": rwkv_w2.astype(BF16), "rwkv_a0": row(rwkv_a0), "rwkv_a1": rwkv_a1.astype(BF16),
              "rwkv_a2": rwkv_a2.astype(BF16), "rwkv_k_k": row(rwkv_k_k), "rwkv_k_a": row(rwkv_k_a),
              "rwkv_r_k": row(rwkv_r_k), "rwkv_lnx_g": row(rwkv_lnx_g), "rwkv_lnx_b": row(rwkv_lnx_b),
              "w_out": w_out.astype(BF16), "ln_g": row(ln_g), "ln_b": row(ln_b),
              "bias_prompt": bias_p.reshape(depth * n_h // 2, 2, Q_BLOCK, K_WINDOW),
              "bias_cache": bias_s[:, :, :n_w], "bias_new": bias_s[:, :, n_w:]}
    s5p = _s5_params(ssm_lam_re, ssm_lam_im, ssm_log_dt, ssm_b_re, ssm_b_im, ssm_c_re, ssm_c_im, ssm_d, seg_lens)
    for l in range(depth):
        lw = dict(params, layer=l, last=(l == depth - 1))
        if yb_p is None:
            flat = lambda a: a.reshape(-1, a.shape[-1])
            yb_p, yb_s = _to_bf16(flat(y_p)), _to_bf16(flat(y_s))
        proj_p, proj_s = _in_proj(yb_p, yb_s, w_in, l)
        y_p, yb_p, st_p = _layer(y_p, proj_p, None, lw, s5p, alpha)
        y_s, yb_s, st_s = _layer(y_s, proj_s, (k_cache, v_cache, state_ssm_re[l], state_ssm_im[l],
                                             state_rwkv[l], state_rwkv_shift[l]), lw, s5p, alpha)
        p_st.append(st_p)
        s_st.append(st_s)
    stacked = lambda states, i: jnp.stack([st[i] for st in states], axis=0)
    return (y_p, y_s) + tuple(stacked(p_st, i) for i in range(6)) + tuple(stacked(s_st, i) for i in range(6))
```

```python
import functools
import math

import jax
import jax.numpy as jnp
import numpy as np
from jax import lax
from jax.experimental import pallas as pl
from jax.experimental.pallas import tpu as pltpu

F32 = jnp.float32
BF16 = jnp.bfloat16

LANES = 128
SUBLANES = 8
VMEM_LIMIT = 56 * 1024 * 1024

CHUNK = 64
LEFT_CHUNKS = 8
ATT_REACH = LEFT_CHUNKS * CHUNK
REL_CLIP = 128
HEAD_DIM = 64
SSM_GROUP = 16
SSM_STATE = 64
RWKV_LORA = 64
NEG_INF = -1e30
GN_EPS = 64e-5
LN_EPS = 1e-5

RWKV_CHUNK = 64
RWKV_GROUP = 16
Q_BLOCK = 4 * CHUNK
K_WINDOW = Q_BLOCK + ATT_REACH
S5_SEGMENTS = SUBLANES
S5_BLOCK = 512


def _cparams(*sem):
    return pltpu.CompilerParams(dimension_semantics=sem, vmem_limit_bytes=VMEM_LIMIT)


def _block_ones(n):
    r = lax.broadcasted_iota(jnp.int32, (n, n), 0) // HEAD_DIM
    c = lax.broadcasted_iota(jnp.int32, (n, n), 1) // HEAD_DIM
    return (r == c).astype(BF16)


def _head_sum(x, ones):
    hi = x.astype(BF16)
    lo = (x - hi.astype(F32)).astype(BF16)
    return jnp.dot(hi, ones, preferred_element_type=F32) + jnp.dot(lo, ones, preferred_element_type=F32)


def _matmul_kernel(x_ref, xs_ref, w_ref, o_ref, os_ref, wb):
    @pl.when(pl.program_id(1) == 0)
    def _():
        wb[...] = w_ref[0].astype(BF16)

    o_ref[...] = jnp.dot(x_ref[...], wb[...], preferred_element_type=F32)

    @pl.when(pl.program_id(1) == pl.num_programs(1) - 1)
    def _():
        os_ref[...] = jnp.dot(xs_ref[...], wb[...], preferred_element_type=F32)


def _cast_kernel(x_ref, o_ref):
    o_ref[...] = x_ref[...].astype(o_ref.dtype)


def _to_bf16(x):
    m, k = x.shape
    tm = min(m, 1024)
    assert m % tm == 0
    return pl.pallas_call(
        _cast_kernel,
        grid=(m // tm,),
        in_specs=[pl.BlockSpec((tm, k), lambda i: (i, 0))],
        out_specs=pl.BlockSpec((tm, k), lambda i: (i, 0)),
        out_shape=jax.ShapeDtypeStruct((m, k), BF16),
        compiler_params=_cparams("parallel"),
        name="to_bf16",
    )(x)


def _in_proj(x, xs, w, layer):
    m, k = x.shape
    ms = xs.shape[0]
    n = w.shape[2]
    tm = min(m, 1024)
    tn = 768
    assert m % tm == 0 and n % tn == 0 and x.dtype == BF16 and xs.dtype == BF16
    return pl.pallas_call(
        _matmul_kernel,
        grid=(n // tn, m // tm),
        in_specs=[pl.BlockSpec((tm, k), lambda j, i: (i, 0)),
                  pl.BlockSpec((ms, k), lambda j, i: (0, 0)),
                  pl.BlockSpec((1, k, tn), lambda j, i: (layer, 0, j))],
        out_specs=[pl.BlockSpec((tm, tn), lambda j, i: (i, j)),
                   pl.BlockSpec((ms, tn), lambda j, i: (0, j))],
        out_shape=[jax.ShapeDtypeStruct((m, n), F32), jax.ShapeDtypeStruct((ms, n), F32)],
        scratch_shapes=[pltpu.VMEM((k, tn), BF16)],
        compiler_params=_cparams("parallel", "arbitrary"),
        name="in_proj",
    )(x, xs, w)


def _s5_kernel(u_ref, wb_ref, ar_ref, ai_ref, pr_ref, pi_ref, h0r_ref, h0i_ref,
               wc_ref, d_ref, y_ref, hr_out, hi_out, h, up, *, seg, nb):
    nlb = h.shape[1] // (2 * LANES)
    n_l = seg * S5_SEGMENTS
    lb = lambda j: slice(j * LANES, (j + 1) * LANES)
    re = lambda j: slice(2 * j * LANES, (2 * j + 1) * LANES)
    im = lambda j: slice((2 * j + 1) * LANES, (2 * j + 2) * LANES)
    aligned = lambda r: r if isinstance(r, int) else pl.multiple_of(r, S5_SEGMENTS)
    rows = lambda bb, i: pl.ds(aligned(bb * n_l + i * S5_SEGMENTS), S5_SEGMENTS)
    strided = lambda i: pl.ds(i, S5_SEGMENTS, stride=seg)
    streams = range(nb)

    def loop(body, init):
        if seg <= 8:
            for i in range(seg):
                init = body(i, init)
            return init
        return lax.fori_loop(0, seg, body, init, unroll=8)

    def interleave(i, c):
        for bb in streams:
            up[rows(bb, i), :] = u_ref[bb, strided(i), :]
        return c

    loop(interleave, 0)
    u = up[...]
    ub = u.astype(BF16)
    h[...] = jnp.dot(ub, wb_ref[0], preferred_element_type=F32)
    bcast = lambda ref, j: jnp.broadcast_to(ref[0, :, lb(j)], (S5_SEGMENTS, LANES))
    ar = [bcast(ar_ref, j) for j in range(nlb)]
    ai = [bcast(ai_ref, j) for j in range(nlb)]
    chains = [(bb, j) for bb in streams for j in range(nlb)]
    n_ch = len(chains)

    def advance(i, c, store):
        nr, ni = [], []
        for q, (bb, j) in enumerate(chains):
            hr, hi = c[q], c[n_ch + q]
            r = ar[j] * hr - ai[j] * hi + h[rows(bb, i), re(j)]
            m = ar[j] * hi + ai[j] * hr + h[rows(bb, i), im(j)]
            if store:
                h[rows(bb, i), re(j)] = r
                h[rows(bb, i), im(j)] = m
            nr.append(r)
            ni.append(m)
        return tuple(nr + ni)

    zero = jnp.zeros((S5_SEGMENTS, LANES), F32)
    ends = loop(lambda i, c: advance(i, c, False), (zero,) * (2 * n_ch))

    starts_r, starts_i = [], []
    for q, (bb, j) in enumerate(chains):
        er, ei = ends[q], ends[n_ch + q]
        pr, pi = pr_ref[0, :, lb(j)], pi_ref[0, :, lb(j)]
        cr, ci = h0r_ref[bb, 0, :, lb(j)], h0i_ref[bb, 0, :, lb(j)]
        start_r, start_i = [cr], [ci]
        for s in range(S5_SEGMENTS - 1):
            cr, ci = (pr * cr - pi * ci + er[s:s + 1, :], pr * ci + pi * cr + ei[s:s + 1, :])
            start_r.append(cr)
            start_i.append(ci)
        starts_r.append(jnp.concatenate(start_r, axis=0))
        starts_i.append(jnp.concatenate(start_i, axis=0))

    last = loop(lambda i, c: advance(i, c, True), tuple(starts_r + starts_i))
    y = d_ref[...] * u
    for q, (bb, j) in enumerate(chains):
        hr_out[bb, 0, :, lb(j)] = last[q][S5_SEGMENTS - 1:, :]
        hi_out[bb, 0, :, lb(j)] = last[n_ch + q][S5_SEGMENTS - 1:, :]
    y += jnp.dot(h[...].astype(BF16), wc_ref[0], preferred_element_type=F32)
    up[...] = y

    def deinterleave(i, c):
        for bb in streams:
            y_ref[bb, strided(i), :] = up[rows(bb, i), :]
        return c

    loop(deinterleave, 0)


def _s5(proj, prm, layer, h0r, h0i):
    n_b, n_l, _ = proj.shape
    nblk = prm["nblk"]
    cin = prm["wb"].shape[1]
    b0 = layer * nblk
    seg = n_l // S5_SEGMENTS
    nb = n_b if seg <= 8 else 1
    assert seg * S5_SEGMENTS == n_l and cin == LANES
    wspec = lambda shape: pl.BlockSpec((1,) + shape, lambda b, s: (b0 + s, 0, 0))
    st_spec = pl.BlockSpec((nb, 1, 1, S5_BLOCK), lambda b, s: (b, s, 0, 0))
    return pl.pallas_call(
        functools.partial(_s5_kernel, seg=seg, nb=nb),
        grid=(n_b // nb, nblk),
        in_specs=[pl.BlockSpec((nb, n_l, cin), lambda b, s: (b, 0, s)),
                  wspec((cin, 2 * S5_BLOCK)),
                  wspec((1, S5_BLOCK)), wspec((1, S5_BLOCK)), wspec((1, S5_BLOCK)), wspec((1, S5_BLOCK)),
                  st_spec, st_spec,
                  wspec((2 * S5_BLOCK, cin)),
                  pl.BlockSpec((1, cin), lambda b, s: (0, b0 + s))],
        out_specs=[pl.BlockSpec((nb, n_l, cin), lambda b, s: (b, 0, s)), st_spec, st_spec],
        out_shape=[jax.ShapeDtypeStruct((n_b, n_l, nblk * cin), F32),
                   jax.ShapeDtypeStruct((n_b, nblk, 1, S5_BLOCK), F32),
                   jax.ShapeDtypeStruct((n_b, nblk, 1, S5_BLOCK), F32)],
        scratch_shapes=[pltpu.VMEM((nb * n_l, 2 * S5_BLOCK), F32), pltpu.VMEM((nb * n_l, LANES), F32)],
        compiler_params=_cparams("parallel", "parallel"),
        name="s5_scan",
    )(proj, prm["wb"], prm["ar"], prm["ai"], *prm["pows"][seg], h0r, h0i, prm["wc"], prm["d"])


def _s5_params(lam_re, lam_im, log_dt, b_re, b_im, c_re, c_im, d_skip, seg_lens):
    depth = lam_re.shape[0]
    merge = lambda t: t.reshape((-1,) + t.shape[2:])
    lam_re, lam_im, log_dt, b_re, b_im, c_re, c_im = map(merge, (lam_re, lam_im, log_dt, b_re, b_im, c_re, c_im))
    n_g, n_p = lam_re.shape
    dt = jnp.exp(log_dt)[:, None]
    e = jnp.exp(lam_re * dt)
    ab_re, ab_im = e * jnp.cos(lam_im * dt), e * jnp.sin(lam_im * dt)
    den = lam_re * lam_re + lam_im * lam_im
    nr, ni = ab_re - 1.0, ab_im
    q_re = (nr * lam_re + ni * lam_im) / den
    q_im = (ni * lam_re - nr * lam_im) / den
    bb_re = q_re[..., None] * b_re - q_im[..., None] * b_im
    bb_im = q_re[..., None] * b_im + q_im[..., None] * b_re
    gpb = S5_BLOCK // n_p
    nblk = n_g // gpb
    eye = jnp.eye(gpb, dtype=F32)

    def pack_b(t):
        t = t.reshape(nblk, gpb, n_p, SSM_GROUP)
        return jnp.einsum("sgpc,gh->sgchp", t, eye).reshape(nblk, gpb * SSM_GROUP, gpb * n_p).astype(BF16)

    def pack_c(t):
        t = t.reshape(nblk, gpb, SSM_GROUP, n_p)
        return jnp.einsum("sgcp,gh->shpgc", t, eye).reshape(nblk, gpb * n_p, gpb * SSM_GROUP).astype(BF16)

    flat = lambda t: t.reshape(nblk, 1, gpb * n_p)
    nlb = gpb * n_p // LANES
    wb = jnp.stack([pack_b(bb_re).reshape(nblk, -1, nlb, LANES), pack_b(bb_im).reshape(nblk, -1, nlb, LANES)], axis=3)
    wc = jnp.stack([pack_c(c_re).reshape(nblk, nlb, LANES, -1), -pack_c(c_im).reshape(nblk, nlb, LANES, -1)], axis=2)
    out = {"wb": wb.reshape(nblk, -1, 2 * gpb * n_p), "wc": wc.reshape(nblk, 2 * gpb * n_p, -1),
           "ar": flat(ab_re), "ai": flat(ab_im), "d": d_skip.reshape(1, -1), "pows": {}, "nblk": nblk // depth}
    for seg in seg_lens:
        assert seg & (seg - 1) == 0
        pr, pi = ab_re, ab_im
        for _ in range(int(math.log2(seg))):
            pr, pi = pr * pr - pi * pi, 2.0 * pr * pi
        out["pows"][seg] = (flat(pr), flat(pi))
    return out


def _att_prompt_kernel(q_ref, k_ref, v_ref, bias_ref, o_ref, kpad, vpad, *, n_l, scale):
    zpad = jnp.zeros((ATT_REACH, LANES), BF16)
    kpad[0:ATT_REACH, :] = zpad
    vpad[0:ATT_REACH, :] = zpad
    kpad[ATT_REACH:, :] = k_ref[0].astype(BF16)
    vpad[ATT_REACH:, :] = v_ref[0].astype(BF16)
    lane = lax.broadcasted_iota(jnp.int32, (1, LANES), 1)
    heads = (lane < HEAD_DIM, lane >= HEAD_DIM)
    col = lax.broadcasted_iota(jnp.int32, (1, K_WINDOW), 1)
    nt = (((1,), (1,)), ((), ()))

    def block(r0, n_masked):
        q = q_ref[0, pl.ds(r0, Q_BLOCK), :] * scale
        kw = kpad[pl.ds(r0, K_WINDOW), :]
        vw = vpad[pl.ds(r0, K_WINDOW), :]
        s = [lax.dot_general(jnp.where(hd, q, 0.0).astype(BF16), kw, nt, preferred_element_type=F32) + bias_ref[0, h]
             for h, hd in enumerate(heads)]
        if n_masked:
            s = [jnp.where(col < n_masked, NEG_INF, x) for x in s]
        m = [jnp.max(x, axis=-1, keepdims=True) for x in s]
        p = [jnp.exp(x - mx) for x, mx in zip(s, m)]
        l = [jnp.sum(x, axis=-1, keepdims=True) for x in p]
        o = [jnp.dot(x.astype(BF16), vw, preferred_element_type=F32) / lx for x, lx in zip(p, l)]
        o_ref[0, pl.ds(r0, Q_BLOCK), :] = jnp.where(heads[0], o[0], o[1]).astype(o_ref.dtype)

    n_blocks = n_l // Q_BLOCK
    n_first = min(n_blocks, ATT_REACH // Q_BLOCK)
    for qb in range(n_first):
        block(qb * Q_BLOCK, ATT_REACH - qb * Q_BLOCK)

    def body(qb, carry):
        block(pl.multiple_of(qb * Q_BLOCK, Q_BLOCK), 0)
        return carry

    lax.fori_loop(n_first, n_blocks, body, 0, unroll=7)


def _att_prompt(proj, bias, layer, q_col, k_col, v_col, n_pairs):
    n_b, n_l, _ = proj.shape
    assert n_l % Q_BLOCK == 0
    p0 = layer * n_pairs
    spec = lambda c0: pl.BlockSpec((1, n_l, LANES), lambda b, p: (b, 0, c0 + p))
    return pl.pallas_call(
        functools.partial(_att_prompt_kernel, n_l=n_l, scale=HEAD_DIM ** -0.5),
        grid=(n_b, n_pairs),
        in_specs=[spec(q_col), spec(k_col), spec(v_col),
                  pl.BlockSpec((1, 2, Q_BLOCK, K_WINDOW), lambda b, p: (p0 + p, 0, 0, 0))],
        out_specs=pl.BlockSpec((1, n_l, LANES), lambda b, p: (b, 0, p)),
        out_shape=jax.ShapeDtypeStruct((n_b, n_l, n_pairs * LANES), BF16),
        scratch_shapes=[pltpu.VMEM((n_l + ATT_REACH, LANES), BF16), pltpu.VMEM((n_l + ATT_REACH, LANES), BF16)],
        compiler_params=_cparams("parallel", "parallel"),
        name="att_prompt",
    )(proj, proj, proj, bias)


def _bias_kernel(g_ref, bp_ref, bs_ref):
    g = g_ref[0]
    n_q, n_k = bp_ref.shape[1], bp_ref.shape[2]
    toep = pltpu.roll(jnp.broadcast_to(g, (n_q, g.shape[1])), 0, 1, stride=1, stride_axis=0)[:, :n_k]
    qc = lax.broadcasted_iota(jnp.int32, (n_q, n_k), 0) // CHUNK
    kc = lax.broadcasted_iota(jnp.int32, (n_q, n_k), 1) // CHUNK
    bp_ref[0] = jnp.where((kc >= qc) & (kc <= qc + LEFT_CHUNKS), toep, NEG_INF)
    n_s, n_ws = bs_ref.shape[1], bs_ref.shape[2]
    bs_ref[0] = pltpu.roll(jnp.broadcast_to(g, (n_s, g.shape[1])), 0, 1, stride=1, stride_axis=0)[:, :n_ws]


def _rel_bias(table, n_s, n_w):
    n_h = table.shape[0]
    assert n_w == ATT_REACH
    width = 1024
    assert width >= K_WINDOW + Q_BLOCK and width % LANES == 0
    d = np.arange(width)
    d = np.where(d < width - Q_BLOCK, d, d - width)
    idx = np.clip(ATT_REACH - d, -REL_CLIP, REL_CLIP) + REL_CLIP
    diag = table[:, idx].astype(F32).reshape(n_h, 1, width)
    return pl.pallas_call(
        _bias_kernel,
        grid=(n_h,),
        in_specs=[pl.BlockSpec((1, 1, width), lambda h: (h, 0, 0))],
        out_specs=[pl.BlockSpec((1, Q_BLOCK, K_WINDOW), lambda h: (h, 0, 0)),
                   pl.BlockSpec((1, n_s, n_w + n_s), lambda h: (h, 0, 0))],
        out_shape=[jax.ShapeDtypeStruct((n_h, Q_BLOCK, K_WINDOW), F32),
                   jax.ShapeDtypeStruct((n_h, n_s, n_w + n_s), F32)],
        compiler_params=_cparams("parallel"),
        name="rel_bias",
    )(diag)


def _att_step_kernel(q_ref, kn_ref, vn_ref, kc_ref, vc_ref, bc_ref, bn_ref, o_ref, *, n_h, scale):
    n_s = q_ref.shape[1]
    d = q_ref.shape[2]
    q = q_ref[0] * scale
    qt = jnp.concatenate([q] * n_h, axis=0)
    rh = lax.broadcasted_iota(jnp.int32, (n_h * n_s, d), 0) // n_s
    ch = lax.broadcasted_iota(jnp.int32, (n_h * n_s, d), 1) // HEAD_DIM
    qbd = jnp.where(rh == ch, qt, 0.0).astype(BF16)
    nt = (((1,), (1,)), ((), ()))
    s1 = lax.dot_general(qbd, kc_ref[0, 0].astype(BF16), nt, preferred_element_type=F32) + bc_ref[...]
    s2 = lax.dot_general(qbd, kn_ref[0].astype(BF16), nt, preferred_element_type=F32) + bn_ref[...]
    m = jnp.maximum(jnp.max(s1, axis=-1, keepdims=True), jnp.max(s2, axis=-1, keepdims=True))
    p1 = jnp.exp(s1 - m)
    p2 = jnp.exp(s2 - m)
    l = jnp.sum(p1, axis=-1, keepdims=True) + jnp.sum(p2, axis=-1, keepdims=True)
    o = (jnp.dot(p1.astype(BF16), vc_ref[0, 0].astype(BF16), preferred_element_type=F32)
         + jnp.dot(p2.astype(BF16), vn_ref[0].astype(BF16), preferred_element_type=F32)) / l
    first = lax.broadcasted_iota(jnp.int32, (1, LANES), 1) < HEAD_DIM
    for j in range(d // LANES):
        blk = o[:, j * LANES:(j + 1) * LANES]
        lo = blk[(2 * j) * n_s:(2 * j + 1) * n_s, :]
        hi = blk[(2 * j + 1) * n_s:(2 * j + 2) * n_s, :]
        o_ref[0, :, j * LANES:(j + 1) * LANES] = jnp.where(first, lo, hi).astype(o_ref.dtype)


def _att_step(proj, k_cache, v_cache, layer, bias_c, bias_n, q_col, k_col, v_col, n_h):
    n_b, n_s, _ = proj.shape
    n_w, d = k_cache.shape[2], k_cache.shape[3]
    spec = lambda c0: pl.BlockSpec((1, n_s, d), lambda b: (b, 0, c0))
    cspec = pl.BlockSpec((1, 1, n_w, d), lambda b: (layer, b, 0, 0))
    bspec = lambda a: pl.BlockSpec((None,) + a.shape[1:], lambda b: (layer, 0, 0))
    return pl.pallas_call(
        functools.partial(_att_step_kernel, n_h=n_h, scale=HEAD_DIM ** -0.5),
        grid=(n_b,),
        in_specs=[spec(q_col), spec(k_col), spec(v_col), cspec, cspec, bspec(bias_c), bspec(bias_n)],
        out_specs=pl.BlockSpec((1, n_s, d), lambda b: (b, 0, 0)),
        out_shape=jax.ShapeDtypeStruct((n_b, n_s, d), BF16),
        compiler_params=_cparams("parallel"),
        name="att_step",
    )(proj, proj, proj, k_cache, v_cache, bias_c, bias_n)


def _rwkv_prep_kernel(r_ref, k_ref, v_ref, u_ref, sh_ref, mu_ref, w0_ref, w1_ref, w2_ref, a0_ref, a1_ref, a2_ref,
                      kk_ref, ka_ref, nkk_out, w_out, b_out, k_out, r_out, v_out, prev):
    j = pl.program_id(1)

    @pl.when(j == 0)
    def _():
        prev[...] = sh_ref[0]

    n_r = r_ref.shape[1]
    row0 = lax.broadcasted_iota(jnp.int32, (n_r, 1), 0) == 0

    def delta(x, i):
        sh = jnp.where(row0, prev[i:i + 1, :], pltpu.roll(x, 1, axis=0))
        prev[i:i + 1, :] = x[n_r - 1:, :]
        return sh - x

    rp, kp, vp, up = r_ref[0], k_ref[0], v_ref[0], u_ref[0]
    du = delta(up, 3)
    r = rp + delta(rp, 0) * mu_ref[0:1, :]
    k = kp + delta(kp, 1) * mu_ref[1:2, :]
    v = vp + delta(vp, 2) * mu_ref[2:3, :]
    xw = up + du * mu_ref[3:4, :]
    xa = up + du * mu_ref[4:5, :]
    dot = lambda a, b: jnp.dot(a.astype(BF16), b, preferred_element_type=F32)
    z = -(w0_ref[...] + dot(jnp.tanh(dot(xw, w1_ref[...])), w2_ref[...]))
    softplus = jnp.maximum(z, 0.0) + jnp.log(1.0 + jnp.exp(-jnp.abs(z)))
    w = -softplus - 0.5
    a = jax.nn.sigmoid(a0_ref[...] + dot(dot(xa, a1_ref[...]), a2_ref[...]))
    kk = k * kk_ref[...]
    ss = _head_sum(kk * kk, _block_ones(kk.shape[1]))
    kk = kk * lax.rsqrt(jnp.maximum(ss, 1e-24))
    nkk_out[0] = (-kk).astype(nkk_out.dtype)
    w_out[0] = -jnp.exp(w)
    b_out[0] = (kk * a).astype(b_out.dtype)
    k_out[0] = (k * (1.0 + (a - 1.0) * ka_ref[...])).astype(k_out.dtype)
    r_out[0] = r.astype(r_out.dtype)
    v_out[0] = v.astype(v_out.dtype)


def _layer_spec(a, layer):
    return pl.BlockSpec((None,) + a.shape[1:], lambda *_: (layer,) + (0,) * (a.ndim - 1),
                        pipeline_mode=pl.Buffered(1))


def _rwkv_prep(proj, shift0, lw, col0):
    n_b, n_l, _ = proj.shape
    d = shift0.shape[2]
    tr = min(n_l, 512)
    assert n_l % tr == 0
    spec = lambda c: pl.BlockSpec((1, tr, d), lambda b, j: (b, j, c))
    full = lambda a: _layer_spec(a, lw["layer"])
    ws = [lw["rwkv_mu"], lw["rwkv_w0"], lw["rwkv_w1"], lw["rwkv_w2"], lw["rwkv_a0"], lw["rwkv_a1"], lw["rwkv_a2"],
          lw["rwkv_k_k"], lw["rwkv_k_a"]]
    ospec = pl.BlockSpec((1, tr, d), lambda b, j: (b, j, 0))
    return pl.pallas_call(
        _rwkv_prep_kernel,
        grid=(n_b, n_l // tr),
        in_specs=[spec(col0), spec(col0 + 1), spec(col0 + 2), spec(col0 + 3),
                  pl.BlockSpec((1, 4, d), lambda b, j: (b, 0, 0))] + [full(a) for a in ws],
        out_specs=[ospec] * 6,
        out_shape=[jax.ShapeDtypeStruct((n_b, n_l, d), F32 if i == 1 else BF16) for i in range(6)],
        scratch_shapes=[pltpu.VMEM((4, d), F32)],
        compiler_params=_cparams("parallel", "arbitrary"),
        name="rwkv_prep",
    )(proj, proj, proj, proj, shift0, *ws)


def _rwkv_rec_kernel(nkk_ref, lw_ref, b_ref, k_ref, r_ref, v_ref, s0_ref, rk_ref, g_ref, beta_ref,
                     y_ref, s_out, state, *, nbk, nch, c):
    tb = pl.program_id(2)

    @pl.when(tb == 0)
    def _():
        state[...] = s0_ref[:, 0]

    c2 = 2 * c
    lane = lax.broadcasted_iota(jnp.int32, (1, LANES), 1)
    m0 = (lane < HEAD_DIM).astype(F32)
    m1 = 1.0 - m0
    by_head = lambda x: jnp.concatenate([x * m0, x * m1], axis=0)
    mm = lambda a, b: jnp.dot(a.astype(BF16), b.astype(BF16), preferred_element_type=F32)
    mm_nt = lambda a, b: lax.dot_general(a.astype(BF16), b.astype(BF16), (((1,), (1,)), ((), ())),
                                         preferred_element_type=F32)
    ii = lambda shape, d: lax.broadcasted_iota(jnp.int32, shape, d)
    tri = (ii((c, c), 0) >= ii((c, c), 1)).astype(BF16)
    rr, cc = ii((c2, c2), 0), ii((c2, c2), 1)
    strict = (rr // c == cc // c) & (cc % c < rr % c)
    incl = ii((c, c2), 1) % c <= ii((c, c2), 0)
    eye_t = (rr == cc).astype(F32)
    eye_s = (ii((LANES, LANES), 0) == ii((LANES, LANES), 1)).astype(F32)
    probs = [(bb, ci) for ci in range(nch) for bb in range(nbk)]
    rows = lambda ref, q: ref[q[0], q[1] * c:(q[1] + 1) * c, :].astype(F32)
    each = lambda fn, *lists: [fn(*xs) for xs in zip(*lists)]

    lw = [rows(lw_ref, q) for q in probs]

    def cumsum(x):
        hi = x.astype(BF16)
        lo = (x - hi.astype(F32)).astype(BF16)
        return jnp.dot(tri, hi, preferred_element_type=F32) + jnp.dot(tri, lo, preferred_element_type=F32)

    cum = each(cumsum, lw)
    tot = [x[c - 1:c, :] for x in cum]
    g_c = [jnp.exp(x) for x in tot]
    a_t = each(lambda q, x, l: rows(nkk_ref, q) * jnp.exp(x - l), probs, cum, lw)
    r_t = each(lambda q, x: rows(r_ref, q) * jnp.exp(x), probs, cum)
    ginv = [jnp.exp(-x) for x in cum]
    ghat = each(lambda x, t: jnp.exp(t - x), cum, tot)
    kx = [rows(k_ref, q) for q in probs]
    bx = [rows(b_ref, q) for q in probs]
    a_bd = [by_head(x) for x in a_t]
    lhs = each(lambda a, r: jnp.concatenate([a, r], axis=0), a_bd, r_t)
    sc_k = each(lambda l, k, gi: mm_nt(l, by_head(k * gi)), lhs, kx, ginv)
    sc_b = each(lambda l, b, gi: mm_nt(l, by_head(b * gi)), lhs, bx, ginv)
    l_ak = [jnp.where(strict, x[:c2], 0.0) for x in sc_k]
    l_ab = [jnp.where(strict, x[:c2], 0.0) for x in sc_b]
    m_rk = [jnp.where(incl, x[c2:], 0.0) for x in sc_k]
    m_rb = [jnp.where(incl, x[c2:], 0.0) for x in sc_b]
    col_t = ii((1, c2), 1)
    t0, t1 = (col_t < c).astype(F32), (col_t >= c).astype(F32)
    blockdiag = lambda x: jnp.concatenate([x * t0, x * t1], axis=0)
    pw = [x[:c] + x[c:] for x in l_ab]
    inv = [x + eye_t[:c] + eye_t[c:] for x in pw]
    for _ in range(int(math.log2(c)) - 1):
        pw = each(lambda x: mm(x, blockdiag(x)), pw)
        inv = each(lambda p, x: p + mm(p, blockdiag(x)), inv, pw)
    inv = [blockdiag(x) for x in inv]
    v_bd = [by_head(rows(v_ref, q)) for q in probs]
    x_in = each(lambda a, l, v: jnp.concatenate([a, mm(l, v)], axis=1), a_bd, l_ak, v_bd)
    uu = each(mm, inv, x_in)
    u_a = [x[:, :LANES] for x in uu]
    u_b = [x[:, LANES:] for x in uu]
    bh_t = each(lambda b, gh: by_head(b * gh).T, bx, ghat)
    kh_t = each(lambda k, gh: by_head(k * gh).T, kx, ghat)
    g_mat = each(lambda gc, bt, ua: eye_s * gc + mm(bt, ua), g_c, bh_t, u_a)
    h_mat = each(lambda kt, v, bt, ub: mm(kt, v) + mm(bt, ub), kh_t, v_bd, bh_t, u_b)
    y_a = each(lambda r, m, ua: r + mm(m, ua), r_t, m_rb, u_a)
    y_b = each(lambda mk, v, mb, ub: mm(mk, v) + mm(mb, ub), m_rk, v_bd, m_rb, u_b)

    s_cur = [state[bb] for bb in range(nbk)]
    ys = {}
    for i, (bb, ci) in enumerate(probs):
        ys[(bb, ci)] = mm(y_a[i], s_cur[bb]) + y_b[i]
        s_cur[bb] = mm(g_mat[i], s_cur[bb]) + h_mat[i]
    for bb in range(nbk):
        state[bb] = s_cur[bb]

    @pl.when(tb == pl.num_programs(2) - 1)
    def _():
        s_out[:, 0] = state[...]

    ones = _block_ones(LANES)
    inv_n = 1.0 / HEAD_DIM
    for bb in range(nbk):
        y = jnp.concatenate([ys[(bb, ci)] for ci in range(nch)], axis=0) if nch > 1 else ys[(bb, 0)]
        mean = _head_sum(y, ones) * inv_n
        yc = y - mean
        var = _head_sum(yc * yc, ones) * inv_n
        yn = yc * lax.rsqrt(var + GN_EPS) * g_ref[...] + beta_ref[...]
        f32 = lambda ref: ref[bb].astype(F32)
        bonus = _head_sum(f32(r_ref) * f32(k_ref) * rk_ref[...], ones) * f32(v_ref)
        y_ref[bb] = (yn + bonus).astype(y_ref.dtype)


def _rwkv_rec(prep, s0, layer, r_k, lnx_g, lnx_b):
    n_b, n_l, d = prep[0].shape
    n_pairs = d // LANES
    c = min(n_l, RWKV_CHUNK)
    nch = min(n_l // c, RWKV_GROUP // 2)
    nbk = min(n_b, RWKV_GROUP // nch)
    lb = nch * c
    assert n_l % lb == 0 and n_b % nbk == 0 and c & (c - 1) == 0
    spec = pl.BlockSpec((nbk, lb, LANES), lambda g, p, t: (g, t, p))
    sspec = pl.BlockSpec((nbk, 1, LANES, LANES), lambda g, p, t: (g, p, 0, 0))
    vec = pl.BlockSpec((None, 1, LANES), lambda g, p, t: (layer, 0, p))
    return pl.pallas_call(
        functools.partial(_rwkv_rec_kernel, nbk=nbk, nch=nch, c=c),
        grid=(n_b // nbk, n_pairs, n_l // lb),
        in_specs=[spec] * 6 + [sspec, vec, vec, vec],
        out_specs=[spec, sspec],
        out_shape=[jax.ShapeDtypeStruct((n_b, n_l, d), BF16),
                   jax.ShapeDtypeStruct((n_b, n_pairs, LANES, LANES), F32)],
        scratch_shapes=[pltpu.VMEM((nbk, LANES, LANES), F32)],
        compiler_params=_cparams("parallel", "parallel", "arbitrary"),
        name="rwkv_rec",
    )(*prep, s0, r_k, lnx_g, lnx_b)


def _pack_pairs(s):
    n_b, n_h, n, _ = s.shape
    st = jnp.swapaxes(s, -1, -2).reshape(n_b, n_h // 2, 2, n, n)
    z = jnp.zeros_like(st[:, :, 0])
    return jnp.concatenate([jnp.concatenate([st[:, :, 0], z], axis=-1),
                            jnp.concatenate([z, st[:, :, 1]], axis=-1)], axis=-2)


def _unpack_pairs(s):
    n_b, n_p, n2, _ = s.shape
    n = n2 // 2
    st = jnp.stack([s[:, :, :n, :n], s[:, :, n:, n:]], axis=2)
    return jnp.swapaxes(st, -1, -2).reshape(n_b, 2 * n_p, n, n)


def _mix_kernel(x_ref, ys_ref, gs_ref, ya_ref, ga_ref, yc_ref, gc_ref, wg_ref, bg_ref, wo_ref, lg_ref, lb_ref,
                y_ref, *maybe_yb_ref, alpha, d_ssm, d_att):
    silu = lambda g: g * jax.nn.sigmoid(g)
    dot = lambda a, b: jnp.dot(a.astype(BF16), b, preferred_element_type=F32)
    n_r = x_ref.shape[0]
    n_part = 2 if n_r % 32 == 0 else 1
    for part in range(n_part):
        rows = slice(part * n_r // n_part, (part + 1) * n_r // n_part)
        ys = ys_ref[rows, :]
        z = 0.5 * ys * (1.0 + jnp.tanh(math.sqrt(2.0 / math.pi) * (ys + 0.044715 * (ys * ys * ys))))
        m_s = z * jax.nn.sigmoid(dot(z, wg_ref[...]) + bg_ref[...]) * silu(gs_ref[rows, :])
        m_a = ya_ref[rows, :] * silu(ga_ref[rows, :])
        m_c = yc_ref[rows, :] * silu(gc_ref[rows, :])
        out = (dot(m_s, wo_ref[0:d_ssm, :]) + dot(m_a, wo_ref[d_ssm:d_ssm + d_att, :])
               + dot(m_c, wo_ref[d_ssm + d_att:, :]))
        h = alpha * x_ref[rows, :] + out
        mu = jnp.mean(h, axis=-1, keepdims=True)
        hc = h - mu
        var = jnp.mean(hc * hc, axis=-1, keepdims=True)
        y = hc * lax.rsqrt(var + LN_EPS) * lg_ref[...] + lb_ref[...]
        y_ref[rows, :] = y
        for yb_ref in maybe_yb_ref:
            yb_ref[rows, :] = y.astype(BF16)


def _mix(x, proj, ys, ya, yc, lw, alpha, gs_col, ga_col, gc_col, want_bf16):
    m, d = x.shape
    d_ssm, d_att, d_rw = ys.shape[1], ya.shape[1], yc.shape[1]
    tm = min(m, 512)
    assert m % tm == 0
    row = lambda w, c: pl.BlockSpec((tm, w), lambda i: (i, c))
    full = lambda a: _layer_spec(a, lw["layer"])
    ws = [lw["w_glu"], lw["b_glu"], lw["w_out"], lw["ln_g"], lw["ln_b"]]
    return pl.pallas_call(
        functools.partial(_mix_kernel, alpha=alpha, d_ssm=d_ssm, d_att=d_att),
        grid=(m // tm,),
        in_specs=[row(d, 0), row(d_ssm, 0), row(d_ssm, gs_col), row(d_att, 0), row(d_att, ga_col),
                  row(d_rw, 0), row(d_rw, gc_col)] + [full(a) for a in ws],
        out_specs=[row(d, 0)] * (2 if want_bf16 else 1),
        out_shape=[jax.ShapeDtypeStruct((m, d), F32)] + [jax.ShapeDtypeStruct((m, d), BF16)] * want_bf16,
        compiler_params=_cparams("parallel"),
        name="mix_out",
    )(x, ys, proj, ya, proj, yc, proj, *ws)


def _layer(x, proj, st, lw, s5p, alpha):
    n_b, n_l, d_model = x.shape
    d_ssm = lw["w_glu"].shape[1]
    n_h = lw["n_att_heads"]
    d_att = n_h * HEAD_DIM
    d_rw = lw["rwkv_w0"].shape[2]
    n_rh = d_rw // HEAD_DIM
    n_blk = d_ssm // LANES
    proj = proj.reshape(n_b, n_l, -1)
    q0 = 2 * d_ssm
    r0 = q0 + 4 * d_att
    assert q0 % d_att == 0 and r0 % d_rw == 0 and d_ssm == d_rw

    if st is None:
        zeros = jnp.zeros((n_b, n_blk, 1, S5_BLOCK), F32)
        h0r = h0i = zeros
        wkv0 = jnp.zeros((n_b, n_rh // 2, LANES, LANES), F32)
        shift0 = jnp.zeros((n_b, 4, d_rw), F32)
    else:
        k_cache, v_cache, h0r, h0i, wkv0, shift0 = st
        h0r = h0r.reshape(n_b, n_blk, 1, S5_BLOCK)
        h0i = h0i.reshape(n_b, n_blk, 1, S5_BLOCK)
        wkv0 = _pack_pairs(wkv0)
        shift0 = shift0.reshape(n_b, 4, d_rw)

    ys, h_re, h_im = _s5(proj, s5p, lw["layer"], h0r, h0i)

    n_keep = min(ATT_REACH, n_l)
    k_rows = proj[:, n_l - n_keep:, q0 + d_att:q0 + 2 * d_att].reshape(n_b, n_keep, n_h, HEAD_DIM)
    v_rows = proj[:, n_l - n_keep:, q0 + 2 * d_att:q0 + 3 * d_att].reshape(n_b, n_keep, n_h, HEAD_DIM)
    if st is None:
        pl0 = q0 // LANES
        ya = _att_prompt(proj, lw["bias_prompt"], lw["layer"], pl0, pl0 + d_att // LANES, pl0 + 2 * d_att // LANES,
                         n_h // 2)
    else:
        c0 = q0 // d_att
        ya = _att_step(proj, k_cache, v_cache, lw["layer"], lw["bias_cache"], lw["bias_new"], c0, c0 + 1, c0 + 2, n_h)

    prep = _rwkv_prep(proj, shift0, lw, r0 // d_rw)
    yc, wkv = _rwkv_rec(prep, wkv0, lw["layer"], lw["rwkv_r_k"], lw["rwkv_lnx_g"], lw["rwkv_lnx_b"])
    shift = proj[:, n_l - 1, r0:r0 + 4 * d_rw]

    m = n_b * n_l
    outs = _mix(x.reshape(m, d_model), proj.reshape(m, -1), ys.reshape(m, d_ssm), ya.reshape(m, d_att),
                yc.reshape(m, d_rw), lw, alpha, 1, (q0 + 3 * d_att) // d_att, (r0 + 4 * d_rw) // d_rw,
                want_bf16=not lw["last"])
    n_g = d_ssm // SSM_GROUP
    states = (k_rows, v_rows, h_re.reshape(n_b, n_g, SSM_STATE), h_im.reshape(n_b, n_g, SSM_STATE),
              _unpack_pairs(wkv), shift)
    y, yb = outs if len(outs) == 2 else (outs[0], None)
    return y.reshape(n_b, n_l, d_model), yb, states


def kernel(x_prompt, x_sample, cache_att_k, cache_att_v, state_ssm_re, state_ssm_im, state_rwkv, state_rwkv_shift, w_in, ssm_lam_re, ssm_lam_im, ssm_log_dt, ssm_b_re, ssm_b_im, ssm_c_re, ssm_c_im, ssm_d, ssm_w_glu, ssm_b_glu, att_rel_bias, rwkv_mu, rwkv_w0, rwkv_w1, rwkv_w2, rwkv_a0, rwkv_a1, rwkv_a2, rwkv_k_k, rwkv_k_a, rwkv_r_k, rwkv_lnx_g, rwkv_lnx_b, w_out, ln_g, ln_b):
    depth = w_in.shape[0]
    alpha = (2.0 * depth) ** 0.25
    y_p, y_s = x_prompt, x_sample
    yb_p = yb_s = None
    seg_lens = sorted({x_prompt.shape[1] // S5_SEGMENTS, x_sample.shape[1] // S5_SEGMENTS})
    p_st, s_st = [], []
    n_d, n_sb, n_w = cache_att_k.shape[:3]
    n_s = x_sample.shape[1]
    n_h = att_rel_bias.shape[1]
    k_cache = cache_att_k.reshape(n_d, n_sb, n_w, -1)
    v_cache = cache_att_v.reshape(n_d, n_sb, n_w, -1)
    row = lambda a: a.reshape(depth, 1, -1)
    bias_p, bias_s = _rel_bias(att_rel_bias.reshape(depth * n_h, -1), n_s, n_w)
    bias_s = bias_s.reshape(depth, n_h * n_s, n_w + n_s)
    params = {"w_in": w_in, "n_att_heads": n_h, "w_glu": ssm_w_glu.astype(BF16), "b_glu": row(ssm_b_glu),
              "rwkv_mu": rwkv_mu, "rwkv_w0": row(rwkv_w0), "rwkv_w1": rwkv_w1.astype(BF16),
              "rwkv_w2": rwkv_w2.astype(BF16), "rwkv_a0": row(rwkv_a0), "rwkv_a1": rwkv_a1.astype(BF16),
              "rwkv_a2": rwkv_a2.astype(BF16), "rwkv_k_k": row(rwkv_k_k), "rwkv_k_a": row(rwkv_k_a),
              "rwkv_r_k": row(rwkv_r_k), "rwkv_lnx_g": row(rwkv_lnx_g), "rwkv_lnx_b": row(rwkv_lnx_b),
              "w_out": w_out.astype(BF16), "ln_g": row(ln_g), "ln_b": row(ln_b),
              "bias_prompt": bias_p.reshape(depth * n_h // 2, 2, Q_BLOCK, K_WINDOW),
              "bias_cache": bias_s[:, :, :n_w], "bias_new": bias_s[:, :, n_w:]}
    s5p = _s5_params(ssm_lam_re, ssm_lam_im, ssm_log_dt, ssm_b_re, ssm_b_im, ssm_c_re, ssm_c_im, ssm_d, seg_lens)
    for l in range(depth):
        lw = dict(params, layer=l, last=(l == depth - 1))
        if yb_p is None:
            flat = lambda a: a.reshape(-1, a.shape[-1])
            yb_p, yb_s = _to_bf16(flat(y_p)), _to_bf16(flat(y_s))
        proj_p, proj_s = _in_proj(yb_p, yb_s, w_in, l)
        y_p, yb_p, st_p = _layer(y_p, proj_p, None, lw, s5p, alpha)
        y_s, yb_s, st_s = _layer(y_s, proj_s, (k_cache, v_cache, state_ssm_re[l], state_ssm_im[l],
                                             state_rwkv[l], state_rwkv_shift[l]), lw, s5p, alpha)
        p_st.append(st_p)
        s_st.append(st_s)
    stacked = lambda states, i: jnp.stack([st[i] for st in states], axis=0)
    return (y_p, y_s) + tuple(stacked(p_st, i) for i in range(6)) + tuple(stacked(s_st, i) for i in range(6))
```

```python
import functools
import math

import jax
import jax.numpy as jnp
import numpy as np
from jax import lax
from jax.experimental import pallas as pl
from jax.experimental.pallas import tpu as pltpu

F32 = jnp.float32
BF16 = jnp.bfloat16

LANES = 128
SUBLANES = 8
VMEM_LIMIT = 56 * 1024 * 1024

CHUNK = 64
LEFT_CHUNKS = 8
ATT_REACH = LEFT_CHUNKS * CHUNK
REL_CLIP = 128
HEAD_DIM = 64
SSM_GROUP = 16
SSM_STATE = 64
RWKV_LORA = 64
NEG_INF = -1e30
GN_EPS = 64e-5
LN_EPS = 1e-5

RWKV_CHUNK = 64
RWKV_GROUP = 16
Q_BLOCK = 4 * CHUNK
K_WINDOW = Q_BLOCK + ATT_REACH
S5_SEGMENTS = SUBLANES
S5_BLOCK = 512


def _cparams(*sem):
    return pltpu.CompilerParams(dimension_semantics=sem, vmem_limit_bytes=VMEM_LIMIT)


def _block_ones(n):
    r = lax.broadcasted_iota(jnp.int32, (n, n), 0) // HEAD_DIM
    c = lax.broadcasted_iota(jnp.int32, (n, n), 1) // HEAD_DIM
    return (r == c).astype(BF16)


def _head_sum(x, ones):
    hi = x.astype(BF16)
    lo = (x - hi.astype(F32)).astype(BF16)
    return jnp.dot(hi, ones, preferred_element_type=F32) + jnp.dot(lo, ones, preferred_element_type=F32)


def _matmul_kernel(x_ref, xs_ref, w_ref, o_ref, os_ref, wb):
    @pl.when(pl.program_id(1) == 0)
    def _():
        wb[...] = w_ref[0].astype(BF16)

    o_ref[...] = jnp.dot(x_ref[...], wb[...], preferred_element_type=F32)

    @pl.when(pl.program_id(1) == pl.num_programs(1) - 1)
    def _():
        os_ref[...] = jnp.dot(xs_ref[...], wb[...], preferred_element_type=F32)


def _cast_kernel(x_ref, o_ref):
    o_ref[...] = x_ref[...].astype(o_ref.dtype)


def _to_bf16(x):
    m, k = x.shape
    tm = min(m, 1024)
    assert m % tm == 0
    return pl.pallas_call(
        _cast_kernel,
        grid=(m // tm,),
        in_specs=[pl.BlockSpec((tm, k), lambda i: (i, 0))],
        out_specs=pl.BlockSpec((tm, k), lambda i: (i, 0)),
        out_shape=jax.ShapeDtypeStruct((m, k), BF16),
        compiler_params=_cparams("parallel"),
        name="to_bf16",
    )(x)


def _in_proj(x, xs, w, layer):
    m, k = x.shape
    ms = xs.shape[0]
    n = w.shape[2]
    tm = min(m, 1024)
    tn = 768
    assert m % tm == 0 and n % tn == 0 and x.dtype == BF16 and xs.dtype == BF16
    return pl.pallas_call(
        _matmul_kernel,
        grid=(n // tn, m // tm),
        in_specs=[pl.BlockSpec((tm, k), lambda j, i: (i, 0)),
                  pl.BlockSpec((ms, k), lambda j, i: (0, 0)),
                  pl.BlockSpec((1, k, tn), lambda j, i: (layer, 0, j))],
        out_specs=[pl.BlockSpec((tm, tn), lambda j, i: (i, j)),
                   pl.BlockSpec((ms, tn), lambda j, i: (0, j))],
        out_shape=[jax.ShapeDtypeStruct((m, n), F32), jax.ShapeDtypeStruct((ms, n), F32)],
        scratch_shapes=[pltpu.VMEM((k, tn), BF16)],
        compiler_params=_cparams("parallel", "arbitrary"),
        name="in_proj",
    )(x, xs, w)


def _s5_kernel(u_ref, wb_ref, ar_ref, ai_ref, pr_ref, pi_ref, h0r_ref, h0i_ref,
               wc_ref, d_ref, y_ref, hr_out, hi_out, h, up, *, seg, nb):
    nlb = h.shape[1] // (2 * LANES)
    n_l = seg * S5_SEGMENTS
    lb = lambda j: slice(j * LANES, (j + 1) * LANES)
    re = lambda j: slice(2 * j * LANES, (2 * j + 1) * LANES)
    im = lambda j: slice((2 * j + 1) * LANES, (2 * j + 2) * LANES)
    aligned = lambda r: r if isinstance(r, int) else pl.multiple_of(r, S5_SEGMENTS)
    rows = lambda bb, i: pl.ds(aligned(bb * n_l + i * S5_SEGMENTS), S5_SEGMENTS)
    strided = lambda i: pl.ds(i, S5_SEGMENTS, stride=seg)
    streams = range(nb)

    def loop(body, init):
        if seg <= 8:
            for i in range(seg):
                init = body(i, init)
            return init
        return lax.fori_loop(0, seg, body, init, unroll=8)

    def interleave(i, c):
        for bb in streams:
            up[rows(bb, i), :] = u_ref[bb, strided(i), :]
        return c

    loop(interleave, 0)
    u = up[...]
    ub = u.astype(BF16)
    h[...] = jnp.dot(ub, wb_ref[0], preferred_element_type=F32)
    bcast = lambda ref, j: jnp.broadcast_to(ref[0, :, lb(j)], (S5_SEGMENTS, LANES))
    ar = [bcast(ar_ref, j) for j in range(nlb)]
    ai = [bcast(ai_ref, j) for j in range(nlb)]
    chains = [(bb, j) for bb in streams for j in range(nlb)]
    n_ch = len(chains)

    def advance(i, c, store):
        nr, ni = [], []
        for q, (bb, j) in enumerate(chains):
            hr, hi = c[q], c[n_ch + q]
            r = ar[j] * hr - ai[j] * hi + h[rows(bb, i), re(j)]
            m = ar[j] * hi + ai[j] * hr + h[rows(bb, i), im(j)]
            if store:
                h[rows(bb, i), re(j)] = r
                h[rows(bb, i), im(j)] = m
            nr.append(r)
            ni.append(m)
        return tuple(nr + ni)

    zero = jnp.zeros((S5_SEGMENTS, LANES), F32)
    ends = loop(lambda i, c: advance(i, c, False), (zero,) * (2 * n_ch))

    starts_r, starts_i = [], []
    for q, (bb, j) in enumerate(chains):
        er, ei = ends[q], ends[n_ch + q]
        pr, pi = pr_ref[0, :, lb(j)], pi_ref[0, :, lb(j)]
        cr, ci = h0r_ref[bb, 0, :, lb(j)], h0i_ref[bb, 0, :, lb(j)]
        start_r, start_i = [cr], [ci]
        for s in range(S5_SEGMENTS - 1):
            cr, ci = (pr * cr - pi * ci + er[s:s + 1, :], pr * ci + pi * cr + ei[s:s + 1, :])
            start_r.append(cr)
            start_i.append(ci)
        starts_r.append(jnp.concatenate(start_r, axis=0))
        starts_i.append(jnp.concatenate(start_i, axis=0))

    last = loop(lambda i, c: advance(i, c, True), tuple(starts_r + starts_i))
    y = d_ref[...] * u
    for q, (bb, j) in enumerate(chains):
        hr_out[bb, 0, :, lb(j)] = last[q][S5_SEGMENTS - 1:, :]
        hi_out[bb, 0, :, lb(j)] = last[n_ch + q][S5_SEGMENTS - 1:, :]
    y += jnp.dot(h[...].astype(BF16), wc_ref[0], preferred_element_type=F32)
    up[...] = y

    def deinterleave(i, c):
        for bb in streams:
            y_ref[bb, strided(i), :] = up[rows(bb, i), :]
        return c

    loop(deinterleave, 0)


def _s5(proj, prm, layer, h0r, h0i):
    n_b, n_l, _ = proj.shape
    nblk = prm["nblk"]
    cin = prm["wb"].shape[1]
    b0 = layer * nblk
    seg = n_l // S5_SEGMENTS
    nb = n_b if seg <= 8 else 1
    assert seg * S5_SEGMENTS == n_l and cin == LANES
    wspec = lambda shape: pl.BlockSpec((1,) + shape, lambda b, s: (b0 + s, 0, 0))
    st_spec = pl.BlockSpec((nb, 1, 1, S5_BLOCK), lambda b, s: (b, s, 0, 0))
    return pl.pallas_call(
        functools.partial(_s5_kernel, seg=seg, nb=nb),
        grid=(n_b // nb, nblk),
        in_specs=[pl.BlockSpec((nb, n_l, cin), lambda b, s: (b, 0, s)),
                  wspec((cin, 2 * S5_BLOCK)),
                  wspec((1, S5_BLOCK)), wspec((1, S5_BLOCK)), wspec((1, S5_BLOCK)), wspec((1, S5_BLOCK)),
                  st_spec, st_spec,
                  wspec((2 * S5_BLOCK, cin)),
                  pl.BlockSpec((1, cin), lambda b, s: (0, b0 + s))],
        out_specs=[pl.BlockSpec((nb, n_l, cin), lambda b, s: (b, 0, s)), st_spec, st_spec],
        out_shape=[jax.ShapeDtypeStruct((n_b, n_l, nblk * cin), F32),
                   jax.ShapeDtypeStruct((n_b, nblk, 1, S5_BLOCK), F32),
                   jax.ShapeDtypeStruct((n_b, nblk, 1, S5_BLOCK), F32)],
        scratch_shapes=[pltpu.VMEM((nb * n_l, 2 * S5_BLOCK), F32), pltpu.VMEM((nb * n_l, LANES), F32)],
        compiler_params=_cparams("parallel", "parallel"),
        name="s5_scan",
    )(proj, prm["wb"], prm["ar"], prm["ai"], *prm["pows"][seg], h0r, h0i, prm["wc"], prm["d"])


def _s5_params(lam_re, lam_im, log_dt, b_re, b_im, c_re, c_im, d_skip, seg_lens):
    depth = lam_re.shape[0]
    merge = lambda t: t.reshape((-1,) + t.shape[2:])
    lam_re, lam_im, log_dt, b_re, b_im, c_re, c_im = map(merge, (lam_re, lam_im, log_dt, b_re, b_im, c_re, c_im))
    n_g, n_p = lam_re.shape
    dt = jnp.exp(log_dt)[:, None]
    e = jnp.exp(lam_re * dt)
    ab_re, ab_im = e * jnp.cos(lam_im * dt), e * jnp.sin(lam_im * dt)
    den = lam_re * lam_re + lam_im * lam_im
    nr, ni = ab_re - 1.0, ab_im
    q_re = (nr * lam_re + ni * lam_im) / den
    q_im = (ni * lam_re - nr * lam_im) / den
    bb_re = q_re[..., None] * b_re - q_im[..., None] * b_im
    bb_im = q_re[..., None] * b_im + q_im[..., None] * b_re
    gpb = S5_BLOCK // n_p
    nblk = n_g // gpb
    eye = jnp.eye(gpb, dtype=F32)

    def pack_b(t):
        t = t.reshape(nblk, gpb, n_p, SSM_GROUP)
        return jnp.einsum("sgpc,gh->sgchp", t, eye).reshape(nblk, gpb * SSM_GROUP, gpb * n_p).astype(BF16)

    def pack_c(t):
        t = t.reshape(nblk, gpb, SSM_GROUP, n_p)
        return jnp.einsum("sgcp,gh->shpgc", t, eye).reshape(nblk, gpb * n_p, gpb * SSM_GROUP).astype(BF16)

    flat = lambda t: t.reshape(nblk, 1, gpb * n_p)
    nlb = gpb * n_p // LANES
    wb = jnp.stack([pack_b(bb_re).reshape(nblk, -1, nlb, LANES), pack_b(bb_im).reshape(nblk, -1, nlb, LANES)], axis=3)
    wc = jnp.stack([pack_c(c_re).reshape(nblk, nlb, LANES, -1), -pack_c(c_im).reshape(nblk, nlb, LANES, -1)], axis=2)
    out = {"wb": wb.reshape(nblk, -1, 2 * gpb * n_p), "wc": wc.reshape(nblk, 2 * gpb * n_p, -1),
           "ar": flat(ab_re), "ai": flat(ab_im), "d": d_skip.reshape(1, -1), "pows": {}, "nblk": nblk // depth}
    for seg in seg_lens:
        assert seg & (seg - 1) == 0
        pr, pi = ab_re, ab_im
        for _ in range(int(math.log2(seg))):
            pr, pi = pr * pr - pi * pi, 2.0 * pr * pi
        out["pows"][seg] = (flat(pr), flat(pi))
    return out


def _att_prompt_kernel(q_ref, k_ref, v_ref, bias_ref, o_ref, kpad, vpad, *, n_l, scale):
    zpad = jnp.zeros((ATT_REACH, LANES), BF16)
    kpad[0:ATT_REACH, :] = zpad
    vpad[0:ATT_REACH, :] = zpad
    kpad[ATT_REACH:, :] = k_ref[0].astype(BF16)
    vpad[ATT_REACH:, :] = v_ref[0].astype(BF16)
    lane = lax.broadcasted_iota(jnp.int32, (1, LANES), 1)
    heads = (lane < HEAD_DIM, lane >= HEAD_DIM)
    col = lax.broadcasted_iota(jnp.int32, (1, K_WINDOW), 1)
    nt = (((1,), (1,)), ((), ()))

    def block(r0, n_masked):
        q = q_ref[0, pl.ds(r0, Q_BLOCK), :] * scale
        kw = kpad[pl.ds(r0, K_WINDOW), :]
        vw = vpad[pl.ds(r0, K_WINDOW), :]
        s = [lax.dot_general(jnp.where(hd, q, 0.0).astype(BF16), kw, nt, preferred_element_type=F32) + bias_ref[0, h]
             for h, hd in enumerate(heads)]
        if n_masked:
            s = [jnp.where(col < n_masked, NEG_INF, x) for x in s]
        m = [jnp.max(x, axis=-1, keepdims=True) for x in s]
        p = [jnp.exp(x - mx) for x, mx in zip(s, m)]
        l = [jnp.sum(x, axis=-1, keepdims=True) for x in p]
        o = [jnp.dot(x.astype(BF16), vw, preferred_element_type=F32) / lx for x, lx in zip(p, l)]
        o_ref[0, pl.ds(r0, Q_BLOCK), :] = jnp.where(heads[0], o[0], o[1]).astype(o_ref.dtype)

    n_blocks = n_l // Q_BLOCK
    n_first = min(n_blocks, ATT_REACH // Q_BLOCK)
    for qb in range(n_first):
        block(qb * Q_BLOCK, ATT_REACH - qb * Q_BLOCK)

    def body(qb, carry):
        block(pl.multiple_of(qb * Q_BLOCK, Q_BLOCK), 0)
        return carry

    lax.fori_loop(n_first, n_blocks, body, 0, unroll=max(1, n_blocks - n_first))


def _att_prompt(proj, bias, layer, q_col, k_col, v_col, n_pairs):
    n_b, n_l, _ = proj.shape
    assert n_l % Q_BLOCK == 0
    p0 = layer * n_pairs
    spec = lambda c0: pl.BlockSpec((1, n_l, LANES), lambda b, p: (b, 0, c0 + p))
    return pl.pallas_call(
        functools.partial(_att_prompt_kernel, n_l=n_l, scale=HEAD_DIM ** -0.5),
        grid=(n_b, n_pairs),
        in_specs=[spec(q_col), spec(k_col), spec(v_col),
                  pl.BlockSpec((1, 2, Q_BLOCK, K_WINDOW), lambda b, p: (p0 + p, 0, 0, 0))],
        out_specs=pl.BlockSpec((1, n_l, LANES), lambda b, p: (b, 0, p)),
        out_shape=jax.ShapeDtypeStruct((n_b, n_l, n_pairs * LANES), BF16),
        scratch_shapes=[pltpu.VMEM((n_l + ATT_REACH, LANES), BF16), pltpu.VMEM((n_l + ATT_REACH, LANES), BF16)],
        compiler_params=_cparams("parallel", "parallel"),
        name="att_prompt",
    )(proj, proj, proj, bias)


def _bias_kernel(g_ref, bp_ref, bs_ref):
    g = g_ref[0]
    n_q, n_k = bp_ref.shape[1], bp_ref.shape[2]
    toep = pltpu.roll(jnp.broadcast_to(g, (n_q, g.shape[1])), 0, 1, stride=1, stride_axis=0)[:, :n_k]
    qc = lax.broadcasted_iota(jnp.int32, (n_q, n_k), 0) // CHUNK
    kc = lax.broadcasted_iota(jnp.int32, (n_q, n_k), 1) // CHUNK
    bp_ref[0] = jnp.where((kc >= qc) & (kc <= qc + LEFT_CHUNKS), toep, NEG_INF)
    n_s, n_ws = bs_ref.shape[1], bs_ref.shape[2]
    bs_ref[0] = pltpu.roll(jnp.broadcast_to(g, (n_s, g.shape[1])), 0, 1, stride=1, stride_axis=0)[:, :n_ws]


def _rel_bias(table, n_s, n_w):
    n_h = table.shape[0]
    assert n_w == ATT_REACH
    width = 1024
    assert width >= K_WINDOW + Q_BLOCK and width % LANES == 0
    d = np.arange(width)
    d = np.where(d < width - Q_BLOCK, d, d - width)
    idx = np.clip(ATT_REACH - d, -REL_CLIP, REL_CLIP) + REL_CLIP
    diag = table[:, idx].astype(F32).reshape(n_h, 1, width)
    return pl.pallas_call(
        _bias_kernel,
        grid=(n_h,),
        in_specs=[pl.BlockSpec((1, 1, width), lambda h: (h, 0, 0))],
        out_specs=[pl.BlockSpec((1, Q_BLOCK, K_WINDOW), lambda h: (h, 0, 0)),
                   pl.BlockSpec((1, n_s, n_w + n_s), lambda h: (h, 0, 0))],
        out_shape=[jax.ShapeDtypeStruct((n_h, Q_BLOCK, K_WINDOW), F32),
                   jax.ShapeDtypeStruct((n_h, n_s, n_w + n_s), F32)],
        compiler_params=_cparams("parallel"),
        name="rel_bias",
    )(diag)


def _att_step_kernel(q_ref, kn_ref, vn_ref, kc_ref, vc_ref, bc_ref, bn_ref, o_ref, *, n_h, scale):
    n_s = q_ref.shape[1]
    d = q_ref.shape[2]
    q = q_ref[0] * scale
    qt = jnp.concatenate([q] * n_h, axis=0)
    rh = lax.broadcasted_iota(jnp.int32, (n_h * n_s, d), 0) // n_s
    ch = lax.broadcasted_iota(jnp.int32, (n_h * n_s, d), 1) // HEAD_DIM
    qbd = jnp.where(rh == ch, qt, 0.0).astype(BF16)
    nt = (((1,), (1,)), ((), ()))
    s1 = lax.dot_general(qbd, kc_ref[0, 0].astype(BF16), nt, preferred_element_type=F32) + bc_ref[...]
    s2 = lax.dot_general(qbd, kn_ref[0].astype(BF16), nt, preferred_element_type=F32) + bn_ref[...]
    m = jnp.maximum(jnp.max(s1, axis=-1, keepdims=True), jnp.max(s2, axis=-1, keepdims=True))
    p1 = jnp.exp(s1 - m)
    p2 = jnp.exp(s2 - m)
    l = jnp.sum(p1, axis=-1, keepdims=True) + jnp.sum(p2, axis=-1, keepdims=True)
    o = (jnp.dot(p1.astype(BF16), vc_ref[0, 0].astype(BF16), preferred_element_type=F32)
         + jnp.dot(p2.astype(BF16), vn_ref[0].astype(BF16), preferred_element_type=F32)) / l
    first = lax.broadcasted_iota(jnp.int32, (1, LANES), 1) < HEAD_DIM
    for j in range(d // LANES):
        blk = o[:, j * LANES:(j + 1) * LANES]
        lo = blk[(2 * j) * n_s:(2 * j + 1) * n_s, :]
        hi = blk[(2 * j + 1) * n_s:(2 * j + 2) * n_s, :]
        o_ref[0, :, j * LANES:(j + 1) * LANES] = jnp.where(first, lo, hi).astype(o_ref.dtype)


def _att_step(proj, k_cache, v_cache, layer, bias_c, bias_n, q_col, k_col, v_col, n_h):
    n_b, n_s, _ = proj.shape
    n_w, d = k_cache.shape[2], k_cache.shape[3]
    spec = lambda c0: pl.BlockSpec((1, n_s, d), lambda b: (b, 0, c0))
    cspec = pl.BlockSpec((1, 1, n_w, d), lambda b: (layer, b, 0, 0))
    bspec = lambda a: pl.BlockSpec((None,) + a.shape[1:], lambda b: (layer, 0, 0))
    return pl.pallas_call(
        functools.partial(_att_step_kernel, n_h=n_h, scale=HEAD_DIM ** -0.5),
        grid=(n_b,),
        in_specs=[spec(q_col), spec(k_col), spec(v_col), cspec, cspec, bspec(bias_c), bspec(bias_n)],
        out_specs=pl.BlockSpec((1, n_s, d), lambda b: (b, 0, 0)),
        out_shape=jax.ShapeDtypeStruct((n_b, n_s, d), BF16),
        compiler_params=_cparams("parallel"),
        name="att_step",
    )(proj, proj, proj, k_cache, v_cache, bias_c, bias_n)


def _rwkv_prep_kernel(r_ref, k_ref, v_ref, u_ref, sh_ref, mu_ref, w0_ref, w1_ref, w2_ref, a0_ref, a1_ref, a2_ref,
                      kk_ref, ka_ref, nkk_out, w_out, b_out, k_out, r_out, v_out, prev):
    j = pl.program_id(1)

    @pl.when(j == 0)
    def _():
        prev[...] = sh_ref[0]

    n_r = r_ref.shape[1]
    row0 = lax.broadcasted_iota(jnp.int32, (n_r, 1), 0) == 0

    def delta(x, i):
        sh = jnp.where(row0, prev[i:i + 1, :], pltpu.roll(x, 1, axis=0))
        prev[i:i + 1, :] = x[n_r - 1:, :]
        return sh - x

    rp, kp, vp, up = r_ref[0], k_ref[0], v_ref[0], u_ref[0]
    du = delta(up, 3)
    r = rp + delta(rp, 0) * mu_ref[0:1, :]
    k = kp + delta(kp, 1) * mu_ref[1:2, :]
    v = vp + delta(vp, 2) * mu_ref[2:3, :]
    xw = up + du * mu_ref[3:4, :]
    xa = up + du * mu_ref[4:5, :]
    dot = lambda a, b: jnp.dot(a.astype(BF16), b, preferred_element_type=F32)
    z = -(w0_ref[...] + dot(jnp.tanh(dot(xw, w1_ref[...])), w2_ref[...]))
    softplus = jnp.maximum(z, 0.0) + jnp.log(1.0 + jnp.exp(-jnp.abs(z)))
    w = -softplus - 0.5
    a = jax.nn.sigmoid(a0_ref[...] + dot(dot(xa, a1_ref[...]), a2_ref[...]))
    kk = k * kk_ref[...]
    ss = _head_sum(kk * kk, _block_ones(kk.shape[1]))
    kk = kk * lax.rsqrt(jnp.maximum(ss, 1e-24))
    nkk_out[0] = (-kk).astype(nkk_out.dtype)
    w_out[0] = -jnp.exp(w)
    b_out[0] = (kk * a).astype(b_out.dtype)
    k_out[0] = (k * (1.0 + (a - 1.0) * ka_ref[...])).astype(k_out.dtype)
    r_out[0] = r.astype(r_out.dtype)
    v_out[0] = v.astype(v_out.dtype)


def _layer_spec(a, layer):
    return pl.BlockSpec((None,) + a.shape[1:], lambda *_: (layer,) + (0,) * (a.ndim - 1),
                        pipeline_mode=pl.Buffered(1))


def _rwkv_prep(proj, shift0, lw, col0):
    n_b, n_l, _ = proj.shape
    d = shift0.shape[2]
    tr = min(n_l, 512)
    assert n_l % tr == 0
    spec = lambda c: pl.BlockSpec((1, tr, d), lambda b, j: (b, j, c))
    full = lambda a: _layer_spec(a, lw["layer"])
    ws = [lw["rwkv_mu"], lw["rwkv_w0"], lw["rwkv_w1"], lw["rwkv_w2"], lw["rwkv_a0"], lw["rwkv_a1"], lw["rwkv_a2"],
          lw["rwkv_k_k"], lw["rwkv_k_a"]]
    ospec = pl.BlockSpec((1, tr, d), lambda b, j: (b, j, 0))
    return pl.pallas_call(
        _rwkv_prep_kernel,
        grid=(n_b, n_l // tr),
        in_specs=[spec(col0), spec(col0 + 1), spec(col0 + 2), spec(col0 + 3),
                  pl.BlockSpec((1, 4, d), lambda b, j: (b, 0, 0))] + [full(a) for a in ws],
        out_specs=[ospec] * 6,
        out_shape=[jax.ShapeDtypeStruct((n_b, n_l, d), F32 if i == 1 else BF16) for i in range(6)],
        scratch_shapes=[pltpu.VMEM((4, d), F32)],
        compiler_params=_cparams("parallel", "arbitrary"),
        name="rwkv_prep",
    )(proj, proj, proj, proj, shift0, *ws)


def _rwkv_rec_kernel(nkk_ref, lw_ref, b_ref, k_ref, r_ref, v_ref, s0_ref, rk_ref, g_ref, beta_ref,
                     y_ref, s_out, state, *, nbk, nch, c):
    tb = pl.program_id(2)

    @pl.when(tb == 0)
    def _():
        state[...] = s0_ref[:, 0]

    c2 = 2 * c
    lane = lax.broadcasted_iota(jnp.int32, (1, LANES), 1)
    m0 = (lane < HEAD_DIM).astype(F32)
    m1 = 1.0 - m0
    by_head = lambda x: jnp.concatenate([x * m0, x * m1], axis=0)
    mm = lambda a, b: jnp.dot(a.astype(BF16), b.astype(BF16), preferred_element_type=F32)
    mm_nt = lambda a, b: lax.dot_general(a.astype(BF16), b.astype(BF16), (((1,), (1,)), ((), ())),
                                         preferred_element_type=F32)
    ii = lambda shape, d: lax.broadcasted_iota(jnp.int32, shape, d)
    tri = (ii((c, c), 0) >= ii((c, c), 1)).astype(BF16)
    rr, cc = ii((c2, c2), 0), ii((c2, c2), 1)
    strict = (rr // c == cc // c) & (cc % c < rr % c)
    incl = ii((c, c2), 1) % c <= ii((c, c2), 0)
    eye_t = (rr == cc).astype(F32)
    eye_s = (ii((LANES, LANES), 0) == ii((LANES, LANES), 1)).astype(F32)
    probs = [(bb, ci) for ci in range(nch) for bb in range(nbk)]
    rows = lambda ref, q: ref[q[0], q[1] * c:(q[1] + 1) * c, :].astype(F32)
    each = lambda fn, *lists: [fn(*xs) for xs in zip(*lists)]

    lw = [rows(lw_ref, q) for q in probs]

    def cumsum(x):
        hi = x.astype(BF16)
        lo = (x - hi.astype(F32)).astype(BF16)
        return jnp.dot(tri, hi, preferred_element_type=F32) + jnp.dot(tri, lo, preferred_element_type=F32)

    cum = each(cumsum, lw)
    tot = [x[c - 1:c, :] for x in cum]
    g_c = [jnp.exp(x) for x in tot]
    a_t = each(lambda q, x, l: rows(nkk_ref, q) * jnp.exp(x - l), probs, cum, lw)
    r_t = each(lambda q, x: rows(r_ref, q) * jnp.exp(x), probs, cum)
    ginv = [jnp.exp(-x) for x in cum]
    ghat = each(lambda x, t: jnp.exp(t - x), cum, tot)
    kx = [rows(k_ref, q) for q in probs]
    bx = [rows(b_ref, q) for q in probs]
    a_bd = [by_head(x) for x in a_t]
    lhs = each(lambda a, r: jnp.concatenate([a, r], axis=0), a_bd, r_t)
    if c2 % LANES == 0:
        sc = each(lambda l, k, b, gi: mm_nt(l, jnp.concatenate([by_head(k * gi), by_head(b * gi)], axis=0)),
                  lhs, kx, bx, ginv)
        sc_k, sc_b = [x[:, :c2] for x in sc], [x[:, c2:] for x in sc]
    else:
        sc_k = each(lambda l, k, gi: mm_nt(l, by_head(k * gi)), lhs, kx, ginv)
        sc_b = each(lambda l, b, gi: mm_nt(l, by_head(b * gi)), lhs, bx, ginv)
    l_ak = [jnp.where(strict, x[:c2], 0.0) for x in sc_k]
    l_ab = [jnp.where(strict, x[:c2], 0.0) for x in sc_b]
    m_rk = [jnp.where(incl, x[c2:], 0.0) for x in sc_k]
    m_rb = [jnp.where(incl, x[c2:], 0.0) for x in sc_b]
    col_t = ii((1, c2), 1)
    t0, t1 = (col_t < c).astype(F32), (col_t >= c).astype(F32)
    blockdiag = lambda x: jnp.concatenate([x * t0, x * t1], axis=0)
    pw = [x[:c] + x[c:] for x in l_ab]
    inv = [x + eye_t[:c] + eye_t[c:] for x in pw]
    for _ in range(int(math.log2(c)) - 1):
        pw = each(lambda x: mm(x, blockdiag(x)), pw)
        inv = each(lambda p, x: p + mm(p, blockdiag(x)), inv, pw)
    inv = [blockdiag(x) for x in inv]
    v_bd = [by_head(rows(v_ref, q)) for q in probs]
    x_in = each(lambda a, l, v: jnp.concatenate([a, mm(l, v)], axis=1), a_bd, l_ak, v_bd)
    uu = each(mm, inv, x_in)
    bh_t = each(lambda b, gh: by_head(b * gh).T, bx, ghat)
    kh_t = each(lambda k, gh: by_head(k * gh).T, kx, ghat)
    bu = each(mm, bh_t, uu)
    g_mat = each(lambda gc, x: eye_s * gc + x[:, :LANES], g_c, bu)
    h_mat = each(lambda kt, v, x: mm(kt, v) + x[:, LANES:], kh_t, v_bd, bu)
    mu = each(mm, m_rb, uu)
    y_a = each(lambda r, x: r + x[:, :LANES], r_t, mu)
    y_b = each(lambda mk, v, x: mm(mk, v) + x[:, LANES:], m_rk, v_bd, mu)

    s_cur = [state[bb] for bb in range(nbk)]
    ys = {}
    for i, (bb, ci) in enumerate(probs):
        both = mm(jnp.concatenate([y_a[i], g_mat[i]], axis=0), s_cur[bb])
        ys[(bb, ci)] = both[:c] + y_b[i]
        s_cur[bb] = both[c:] + h_mat[i]
    for bb in range(nbk):
        state[bb] = s_cur[bb]

    @pl.when(tb == pl.num_programs(2) - 1)
    def _():
        s_out[:, 0] = state[...]

    ones = _block_ones(LANES)
    inv_n = 1.0 / HEAD_DIM
    head_sum = lambda x: jnp.dot(x.astype(BF16), ones, preferred_element_type=F32)
    for bb in range(nbk):
        y = jnp.concatenate([ys[(bb, ci)] for ci in range(nch)], axis=0) if nch > 1 else ys[(bb, 0)]
        mean = head_sum(y) * inv_n
        yc = y - mean
        var = head_sum(yc * yc) * inv_n
        yn = yc * lax.rsqrt(var + GN_EPS) * g_ref[...] + beta_ref[...]
        f32 = lambda ref: ref[bb].astype(F32)
        bonus = head_sum(f32(r_ref) * f32(k_ref) * rk_ref[...]) * f32(v_ref)
        y_ref[bb] = (yn + bonus).astype(y_ref.dtype)


def _rwkv_rec(prep, s0, layer, r_k, lnx_g, lnx_b):
    n_b, n_l, d = prep[0].shape
    n_pairs = d // LANES
    c = min(n_l, RWKV_CHUNK)
    nch = min(n_l // c, RWKV_GROUP // 2)
    nbk = min(n_b, RWKV_GROUP // nch)
    lb = nch * c
    assert n_l % lb == 0 and n_b % nbk == 0 and c & (c - 1) == 0
    spec = pl.BlockSpec((nbk, lb, LANES), lambda g, p, t: (g, t, p))
    sspec = pl.BlockSpec((nbk, 1, LANES, LANES), lambda g, p, t: (g, p, 0, 0))
    vec = pl.BlockSpec((None, 1, LANES), lambda g, p, t: (layer, 0, p))
    return pl.pallas_call(
        functools.partial(_rwkv_rec_kernel, nbk=nbk, nch=nch, c=c),
        grid=(n_b // nbk, n_pairs, n_l // lb),
        in_specs=[spec] * 6 + [sspec, vec, vec, vec],
        out_specs=[spec, sspec],
        out_shape=[jax.ShapeDtypeStruct((n_b, n_l, d), BF16),
                   jax.ShapeDtypeStruct((n_b, n_pairs, LANES, LANES), F32)],
        scratch_shapes=[pltpu.VMEM((nbk, LANES, LANES), F32)],
        compiler_params=_cparams("parallel", "parallel", "arbitrary"),
        name="rwkv_rec",
    )(*prep, s0, r_k, lnx_g, lnx_b)


def _pack_pairs(s):
    n_b, n_h, n, _ = s.shape
    st = jnp.swapaxes(s, -1, -2).reshape(n_b, n_h // 2, 2, n, n)
    z = jnp.zeros_like(st[:, :, 0])
    return jnp.concatenate([jnp.concatenate([st[:, :, 0], z], axis=-1),
                            jnp.concatenate([z, st[:, :, 1]], axis=-1)], axis=-2)


def _unpack_pairs(s):
    n_b, n_p, n2, _ = s.shape
    n = n2 // 2
    st = jnp.stack([s[:, :, :n, :n], s[:, :, n:, n:]], axis=2)
    return jnp.swapaxes(st, -1, -2).reshape(n_b, 2 * n_p, n, n)


def _mix_kernel(x_ref, ys_ref, gs_ref, ya_ref, ga_ref, yc_ref, gc_ref, wg_ref, bg_ref, wo_ref, lg_ref, lb_ref,
                y_ref, *maybe_yb_ref, alpha, d_ssm, d_att):
    silu = lambda g: g * jax.nn.sigmoid(g)
    dot = lambda a, b: jnp.dot(a.astype(BF16), b, preferred_element_type=F32)
    n_r = x_ref.shape[0]
    n_part = 2 if n_r % 32 == 0 else 1
    for part in range(n_part):
        rows = slice(part * n_r // n_part, (part + 1) * n_r // n_part)
        ys = ys_ref[rows, :]
        z = 0.5 * ys * (1.0 + jnp.tanh(math.sqrt(2.0 / math.pi) * (ys + 0.044715 * (ys * ys * ys))))
        m_s = z * jax.nn.sigmoid(dot(z, wg_ref[...]) + bg_ref[...]) * silu(gs_ref[rows, :])
        m_a = ya_ref[rows, :] * silu(ga_ref[rows, :])
        m_c = yc_ref[rows, :] * silu(gc_ref[rows, :])
        out = (dot(m_s, wo_ref[0:d_ssm, :]) + dot(m_a, wo_ref[d_ssm:d_ssm + d_att, :])
               + dot(m_c, wo_ref[d_ssm + d_att:, :]))
        h = alpha * x_ref[rows, :] + out
        mu = jnp.mean(h, axis=-1, keepdims=True)
        hc = h - mu
        var = jnp.mean(hc * hc, axis=-1, keepdims=True)
        y = hc * lax.rsqrt(var + LN_EPS) * lg_ref[...] + lb_ref[...]
        y_ref[rows, :] = y
        for yb_ref in maybe_yb_ref:
            yb_ref[rows, :] = y.astype(BF16)


def _mix(x, proj, ys, ya, yc, lw, alpha, gs_col, ga_col, gc_col, want_bf16):
    m, d = x.shape
    d_ssm, d_att, d_rw = ys.shape[1], ya.shape[1], yc.shape[1]
    tm = min(m, 512)
    assert m % tm == 0
    row = lambda w, c: pl.BlockSpec((tm, w), lambda i: (i, c))
    full = lambda a: _layer_spec(a, lw["layer"])
    ws = [lw["w_glu"], lw["b_glu"], lw["w_out"], lw["ln_g"], lw["ln_b"]]
    return pl.pallas_call(
        functools.partial(_mix_kernel, alpha=alpha, d_ssm=d_ssm, d_att=d_att),
        grid=(m // tm,),
        in_specs=[row(d, 0), row(d_ssm, 0), row(d_ssm, gs_col), row(d_att, 0), row(d_att, ga_col),
                  row(d_rw, 0), row(d_rw, gc_col)] + [full(a) for a in ws],
        out_specs=[row(d, 0)] * (2 if want_bf16 else 1),
        out_shape=[jax.ShapeDtypeStruct((m, d), F32)] + [jax.ShapeDtypeStruct((m, d), BF16)] * want_bf16,
        compiler_params=_cparams("parallel"),
        name="mix_out",
    )(x, ys, proj, ya, proj, yc, proj, *ws)


def _layer(x, proj, st, lw, s5p, alpha):
    n_b, n_l, d_model = x.shape
    d_ssm = lw["w_glu"].shape[1]
    n_h = lw["n_att_heads"]
    d_att = n_h * HEAD_DIM
    d_rw = lw["rwkv_w0"].shape[2]
    n_rh = d_rw // HEAD_DIM
    n_blk = d_ssm // LANES
    proj = proj.reshape(n_b, n_l, -1)
    q0 = 2 * d_ssm
    r0 = q0 + 4 * d_att
    assert q0 % d_att == 0 and r0 % d_rw == 0 and d_ssm == d_rw

    if st is None:
        zeros = jnp.zeros((n_b, n_blk, 1, S5_BLOCK), F32)
        h0r = h0i = zeros
        wkv0 = jnp.zeros((n_b, n_rh // 2, LANES, LANES), F32)
        shift0 = jnp.zeros((n_b, 4, d_rw), F32)
    else:
        k_cache, v_cache, h0r, h0i, wkv0, shift0 = st
        h0r = h0r.reshape(n_b, n_blk, 1, S5_BLOCK)
        h0i = h0i.reshape(n_b, n_blk, 1, S5_BLOCK)
        wkv0 = _pack_pairs(wkv0)
        shift0 = shift0.reshape(n_b, 4, d_rw)

    ys, h_re, h_im = _s5(proj, s5p, lw["layer"], h0r, h0i)

    n_keep = min(ATT_REACH, n_l)
    k_rows = proj[:, n_l - n_keep:, q0 + d_att:q0 + 2 * d_att].reshape(n_b, n_keep, n_h, HEAD_DIM)
    v_rows = proj[:, n_l - n_keep:, q0 + 2 * d_att:q0 + 3 * d_att].reshape(n_b, n_keep, n_h, HEAD_DIM)
    if st is None:
        pl0 = q0 // LANES
        ya = _att_prompt(proj, lw["bias_prompt"], lw["layer"], pl0, pl0 + d_att // LANES, pl0 + 2 * d_att // LANES,
                         n_h // 2)
    else:
        c0 = q0 // d_att
        ya = _att_step(proj, k_cache, v_cache, lw["layer"], lw["bias_cache"], lw["bias_new"], c0, c0 + 1, c0 + 2, n_h)

    prep = _rwkv_prep(proj, shift0, lw, r0 // d_rw)
    yc, wkv = _rwkv_rec(prep, wkv0, lw["layer"], lw["rwkv_r_k"], lw["rwkv_lnx_g"], lw["rwkv_lnx_b"])
    shift = proj[:, n_l - 1, r0:r0 + 4 * d_rw]

    m = n_b * n_l
    outs = _mix(x.reshape(m, d_model), proj.reshape(m, -1), ys.reshape(m, d_ssm), ya.reshape(m, d_att),
                yc.reshape(m, d_rw), lw, alpha, 1, (q0 + 3 * d_att) // d_att, (r0 + 4 * d_rw) // d_rw,
                want_bf16=not lw["last"])
    n_g = d_ssm // SSM_GROUP
    states = (k_rows, v_rows, h_re.reshape(n_b, n_g, SSM_STATE), h_im.reshape(n_b, n_g, SSM_STATE),
              _unpack_pairs(wkv), shift)
    y, yb = outs if len(outs) == 2 else (outs[0], None)
    return y.reshape(n_b, n_l, d_model), yb, states


def kernel(x_prompt, x_sample, cache_att_k, cache_att_v, state_ssm_re, state_ssm_im, state_rwkv, state_rwkv_shift, w_in, ssm_lam_re, ssm_lam_im, ssm_log_dt, ssm_b_re, ssm_b_im, ssm_c_re, ssm_c_im, ssm_d, ssm_w_glu, ssm_b_glu, att_rel_bias, rwkv_mu, rwkv_w0, rwkv_w1, rwkv_w2, rwkv_a0, rwkv_a1, rwkv_a2, rwkv_k_k, rwkv_k_a, rwkv_r_k, rwkv_lnx_g, rwkv_lnx_b, w_out, ln_g, ln_b):
    depth = w_in.shape[0]
    alpha = (2.0 * depth) ** 0.25
    y_p, y_s = x_prompt, x_sample
    yb_p = yb_s = None
    seg_lens = sorted({x_prompt.shape[1] // S5_SEGMENTS, x_sample.shape[1] // S5_SEGMENTS})
    p_st, s_st = [], []
    n_d, n_sb, n_w = cache_att_k.shape[:3]
    n_s = x_sample.shape[1]
    n_h = att_rel_bias.shape[1]
    k_cache = cache_att_k.reshape(n_d, n_sb, n_w, -1)
    v_cache = cache_att_v.reshape(n_d, n_sb, n_w, -1)
    row = lambda a: a.reshape(depth, 1, -1)
    bias_p, bias_s = _rel_bias(att_rel_bias.reshape(depth * n_h, -1), n_s, n_w)
    bias_s = bias_s.reshape(depth, n_h * n_s, n_w + n_s)
    params = {"w_in": w_in, "n_att_heads": n_h, "w_glu": ssm_w_glu.astype(BF16), "b_glu": row(ssm_b_glu),
              "rwkv_mu": rwkv_mu, "rwkv_w0": row(rwkv_w0), "rwkv_w1": rwkv_w1.astype(BF16),
              "rwkv_w2": rwkv_w2.astype(BF16), "rwkv_a0": row(rwkv_a0), "rwkv_a1": rwkv_a1.astype(BF16),
              "rwkv_a2": rwkv_a2.astype(BF16), "rwkv_k_k": row(rwkv_k_k), "rwkv_k_a": row(rwkv_k_a),
              "rwkv_r_k": row(rwkv_r_k), "rwkv_lnx_g": row(rwkv_lnx_g), "rwkv_lnx_b": row(rwkv_lnx_b),
              "w_out": w_out.astype(BF16), "ln_g": row(ln_g), "ln_b": row(ln_b),
              "bias_prompt": bias_p.reshape(depth * n_h // 2, 2, Q_BLOCK, K_WINDOW),
              "bias_cache": bias_s[:, :, :n_w], "bias_new": bias_s[:, :, n_w:]}
    s5p = _s5_params(ssm_lam_re, ssm_lam_im, ssm_log_dt, ssm_b_re, ssm_b_im, ssm_c_re, ssm_c_im, ssm_d, seg_lens)
    for l in range(depth):
        lw = dict(params, layer=l, last=(l == depth - 1))
        if yb_p is None:
            flat = lambda a: a.reshape(-1, a.shape[-1])
            yb_p, yb_s = _to_bf16(flat(y_p)), _to_bf16(flat(y_s))
        proj_p, proj_s = _in_proj(yb_p, yb_s, w_in, l)
        y_p, yb_p, st_p = _layer(y_p, proj_p, None, lw, s5p, alpha)
        y_s, yb_s, st_s = _layer(y_s, proj_s, (k_cache, v_cache, state_ssm_re[l], state_ssm_im[l],
                                             state_rwkv[l], state_rwkv_shift[l]), lw, s5p, alpha)
        p_st.append(st_p)
        s_st.append(st_s)
    stacked = lambda states, i: jnp.stack([st[i] for st in states], axis=0)
    return (y_p, y_s) + tuple(stacked(p_st, i) for i in range(6)) + tuple(stacked(s_st, i) for i in range(6))
```

```python
import functools
import math

import jax
import jax.numpy as jnp
import numpy as np
from jax import lax
from jax.experimental import pallas as pl
from jax.experimental.pallas import tpu as pltpu

F32 = jnp.float32
BF16 = jnp.bfloat16

LANES = 128
SUBLANES = 8
VMEM_LIMIT = 56 * 1024 * 1024

CHUNK = 64
LEFT_CHUNKS = 8
ATT_REACH = LEFT_CHUNKS * CHUNK
REL_CLIP = 128
HEAD_DIM = 64
SSM_GROUP = 16
SSM_STATE = 64
RWKV_LORA = 64
NEG_INF = -1e30
GN_EPS = 64e-5
LN_EPS = 1e-5

RWKV_CHUNK = 64
RWKV_GROUP = 16
Q_BLOCK = 4 * CHUNK
K_WINDOW = Q_BLOCK + ATT_REACH
S5_SEGMENTS = SUBLANES
S5_BLOCK = 512


def _cparams(*sem):
    return pltpu.CompilerParams(dimension_semantics=sem, vmem_limit_bytes=VMEM_LIMIT)


def _block_ones(n):
    r = lax.broadcasted_iota(jnp.int32, (n, n), 0) // HEAD_DIM
    c = lax.broadcasted_iota(jnp.int32, (n, n), 1) // HEAD_DIM
    return (r == c).astype(BF16)


def _head_sum(x, ones):
    hi = x.astype(BF16)
    lo = (x - hi.astype(F32)).astype(BF16)
    return jnp.dot(hi, ones, preferred_element_type=F32) + jnp.dot(lo, ones, preferred_element_type=F32)


def _matmul_kernel(x_ref, xs_ref, w_ref, o_ref, os_ref, wb):
    @pl.when(pl.program_id(1) == 0)
    def _():
        wb[...] = w_ref[0].astype(BF16)

    o_ref[...] = jnp.dot(x_ref[...], wb[...], preferred_element_type=F32)

    @pl.when(pl.program_id(1) == pl.num_programs(1) - 1)
    def _():
        os_ref[...] = jnp.dot(xs_ref[...], wb[...], preferred_element_type=F32)


def _cast_kernel(x_ref, o_ref):
    o_ref[...] = x_ref[...].astype(o_ref.dtype)


def _to_bf16(x):
    m, k = x.shape
    tm = min(m, 1024)
    assert m % tm == 0
    return pl.pallas_call(
        _cast_kernel,
        grid=(m // tm,),
        in_specs=[pl.BlockSpec((tm, k), lambda i: (i, 0))],
        out_specs=pl.BlockSpec((tm, k), lambda i: (i, 0)),
        out_shape=jax.ShapeDtypeStruct((m, k), BF16),
        compiler_params=_cparams("parallel"),
        name="to_bf16",
    )(x)


def _in_proj(x, xs, w, layer):
    m, k = x.shape
    ms = xs.shape[0]
    n = w.shape[2]
    tm = min(m, 512)
    tn = 1536
    assert m % tm == 0 and n % tn == 0 and x.dtype == BF16 and xs.dtype == BF16
    return pl.pallas_call(
        _matmul_kernel,
        grid=(n // tn, m // tm),
        in_specs=[pl.BlockSpec((tm, k), lambda j, i: (i, 0)),
                  pl.BlockSpec((ms, k), lambda j, i: (0, 0)),
                  pl.BlockSpec((1, k, tn), lambda j, i: (layer, 0, j))],
        out_specs=[pl.BlockSpec((tm, tn), lambda j, i: (i, j)),
                   pl.BlockSpec((ms, tn), lambda j, i: (0, j))],
        out_shape=[jax.ShapeDtypeStruct((m, n), F32), jax.ShapeDtypeStruct((ms, n), F32)],
        scratch_shapes=[pltpu.VMEM((k, tn), BF16)],
        compiler_params=_cparams("parallel", "arbitrary"),
        name="in_proj",
    )(x, xs, w)


def _s5_kernel(u_ref, wb_ref, ar_ref, ai_ref, pr_ref, pi_ref, h0r_ref, h0i_ref,
               wc_ref, d_ref, y_ref, hr_out, hi_out, h, up, *, seg, nb):
    nlb = h.shape[1] // (2 * LANES)
    n_l = seg * S5_SEGMENTS
    lb = lambda j: slice(j * LANES, (j + 1) * LANES)
    re = lambda j: slice(2 * j * LANES, (2 * j + 1) * LANES)
    im = lambda j: slice((2 * j + 1) * LANES, (2 * j + 2) * LANES)
    aligned = lambda r: r if isinstance(r, int) else pl.multiple_of(r, S5_SEGMENTS)
    rows = lambda bb, i: pl.ds(aligned(bb * n_l + i * S5_SEGMENTS), S5_SEGMENTS)
    strided = lambda i: pl.ds(i, S5_SEGMENTS, stride=seg)
    streams = range(nb)

    def loop(body, init):
        if seg <= 8:
            for i in range(seg):
                init = body(i, init)
            return init
        return lax.fori_loop(0, seg, body, init, unroll=8)

    def interleave(i, c):
        for bb in streams:
            up[rows(bb, i), :] = u_ref[bb, strided(i), :]
        return c

    loop(interleave, 0)
    u = up[...]
    ub = u.astype(BF16)
    h[...] = jnp.dot(ub, wb_ref[0], preferred_element_type=F32)
    bcast = lambda ref, j: jnp.broadcast_to(ref[0, :, lb(j)], (S5_SEGMENTS, LANES))
    ar = [bcast(ar_ref, j) for j in range(nlb)]
    ai = [bcast(ai_ref, j) for j in range(nlb)]
    chains = [(bb, j) for bb in streams for j in range(nlb)]
    n_ch = len(chains)

    def advance(i, c, store):
        nr, ni = [], []
        for q, (bb, j) in enumerate(chains):
            hr, hi = c[q], c[n_ch + q]
            r = ar[j] * hr - ai[j] * hi + h[rows(bb, i), re(j)]
            m = ar[j] * hi + ai[j] * hr + h[rows(bb, i), im(j)]
            if store:
                h[rows(bb, i), re(j)] = r
                h[rows(bb, i), im(j)] = m
            nr.append(r)
            ni.append(m)
        return tuple(nr + ni)

    zero = jnp.zeros((S5_SEGMENTS, LANES), F32)
    ends = loop(lambda i, c: advance(i, c, False), (zero,) * (2 * n_ch))

    starts_r, starts_i = [], []
    for q, (bb, j) in enumerate(chains):
        er, ei = ends[q], ends[n_ch + q]
        pr, pi = pr_ref[0, :, lb(j)], pi_ref[0, :, lb(j)]
        cr, ci = h0r_ref[bb, 0, :, lb(j)], h0i_ref[bb, 0, :, lb(j)]
        start_r, start_i = [cr], [ci]
        for s in range(S5_SEGMENTS - 1):
            cr, ci = (pr * cr - pi * ci + er[s:s + 1, :], pr * ci + pi * cr + ei[s:s + 1, :])
            start_r.append(cr)
            start_i.append(ci)
        starts_r.append(jnp.concatenate(start_r, axis=0))
        starts_i.append(jnp.concatenate(start_i, axis=0))

    last = loop(lambda i, c: advance(i, c, True), tuple(starts_r + starts_i))
    y = d_ref[...] * u
    for q, (bb, j) in enumerate(chains):
        hr_out[bb, 0, :, lb(j)] = last[q][S5_SEGMENTS - 1:, :]
        hi_out[bb, 0, :, lb(j)] = last[n_ch + q][S5_SEGMENTS - 1:, :]
    y += jnp.dot(h[...].astype(BF16), wc_ref[0], preferred_element_type=F32)
    up[...] = y

    def deinterleave(i, c):
        for bb in streams:
            y_ref[bb, strided(i), :] = up[rows(bb, i), :]
        return c

    loop(deinterleave, 0)


def _s5(proj, prm, layer, h0r, h0i):
    n_b, n_l, _ = proj.shape
    nblk = prm["nblk"]
    cin = prm["wb"].shape[1]
    b0 = layer * nblk
    seg = n_l // S5_SEGMENTS
    nb = n_b if seg <= 8 else 1
    assert seg * S5_SEGMENTS == n_l and cin == LANES
    wspec = lambda shape: pl.BlockSpec((1,) + shape, lambda b, s: (b0 + s, 0, 0))
    st_spec = pl.BlockSpec((nb, 1, 1, S5_BLOCK), lambda b, s: (b, s, 0, 0))
    return pl.pallas_call(
        functools.partial(_s5_kernel, seg=seg, nb=nb),
        grid=(n_b // nb, nblk),
        in_specs=[pl.BlockSpec((nb, n_l, cin), lambda b, s: (b, 0, s)),
                  wspec((cin, 2 * S5_BLOCK)),
                  wspec((1, S5_BLOCK)), wspec((1, S5_BLOCK)), wspec((1, S5_BLOCK)), wspec((1, S5_BLOCK)),
                  st_spec, st_spec,
                  wspec((2 * S5_BLOCK, cin)),
                  pl.BlockSpec((1, cin), lambda b, s: (0, b0 + s))],
        out_specs=[pl.BlockSpec((nb, n_l, cin), lambda b, s: (b, 0, s)), st_spec, st_spec],
        out_shape=[jax.ShapeDtypeStruct((n_b, n_l, nblk * cin), F32),
                   jax.ShapeDtypeStruct((n_b, nblk, 1, S5_BLOCK), F32),
                   jax.ShapeDtypeStruct((n_b, nblk, 1, S5_BLOCK), F32)],
        scratch_shapes=[pltpu.VMEM((nb * n_l, 2 * S5_BLOCK), F32), pltpu.VMEM((nb * n_l, LANES), F32)],
        compiler_params=_cparams("parallel", "parallel"),
        name="s5_scan",
    )(proj, prm["wb"], prm["ar"], prm["ai"], *prm["pows"][seg], h0r, h0i, prm["wc"], prm["d"])


def _s5_params(lam_re, lam_im, log_dt, b_re, b_im, c_re, c_im, d_skip, seg_lens):
    depth = lam_re.shape[0]
    merge = lambda t: t.reshape((-1,) + t.shape[2:])
    lam_re, lam_im, log_dt, b_re, b_im, c_re, c_im = map(merge, (lam_re, lam_im, log_dt, b_re, b_im, c_re, c_im))
    n_g, n_p = lam_re.shape
    dt = jnp.exp(log_dt)[:, None]
    e = jnp.exp(lam_re * dt)
    ab_re, ab_im = e * jnp.cos(lam_im * dt), e * jnp.sin(lam_im * dt)
    den = lam_re * lam_re + lam_im * lam_im
    nr, ni = ab_re - 1.0, ab_im
    q_re = (nr * lam_re + ni * lam_im) / den
    q_im = (ni * lam_re - nr * lam_im) / den
    bb_re = q_re[..., None] * b_re - q_im[..., None] * b_im
    bb_im = q_re[..., None] * b_im + q_im[..., None] * b_re
    gpb = S5_BLOCK // n_p
    nblk = n_g // gpb
    eye = jnp.eye(gpb, dtype=F32)

    def pack_b(t):
        t = t.reshape(nblk, gpb, n_p, SSM_GROUP)
        return jnp.einsum("sgpc,gh->sgchp", t, eye).reshape(nblk, gpb * SSM_GROUP, gpb * n_p).astype(BF16)

    def pack_c(t):
        t = t.reshape(nblk, gpb, SSM_GROUP, n_p)
        return jnp.einsum("sgcp,gh->shpgc", t, eye).reshape(nblk, gpb * n_p, gpb * SSM_GROUP).astype(BF16)

    flat = lambda t: t.reshape(nblk, 1, gpb * n_p)
    nlb = gpb * n_p // LANES
    wb = jnp.stack([pack_b(bb_re).reshape(nblk, -1, nlb, LANES), pack_b(bb_im).reshape(nblk, -1, nlb, LANES)], axis=3)
    wc = jnp.stack([pack_c(c_re).reshape(nblk, nlb, LANES, -1), -pack_c(c_im).reshape(nblk, nlb, LANES, -1)], axis=2)
    out = {"wb": wb.reshape(nblk, -1, 2 * gpb * n_p), "wc": wc.reshape(nblk, 2 * gpb * n_p, -1),
           "ar": flat(ab_re), "ai": flat(ab_im), "d": d_skip.reshape(1, -1), "pows": {}, "nblk": nblk // depth}
    for seg in seg_lens:
        assert seg & (seg - 1) == 0
        pr, pi = ab_re, ab_im
        for _ in range(int(math.log2(seg))):
            pr, pi = pr * pr - pi * pi, 2.0 * pr * pi
        out["pows"][seg] = (flat(pr), flat(pi))
    return out


def _att_prompt_kernel(q_ref, k_ref, v_ref, bias_ref, o_ref, kpad, vpad, *, n_l, scale):
    zpad = jnp.zeros((ATT_REACH, LANES), BF16)
    kpad[0:ATT_REACH, :] = zpad
    vpad[0:ATT_REACH, :] = zpad
    kpad[ATT_REACH:, :] = k_ref[0].astype(BF16)
    vpad[ATT_REACH:, :] = v_ref[0].astype(BF16)
    lane = lax.broadcasted_iota(jnp.int32, (1, LANES), 1)
    heads = (lane < HEAD_DIM, lane >= HEAD_DIM)
    col = lax.broadcasted_iota(jnp.int32, (1, K_WINDOW), 1)
    nt = (((1,), (1,)), ((), ()))

    def block(r0, n_masked):
        q = q_ref[0, pl.ds(r0, Q_BLOCK), :] * scale
        kw = kpad[pl.ds(r0, K_WINDOW), :]
        vw = vpad[pl.ds(r0, K_WINDOW), :]
        s = [lax.dot_general(jnp.where(hd, q, 0.0).astype(BF16), kw, nt, preferred_element_type=F32) + bias_ref[0, h]
             for h, hd in enumerate(heads)]
        if n_masked:
            s = [jnp.where(col < n_masked, NEG_INF, x) for x in s]
        m = [jnp.max(x, axis=-1, keepdims=True) for x in s]
        p = [jnp.exp(x - mx) for x, mx in zip(s, m)]
        l = [jnp.sum(x, axis=-1, keepdims=True) for x in p]
        o = [jnp.dot(x.astype(BF16), vw, preferred_element_type=F32) / lx for x, lx in zip(p, l)]
        o_ref[0, pl.ds(r0, Q_BLOCK), :] = jnp.where(heads[0], o[0], o[1]).astype(o_ref.dtype)

    n_blocks = n_l // Q_BLOCK
    n_first = min(n_blocks, ATT_REACH // Q_BLOCK)
    for qb in range(n_first):
        block(qb * Q_BLOCK, ATT_REACH - qb * Q_BLOCK)

    def body(qb, carry):
        block(pl.multiple_of(qb * Q_BLOCK, Q_BLOCK), 0)
        return carry

    lax.fori_loop(n_first, n_blocks, body, 0, unroll=max(1, n_blocks - n_first))


def _att_prompt(proj, bias, layer, q_col, k_col, v_col, n_pairs):
    n_b, n_l, _ = proj.shape
    assert n_l % Q_BLOCK == 0
    p0 = layer * n_pairs
    spec = lambda c0: pl.BlockSpec((1, n_l, LANES), lambda b, p: (b, 0, c0 + p))
    return pl.pallas_call(
        functools.partial(_att_prompt_kernel, n_l=n_l, scale=HEAD_DIM ** -0.5),
        grid=(n_b, n_pairs),
        in_specs=[spec(q_col), spec(k_col), spec(v_col),
                  pl.BlockSpec((1, 2, Q_BLOCK, K_WINDOW), lambda b, p: (p0 + p, 0, 0, 0))],
        out_specs=pl.BlockSpec((1, n_l, LANES), lambda b, p: (b, 0, p)),
        out_shape=jax.ShapeDtypeStruct((n_b, n_l, n_pairs * LANES), BF16),
        scratch_shapes=[pltpu.VMEM((n_l + ATT_REACH, LANES), BF16), pltpu.VMEM((n_l + ATT_REACH, LANES), BF16)],
        compiler_params=_cparams("parallel", "parallel"),
        name="att_prompt",
    )(proj, proj, proj, bias)


def _bias_kernel(g_ref, bp_ref, bs_ref):
    g = g_ref[0]
    n_q, n_k = bp_ref.shape[1], bp_ref.shape[2]
    toep = pltpu.roll(jnp.broadcast_to(g, (n_q, g.shape[1])), 0, 1, stride=1, stride_axis=0)[:, :n_k]
    qc = lax.broadcasted_iota(jnp.int32, (n_q, n_k), 0) // CHUNK
    kc = lax.broadcasted_iota(jnp.int32, (n_q, n_k), 1) // CHUNK
    bp_ref[0] = jnp.where((kc >= qc) & (kc <= qc + LEFT_CHUNKS), toep, NEG_INF)
    n_s, n_ws = bs_ref.shape[1], bs_ref.shape[2]
    bs_ref[0] = pltpu.roll(jnp.broadcast_to(g, (n_s, g.shape[1])), 0, 1, stride=1, stride_axis=0)[:, :n_ws]


def _rel_bias(table, n_s, n_w):
    n_h = table.shape[0]
    assert n_w == ATT_REACH
    width = 1024
    assert width >= K_WINDOW + Q_BLOCK and width % LANES == 0
    d = np.arange(width)
    d = np.where(d < width - Q_BLOCK, d, d - width)
    idx = np.clip(ATT_REACH - d, -REL_CLIP, REL_CLIP) + REL_CLIP
    diag = table[:, idx].astype(F32).reshape(n_h, 1, width)
    return pl.pallas_call(
        _bias_kernel,
        grid=(n_h,),
        in_specs=[pl.BlockSpec((1, 1, width), lambda h: (h, 0, 0))],
        out_specs=[pl.BlockSpec((1, Q_BLOCK, K_WINDOW), lambda h: (h, 0, 0)),
                   pl.BlockSpec((1, n_s, n_w + n_s), lambda h: (h, 0, 0))],
        out_shape=[jax.ShapeDtypeStruct((n_h, Q_BLOCK, K_WINDOW), F32),
                   jax.ShapeDtypeStruct((n_h, n_s, n_w + n_s), F32)],
        compiler_params=_cparams("parallel"),
        name="rel_bias",
    )(diag)


def _att_step_kernel(q_ref, kn_ref, vn_ref, kc_ref, vc_ref, bc_ref, bn_ref, o_ref, *, n_h, scale):
    n_s = q_ref.shape[1]
    d = q_ref.shape[2]
    q = q_ref[0] * scale
    qt = jnp.concatenate([q] * n_h, axis=0)
    rh = lax.broadcasted_iota(jnp.int32, (n_h * n_s, d), 0) // n_s
    ch = lax.broadcasted_iota(jnp.int32, (n_h * n_s, d), 1) // HEAD_DIM
    qbd = jnp.where(rh == ch, qt, 0.0).astype(BF16)
    nt = (((1,), (1,)), ((), ()))
    s1 = lax.dot_general(qbd, kc_ref[0, 0].astype(BF16), nt, preferred_element_type=F32) + bc_ref[...]
    s2 = lax.dot_general(qbd, kn_ref[0].astype(BF16), nt, preferred_element_type=F32) + bn_ref[...]
    m = jnp.maximum(jnp.max(s1, axis=-1, keepdims=True), jnp.max(s2, axis=-1, keepdims=True))
    p1 = jnp.exp(s1 - m)
    p2 = jnp.exp(s2 - m)
    l = jnp.sum(p1, axis=-1, keepdims=True) + jnp.sum(p2, axis=-1, keepdims=True)
    o = (jnp.dot(p1.astype(BF16), vc_ref[0, 0].astype(BF16), preferred_element_type=F32)
         + jnp.dot(p2.astype(BF16), vn_ref[0].astype(BF16), preferred_element_type=F32)) / l
    first = lax.broadcasted_iota(jnp.int32, (1, LANES), 1) < HEAD_DIM
    for j in range(d // LANES):
        blk = o[:, j * LANES:(j + 1) * LANES]
        lo = blk[(2 * j) * n_s:(2 * j + 1) * n_s, :]
        hi = blk[(2 * j + 1) * n_s:(2 * j + 2) * n_s, :]
        o_ref[0, :, j * LANES:(j + 1) * LANES] = jnp.where(first, lo, hi).astype(o_ref.dtype)


def _att_step(proj, k_cache, v_cache, layer, bias_c, bias_n, q_col, k_col, v_col, n_h):
    n_b, n_s, _ = proj.shape
    n_w, d = k_cache.shape[2], k_cache.shape[3]
    spec = lambda c0: pl.BlockSpec((1, n_s, d), lambda b: (b, 0, c0))
    cspec = pl.BlockSpec((1, 1, n_w, d), lambda b: (layer, b, 0, 0))
    bspec = lambda a: pl.BlockSpec((None,) + a.shape[1:], lambda b: (layer, 0, 0))
    return pl.pallas_call(
        functools.partial(_att_step_kernel, n_h=n_h, scale=HEAD_DIM ** -0.5),
        grid=(n_b,),
        in_specs=[spec(q_col), spec(k_col), spec(v_col), cspec, cspec, bspec(bias_c), bspec(bias_n)],
        out_specs=pl.BlockSpec((1, n_s, d), lambda b: (b, 0, 0)),
        out_shape=jax.ShapeDtypeStruct((n_b, n_s, d), BF16),
        compiler_params=_cparams("parallel"),
        name="att_step",
    )(proj, proj, proj, k_cache, v_cache, bias_c, bias_n)


def _rwkv_prep_kernel(r_ref, k_ref, v_ref, u_ref, sh_ref, mu_ref, w0_ref, w1_ref, w2_ref, a0_ref, a1_ref, a2_ref,
                      kk_ref, ka_ref, nkk_out, w_out, b_out, k_out, r_out, v_out, prev):
    j = pl.program_id(1)

    @pl.when(j == 0)
    def _():
        prev[...] = sh_ref[0]

    n_r = r_ref.shape[1]
    row0 = lax.broadcasted_iota(jnp.int32, (SUBLANES, 1), 0) == 0

    def delta(x, i):
        sh = pltpu.roll(x, 1, axis=0)
        top = jnp.where(row0, prev[i:i + 1, :], sh[:SUBLANES])
        sh = jnp.concatenate([top, sh[SUBLANES:]], axis=0) if n_r > SUBLANES else top
        prev[i:i + 1, :] = x[n_r - 1:, :]
        return sh - x

    rp, kp, vp, up = r_ref[0], k_ref[0], v_ref[0], u_ref[0]
    du = delta(up, 3)
    r = rp + delta(rp, 0) * mu_ref[0:1, :]
    k = kp + delta(kp, 1) * mu_ref[1:2, :]
    v = vp + delta(vp, 2) * mu_ref[2:3, :]
    xw = up + du * mu_ref[3:4, :]
    xa = up + du * mu_ref[4:5, :]
    dot = lambda a, b: jnp.dot(a.astype(BF16), b, preferred_element_type=F32)
    z = -(w0_ref[...] + dot(jnp.tanh(dot(xw, w1_ref[...])), w2_ref[...]))
    softplus = jnp.maximum(z, 0.0) + jnp.log(1.0 + jnp.exp(-jnp.abs(z)))
    w = -softplus - 0.5
    a = jax.nn.sigmoid(a0_ref[...] + dot(dot(xa, a1_ref[...]), a2_ref[...]))
    kk = k * kk_ref[...]
    ss = _head_sum(kk * kk, _block_ones(kk.shape[1]))
    kk = kk * lax.rsqrt(jnp.maximum(ss, 1e-24))
    nkk_out[0] = (-kk).astype(nkk_out.dtype)
    w_out[0] = -jnp.exp(w)
    b_out[0] = (kk * a).astype(b_out.dtype)
    k_out[0] = (k * (1.0 + (a - 1.0) * ka_ref[...])).astype(k_out.dtype)
    r_out[0] = r.astype(r_out.dtype)
    v_out[0] = v.astype(v_out.dtype)


def _layer_spec(a, layer):
    return pl.BlockSpec((None,) + a.shape[1:], lambda *_: (layer,) + (0,) * (a.ndim - 1),
                        pipeline_mode=pl.Buffered(1))


def _rwkv_prep(proj, shift0, lw, col0):
    n_b, n_l, _ = proj.shape
    d = shift0.shape[2]
    tr = min(n_l, 512)
    assert n_l % tr == 0
    spec = lambda c: pl.BlockSpec((1, tr, d), lambda b, j: (b, j, c))
    full = lambda a: _layer_spec(a, lw["layer"])
    ws = [lw["rwkv_mu"], lw["rwkv_w0"], lw["rwkv_w1"], lw["rwkv_w2"], lw["rwkv_a0"], lw["rwkv_a1"], lw["rwkv_a2"],
          lw["rwkv_k_k"], lw["rwkv_k_a"]]
    ospec = pl.BlockSpec((1, tr, d), lambda b, j: (b, j, 0))
    return pl.pallas_call(
        _rwkv_prep_kernel,
        grid=(n_b, n_l // tr),
        in_specs=[spec(col0), spec(col0 + 1), spec(col0 + 2), spec(col0 + 3),
                  pl.BlockSpec((1, 4, d), lambda b, j: (b, 0, 0))] + [full(a) for a in ws],
        out_specs=[ospec] * 6,
        out_shape=[jax.ShapeDtypeStruct((n_b, n_l, d), F32 if i == 1 else BF16) for i in range(6)],
        scratch_shapes=[pltpu.VMEM((4, d), F32)],
        compiler_params=_cparams("parallel", "arbitrary"),
        name="rwkv_prep",
    )(proj, proj, proj, proj, shift0, *ws)


def _rwkv_rec_kernel(nkk_ref, lw_ref, b_ref, k_ref, r_ref, v_ref, s0_ref, rk_ref, g_ref, beta_ref,
                     y_ref, s_out, state, *, nbk, nch, c):
    tb = pl.program_id(2)

    @pl.when(tb == 0)
    def _():
        state[...] = s0_ref[:, 0]

    c2 = 2 * c
    lane = lax.broadcasted_iota(jnp.int32, (1, LANES), 1)
    m0 = (lane < HEAD_DIM).astype(F32)
    m1 = 1.0 - m0
    by_head = lambda x: jnp.concatenate([x * m0, x * m1], axis=0)
    mm = lambda a, b: jnp.dot(a.astype(BF16), b.astype(BF16), preferred_element_type=F32)
    mm_nt = lambda a, b: lax.dot_general(a.astype(BF16), b.astype(BF16), (((1,), (1,)), ((), ())),
                                         preferred_element_type=F32)
    ii = lambda shape, d: lax.broadcasted_iota(jnp.int32, shape, d)
    tri = (ii((c, c), 0) >= ii((c, c), 1)).astype(BF16)
    rr, cc = ii((c2, c2), 0), ii((c2, c2), 1)
    strict = (rr // c == cc // c) & (cc % c < rr % c)
    incl = ii((c, c2), 1) % c <= ii((c, c2), 0)
    eye_t = (rr == cc).astype(F32)
    eye_s = (ii((LANES, LANES), 0) == ii((LANES, LANES), 1)).astype(F32)
    probs = [(bb, ci) for ci in range(nch) for bb in range(nbk)]
    rows = lambda ref, q: ref[q[0], q[1] * c:(q[1] + 1) * c, :].astype(F32)
    each = lambda fn, *lists: [fn(*xs) for xs in zip(*lists)]

    lw = [rows(lw_ref, q) for q in probs]

    def cumsum(x):
        hi = x.astype(BF16)
        lo = (x - hi.astype(F32)).astype(BF16)
        return jnp.dot(tri, hi, preferred_element_type=F32) + jnp.dot(tri, lo, preferred_element_type=F32)

    cum = each(cumsum, lw)
    tot = [x[c - 1:c, :] for x in cum]
    g_c = [jnp.exp(x) for x in tot]
    a_t = each(lambda q, x, l: rows(nkk_ref, q) * jnp.exp(x - l), probs, cum, lw)
    r_t = each(lambda q, x: rows(r_ref, q) * jnp.exp(x), probs, cum)
    ginv = [jnp.exp(-x) for x in cum]
    ghat = each(lambda x, t: jnp.exp(t - x), cum, tot)
    kx = [rows(k_ref, q) for q in probs]
    bx = [rows(b_ref, q) for q in probs]
    a_bd = [by_head(x) for x in a_t]
    lhs = each(lambda a, r: jnp.concatenate([a, r], axis=0), a_bd, r_t)
    if c2 % LANES == 0:
        sc = each(lambda l, k, b, gi: mm_nt(l, jnp.concatenate([by_head(k * gi), by_head(b * gi)], axis=0)),
                  lhs, kx, bx, ginv)
        sc_k, sc_b = [x[:, :c2] for x in sc], [x[:, c2:] for x in sc]
    else:
        sc_k = each(lambda l, k, gi: mm_nt(l, by_head(k * gi)), lhs, kx, ginv)
        sc_b = each(lambda l, b, gi: mm_nt(l, by_head(b * gi)), lhs, bx, ginv)
    l_ak = [jnp.where(strict, x[:c2], 0.0) for x in sc_k]
    l_ab = [jnp.where(strict, x[:c2], 0.0) for x in sc_b]
    m_rk = [jnp.where(incl, x[c2:], 0.0) for x in sc_k]
    m_rb = [jnp.where(incl, x[c2:], 0.0) for x in sc_b]
    col_t = ii((1, c2), 1)
    t0, t1 = (col_t < c).astype(F32), (col_t >= c).astype(F32)
    blockdiag = lambda x: jnp.concatenate([x * t0, x * t1], axis=0)
    pw = [x[:c] + x[c:] for x in l_ab]
    inv = [x + eye_t[:c] + eye_t[c:] for x in pw]
    for _ in range(int(math.log2(c)) - 1):
        pw = each(lambda x: mm(x, blockdiag(x)), pw)
        inv = each(lambda p, x: p + mm(p, blockdiag(x)), inv, pw)
    inv = [blockdiag(x) for x in inv]
    v_bd = [by_head(rows(v_ref, q)) for q in probs]
    x_in = each(lambda a, l, v: jnp.concatenate([a, mm(l, v)], axis=1), a_bd, l_ak, v_bd)
    uu = each(mm, inv, x_in)
    bh_t = each(lambda b, gh: by_head(b * gh).T, bx, ghat)
    kh_t = each(lambda k, gh: by_head(k * gh).T, kx, ghat)
    bu = each(mm, bh_t, uu)
    g_mat = each(lambda gc, x: eye_s * gc + x[:, :LANES], g_c, bu)
    h_mat = each(lambda kt, v, x: mm(kt, v) + x[:, LANES:], kh_t, v_bd, bu)
    mu = each(mm, m_rb, uu)
    y_a = each(lambda r, x: r + x[:, :LANES], r_t, mu)
    y_b = each(lambda mk, v, x: mm(mk, v) + x[:, LANES:], m_rk, v_bd, mu)

    s_cur = [state[bb] for bb in range(nbk)]
    ys = {}
    for i, (bb, ci) in enumerate(probs):
        both = mm(jnp.concatenate([y_a[i], g_mat[i]], axis=0), s_cur[bb])
        ys[(bb, ci)] = both[:c] + y_b[i]
        s_cur[bb] = both[c:] + h_mat[i]
    for bb in range(nbk):
        state[bb] = s_cur[bb]

    @pl.when(tb == pl.num_programs(2) - 1)
    def _():
        s_out[:, 0] = state[...]

    ones = _block_ones(LANES)
    inv_n = 1.0 / HEAD_DIM
    head_sum = lambda x: jnp.dot(x.astype(BF16), ones, preferred_element_type=F32)
    for bb in range(nbk):
        y = jnp.concatenate([ys[(bb, ci)] for ci in range(nch)], axis=0) if nch > 1 else ys[(bb, 0)]
        mean = head_sum(y) * inv_n
        yc = y - mean
        var = head_sum(yc * yc) * inv_n
        yn = yc * lax.rsqrt(var + GN_EPS) * g_ref[...] + beta_ref[...]
        f32 = lambda ref: ref[bb].astype(F32)
        bonus = head_sum(f32(r_ref) * f32(k_ref) * rk_ref[...]) * f32(v_ref)
        y_ref[bb] = (yn + bonus).astype(y_ref.dtype)


def _rwkv_rec(prep, s0, layer, r_k, lnx_g, lnx_b):
    n_b, n_l, d = prep[0].shape
    n_pairs = d // LANES
    c = min(n_l, RWKV_CHUNK)
    nch = min(n_l // c, RWKV_GROUP // 2)
    nbk = min(n_b, RWKV_GROUP // nch)
    lb = nch * c
    assert n_l % lb == 0 and n_b % nbk == 0 and c & (c - 1) == 0
    spec = pl.BlockSpec((nbk, lb, LANES), lambda g, p, t: (g, t, p))
    sspec = pl.BlockSpec((nbk, 1, LANES, LANES), lambda g, p, t: (g, p, 0, 0))
    vec = pl.BlockSpec((None, 1, LANES), lambda g, p, t: (layer, 0, p))
    return pl.pallas_call(
        functools.partial(_rwkv_rec_kernel, nbk=nbk, nch=nch, c=c),
        grid=(n_b // nbk, n_pairs, n_l // lb),
        in_specs=[spec] * 6 + [sspec, vec, vec, vec],
        out_specs=[spec, sspec],
        out_shape=[jax.ShapeDtypeStruct((n_b, n_l, d), BF16),
                   jax.ShapeDtypeStruct((n_b, n_pairs, LANES, LANES), F32)],
        scratch_shapes=[pltpu.VMEM((nbk, LANES, LANES), F32)],
        compiler_params=_cparams("parallel", "parallel", "arbitrary"),
        name="rwkv_rec",
    )(*prep, s0, r_k, lnx_g, lnx_b)


def _pack_pairs(s):
    n_b, n_h, n, _ = s.shape
    eye = jnp.eye(2, dtype=s.dtype)
    st = jnp.swapaxes(s.reshape(n_b, n_h // 2, 2, n, n), -1, -2)
    return (st[:, :, :, :, None, :] * eye[:, None, :, None]).reshape(n_b, n_h // 2, 2 * n, 2 * n)


def _unpack_pairs(s):
    n_b, n_p, n2, _ = s.shape
    n = n2 // 2
    eye = jnp.eye(2, dtype=s.dtype)
    diag = jnp.sum(s.reshape(n_b, n_p, 2, n, 2, n) * eye[:, None, :, None], axis=4)
    return jnp.swapaxes(diag, -1, -2).reshape(n_b, 2 * n_p, n, n)


def _mix_kernel(x_ref, ys_ref, gs_ref, ya_ref, ga_ref, yc_ref, gc_ref, wg_ref, bg_ref, wo_ref, lg_ref, lb_ref,
                y_ref, *maybe_yb_ref, alpha, d_ssm, d_att):
    silu = lambda g: g * jax.nn.sigmoid(g)
    dot = lambda a, b: jnp.dot(a.astype(BF16), b, preferred_element_type=F32)
    n_r = x_ref.shape[0]
    n_part = 2 if n_r % 32 == 0 else 1
    for part in range(n_part):
        rows = slice(part * n_r // n_part, (part + 1) * n_r // n_part)
        ys = ys_ref[rows, :]
        z = 0.5 * ys * (1.0 + jnp.tanh(math.sqrt(2.0 / math.pi) * (ys + 0.044715 * (ys * ys * ys))))
        m_s = z * jax.nn.sigmoid(dot(z, wg_ref[...]) + bg_ref[...]) * silu(gs_ref[rows, :])
        m_a = ya_ref[rows, :] * silu(ga_ref[rows, :])
        m_c = yc_ref[rows, :] * silu(gc_ref[rows, :])
        out = (dot(m_s, wo_ref[0:d_ssm, :]) + dot(m_a, wo_ref[d_ssm:d_ssm + d_att, :])
               + dot(m_c, wo_ref[d_ssm + d_att:, :]))
        h = alpha * x_ref[rows, :] + out
        mu = jnp.mean(h, axis=-1, keepdims=True)
        hc = h - mu
        var = jnp.mean(hc * hc, axis=-1, keepdims=True)
        y = hc * lax.rsqrt(var + LN_EPS) * lg_ref[...] + lb_ref[...]
        y_ref[rows, :] = y
        for yb_ref in maybe_yb_ref:
            yb_ref[rows, :] = y.astype(BF16)


def _mix(x, proj, ys, ya, yc, lw, alpha, gs_col, ga_col, gc_col, want_bf16):
    m, d = x.shape
    d_ssm, d_att, d_rw = ys.shape[1], ya.shape[1], yc.shape[1]
    tm = min(m, 512)
    assert m % tm == 0
    row = lambda w, c: pl.BlockSpec((tm, w), lambda i: (i, c))
    full = lambda a: _layer_spec(a, lw["layer"])
    ws = [lw["w_glu"], lw["b_glu"], lw["w_out"], lw["ln_g"], lw["ln_b"]]
    return pl.pallas_call(
        functools.partial(_mix_kernel, alpha=alpha, d_ssm=d_ssm, d_att=d_att),
        grid=(m // tm,),
        in_specs=[row(d, 0), row(d_ssm, 0), row(d_ssm, gs_col), row(d_att, 0), row(d_att, ga_col),
                  row(d_rw, 0), row(d_rw, gc_col)] + [full(a) for a in ws],
        out_specs=[row(d, 0)] * (2 if want_bf16 else 1),
        out_shape=[jax.ShapeDtypeStruct((m, d), F32)] + [jax.ShapeDtypeStruct((m, d), BF16)] * want_bf16,
        compiler_params=_cparams("parallel"),
        name="mix_out",
    )(x, ys, proj, ya, proj, yc, proj, *ws)


def _layer(x, proj, st, lw, s5p, alpha):
    n_b, n_l, d_model = x.shape
    d_ssm = lw["w_glu"].shape[1]
    n_h = lw["n_att_heads"]
    d_att = n_h * HEAD_DIM
    d_rw = lw["rwkv_w0"].shape[2]
    n_rh = d_rw // HEAD_DIM
    n_blk = d_ssm // LANES
    proj = proj.reshape(n_b, n_l, -1)
    q0 = 2 * d_ssm
    r0 = q0 + 4 * d_att
    assert q0 % d_att == 0 and r0 % d_rw == 0 and d_ssm == d_rw

    if st is None:
        zeros = jnp.zeros((n_b, n_blk, 1, S5_BLOCK), F32)
        h0r = h0i = zeros
        wkv0 = jnp.zeros((n_b, n_rh // 2, LANES, LANES), F32)
        shift0 = jnp.zeros((n_b, 4, d_rw), F32)
    else:
        k_cache, v_cache, h0r, h0i, wkv0, shift0 = st
        h0r = h0r.reshape(n_b, n_blk, 1, S5_BLOCK)
        h0i = h0i.reshape(n_b, n_blk, 1, S5_BLOCK)
        wkv0 = _pack_pairs(wkv0)
        shift0 = shift0.reshape(n_b, 4, d_rw)

    ys, h_re, h_im = _s5(proj, s5p, lw["layer"], h0r, h0i)

    n_keep = min(ATT_REACH, n_l)
    k_rows = proj[:, n_l - n_keep:, q0 + d_att:q0 + 2 * d_att].reshape(n_b, n_keep, n_h, HEAD_DIM)
    v_rows = proj[:, n_l - n_keep:, q0 + 2 * d_att:q0 + 3 * d_att].reshape(n_b, n_keep, n_h, HEAD_DIM)
    if st is None:
        pl0 = q0 // LANES
        ya = _att_prompt(proj, lw["bias_prompt"], lw["layer"], pl0, pl0 + d_att // LANES, pl0 + 2 * d_att // LANES,
                         n_h // 2)
    else:
        c0 = q0 // d_att
        ya = _att_step(proj, k_cache, v_cache, lw["layer"], lw["bias_cache"], lw["bias_new"], c0, c0 + 1, c0 + 2, n_h)

    prep = _rwkv_prep(proj, shift0, lw, r0 // d_rw)
    yc, wkv = _rwkv_rec(prep, wkv0, lw["layer"], lw["rwkv_r_k"], lw["rwkv_lnx_g"], lw["rwkv_lnx_b"])
    shift = proj[:, n_l - 1, r0:r0 + 4 * d_rw]

    m = n_b * n_l
    outs = _mix(x.reshape(m, d_model), proj.reshape(m, -1), ys.reshape(m, d_ssm), ya.reshape(m, d_att),
                yc.reshape(m, d_rw), lw, alpha, 1, (q0 + 3 * d_att) // d_att, (r0 + 4 * d_rw) // d_rw,
                want_bf16=not lw["last"])
    n_g = d_ssm // SSM_GROUP
    states = (k_rows, v_rows, h_re.reshape(n_b, n_g, SSM_STATE), h_im.reshape(n_b, n_g, SSM_STATE),
              _unpack_pairs(wkv), shift)
    y, yb = outs if len(outs) == 2 else (outs[0], None)
    return y.reshape(n_b, n_l, d_model), yb, states


def kernel(x_prompt, x_sample, cache_att_k, cache_att_v, state_ssm_re, state_ssm_im, state_rwkv, state_rwkv_shift, w_in, ssm_lam_re, ssm_lam_im, ssm_log_dt, ssm_b_re, ssm_b_im, ssm_c_re, ssm_c_im, ssm_d, ssm_w_glu, ssm_b_glu, att_rel_bias, rwkv_mu, rwkv_w0, rwkv_w1, rwkv_w2, rwkv_a0, rwkv_a1, rwkv_a2, rwkv_k_k, rwkv_k_a, rwkv_r_k, rwkv_lnx_g, rwkv_lnx_b, w_out, ln_g, ln_b):
    depth = w_in.shape[0]
    alpha = (2.0 * depth) ** 0.25
    y_p, y_s = x_prompt, x_sample
    yb_p = yb_s = None
    seg_lens = sorted({x_prompt.shape[1] // S5_SEGMENTS, x_sample.shape[1] // S5_SEGMENTS})
    p_st, s_st = [], []
    n_d, n_sb, n_w = cache_att_k.shape[:3]
    n_s = x_sample.shape[1]
    n_h = att_rel_bias.shape[1]
    k_cache = cache_att_k.reshape(n_d, n_sb, n_w, -1)
    v_cache = cache_att_v.reshape(n_d, n_sb, n_w, -1)
    row = lambda a: a.reshape(depth, 1, -1)
    bias_p, bias_s = _rel_bias(att_rel_bias.reshape(depth * n_h, -1), n_s, n_w)
    bias_s = bias_s.reshape(depth, n_h * n_s, n_w + n_s)
    params = {"w_in": w_in, "n_att_heads": n_h, "w_glu": ssm_w_glu.astype(BF16), "b_glu": row(ssm_b_glu),
              "rwkv_mu": rwkv_mu, "rwkv_w0": row(rwkv_w0), "rwkv_w1": rwkv_w1.astype(BF16),
              "rwkv_w2": rwkv_w2.astype(BF16), "rwkv_a0": row(rwkv_a0), "rwkv_a1": rwkv_a1.astype(BF16),
              "rwkv_a2": rwkv_a2.astype(BF16), "rwkv_k_k": row(rwkv_k_k), "rwkv_k_a": row(rwkv_k_a),
              "rwkv_r_k": row(rwkv_r_k), "rwkv_lnx_g": row(rwkv_lnx_g), "rwkv_lnx_b": row(rwkv_lnx_b),
              "w_out": w_out.astype(BF16), "ln_g": row(ln_g), "ln_b": row(ln_b),
              "bias_prompt": bias_p.reshape(depth * n_h // 2, 2, Q_BLOCK, K_WINDOW),
              "bias_cache": bias_s[:, :, :n_w], "bias_new": bias_s[:, :, n_w:]}
    s5p = _s5_params(ssm_lam_re, ssm_lam_im, ssm_log_dt, ssm_b_re, ssm_b_im, ssm_c_re, ssm_c_im, ssm_d, seg_lens)
    for l in range(depth):
        lw = dict(params, layer=l, last=(l == depth - 1))
        if yb_p is None:
            flat = lambda a: a.reshape(-1, a.shape[-1])
            yb_p, yb_s = _to_bf16(flat(y_p)), _to_bf16(flat(y_s))
        proj_p, proj_s = _in_proj(yb_p, yb_s, w_in, l)
        y_p, yb_p, st_p = _layer(y_p, proj_p, None, lw, s5p, alpha)
        y_s, yb_s, st_s = _layer(y_s, proj_s, (k_cache, v_cache, state_ssm_re[l], state_ssm_im[l],
                                             state_rwkv[l], state_rwkv_shift[l]), lw, s5p, alpha)
        p_st.append(st_p)
        s_st.append(st_s)
    stacked = lambda states, i: jnp.stack([st[i] for st in states], axis=0)
    return (y_p, y_s) + tuple(stacked(p_st, i) for i in range(6)) + tuple(stacked(s_st, i) for i in range(6))
```

```python
import functools
import math

import jax
import jax.numpy as jnp
import numpy as np
from jax import lax
from jax.experimental import pallas as pl
from jax.experimental.pallas import tpu as pltpu

F32 = jnp.float32
BF16 = jnp.bfloat16

LANES = 128
SUBLANES = 8
VMEM_LIMIT = 56 * 1024 * 1024

CHUNK = 64
LEFT_CHUNKS = 8
ATT_REACH = LEFT_CHUNKS * CHUNK
REL_CLIP = 128
HEAD_DIM = 64
SSM_GROUP = 16
SSM_STATE = 64
RWKV_LORA = 64
NEG_INF = -1e30
GN_EPS = 64e-5
LN_EPS = 1e-5

RWKV_CHUNK = 64
RWKV_GROUP = 16
Q_BLOCK = 4 * CHUNK
K_WINDOW = Q_BLOCK + ATT_REACH
S5_SEGMENTS = SUBLANES
S5_BLOCK = 512


def _cparams(*sem):
    return pltpu.CompilerParams(dimension_semantics=sem, vmem_limit_bytes=VMEM_LIMIT)


def _block_ones(n):
    r = lax.broadcasted_iota(jnp.int32, (n, n), 0) // HEAD_DIM
    c = lax.broadcasted_iota(jnp.int32, (n, n), 1) // HEAD_DIM
    return (r == c).astype(BF16)


def _head_sum(x, ones):
    hi = x.astype(BF16)
    lo = (x - hi.astype(F32)).astype(BF16)
    return jnp.dot(hi, ones, preferred_element_type=F32) + jnp.dot(lo, ones, preferred_element_type=F32)


def _matmul_kernel(x_ref, xs_ref, w_ref, o_ref, os_ref, wb):
    @pl.when(pl.program_id(1) == 0)
    def _():
        wb[...] = w_ref[0].astype(BF16)

    o_ref[...] = jnp.dot(x_ref[...], wb[...], preferred_element_type=F32)

    @pl.when(pl.program_id(1) == pl.num_programs(1) - 1)
    def _():
        os_ref[...] = jnp.dot(xs_ref[...], wb[...], preferred_element_type=F32)


def _cast_kernel(x_ref, o_ref):
    o_ref[...] = x_ref[...].astype(o_ref.dtype)


def _to_bf16(x):
    m, k = x.shape
    tm = min(m, 1024)
    assert m % tm == 0
    return pl.pallas_call(
        _cast_kernel,
        grid=(m // tm,),
        in_specs=[pl.BlockSpec((tm, k), lambda i: (i, 0))],
        out_specs=pl.BlockSpec((tm, k), lambda i: (i, 0)),
        out_shape=jax.ShapeDtypeStruct((m, k), BF16),
        compiler_params=_cparams("parallel"),
        name="to_bf16",
    )(x)


def _in_proj(x, xs, w, layer):
    m, k = x.shape
    ms = xs.shape[0]
    n = w.shape[2]
    tm = min(m, 512)
    tn = 1536
    assert m % tm == 0 and n % tn == 0 and x.dtype == BF16 and xs.dtype == BF16
    return pl.pallas_call(
        _matmul_kernel,
        grid=(n // tn, m // tm),
        in_specs=[pl.BlockSpec((tm, k), lambda j, i: (i, 0)),
                  pl.BlockSpec((ms, k), lambda j, i: (0, 0)),
                  pl.BlockSpec((1, k, tn), lambda j, i: (layer, 0, j))],
        out_specs=[pl.BlockSpec((tm, tn), lambda j, i: (i, j)),
                   pl.BlockSpec((ms, tn), lambda j, i: (0, j))],
        out_shape=[jax.ShapeDtypeStruct((m, n), F32), jax.ShapeDtypeStruct((ms, n), F32)],
        scratch_shapes=[pltpu.VMEM((k, tn), BF16)],
        compiler_params=_cparams("parallel", "arbitrary"),
        name="in_proj",
    )(x, xs, w)


def _s5_kernel(u_ref, wb_ref, ar_ref, ai_ref, pr_ref, pi_ref, h0r_ref, h0i_ref,
               wc_ref, d_ref, y_ref, hr_out, hi_out, h, up, *, seg, nb):
    nlb = h.shape[1] // (2 * LANES)
    n_l = seg * S5_SEGMENTS
    lb = lambda j: slice(j * LANES, (j + 1) * LANES)
    re = lambda j: slice(2 * j * LANES, (2 * j + 1) * LANES)
    im = lambda j: slice((2 * j + 1) * LANES, (2 * j + 2) * LANES)
    aligned = lambda r: r if isinstance(r, int) else pl.multiple_of(r, S5_SEGMENTS)
    rows = lambda bb, i: pl.ds(aligned(bb * n_l + i * S5_SEGMENTS), S5_SEGMENTS)
    strided = lambda i: pl.ds(i, S5_SEGMENTS, stride=seg)
    streams = range(nb)

    def loop(body, init):
        if seg <= 8:
            for i in range(seg):
                init = body(i, init)
            return init
        return lax.fori_loop(0, seg, body, init, unroll=8)

    def interleave(i, c):
        for bb in streams:
            up[rows(bb, i), :] = u_ref[bb, strided(i), :]
        return c

    loop(interleave, 0)
    u = up[...]
    ub = u.astype(BF16)
    h[...] = jnp.dot(ub, wb_ref[0], preferred_element_type=F32)
    bcast = lambda ref, j: jnp.broadcast_to(ref[0, :, lb(j)], (S5_SEGMENTS, LANES))
    ar = [bcast(ar_ref, j) for j in range(nlb)]
    ai = [bcast(ai_ref, j) for j in range(nlb)]
    chains = [(bb, j) for bb in streams for j in range(nlb)]
    n_ch = len(chains)

    def advance(i, c, store):
        nr, ni = [], []
        for q, (bb, j) in enumerate(chains):
            hr, hi = c[q], c[n_ch + q]
            r = ar[j] * hr - ai[j] * hi + h[rows(bb, i), re(j)]
            m = ar[j] * hi + ai[j] * hr + h[rows(bb, i), im(j)]
            if store:
                h[rows(bb, i), re(j)] = r
                h[rows(bb, i), im(j)] = m
            nr.append(r)
            ni.append(m)
        return tuple(nr + ni)

    zero = jnp.zeros((S5_SEGMENTS, LANES), F32)
    ends = loop(lambda i, c: advance(i, c, False), (zero,) * (2 * n_ch))

    starts_r, starts_i = [], []
    for q, (bb, j) in enumerate(chains):
        er, ei = ends[q], ends[n_ch + q]
        pr, pi = pr_ref[0, :, lb(j)], pi_ref[0, :, lb(j)]
        cr, ci = h0r_ref[bb, 0, :, lb(j)], h0i_ref[bb, 0, :, lb(j)]
        start_r, start_i = [cr], [ci]
        for s in range(S5_SEGMENTS - 1):
            cr, ci = (pr * cr - pi * ci + er[s:s + 1, :], pr * ci + pi * cr + ei[s:s + 1, :])
            start_r.append(cr)
            start_i.append(ci)
        starts_r.append(jnp.concatenate(start_r, axis=0))
        starts_i.append(jnp.concatenate(start_i, axis=0))

    last = loop(lambda i, c: advance(i, c, True), tuple(starts_r + starts_i))
    y = d_ref[...] * u
    for q, (bb, j) in enumerate(chains):
        hr_out[bb, 0, :, lb(j)] = last[q][S5_SEGMENTS - 1:, :]
        hi_out[bb, 0, :, lb(j)] = last[n_ch + q][S5_SEGMENTS - 1:, :]
    y += jnp.dot(h[...].astype(BF16), wc_ref[0], preferred_element_type=F32)
    up[...] = y

    def deinterleave(i, c):
        for bb in streams:
            y_ref[bb, strided(i), :] = up[rows(bb, i), :]
        return c

    loop(deinterleave, 0)


def _s5(proj, prm, layer, h0r, h0i):
    n_b, n_l, _ = proj.shape
    nblk = prm["nblk"]
    cin = prm["wb"].shape[1]
    b0 = layer * nblk
    seg = n_l // S5_SEGMENTS
    nb = n_b if seg <= 8 else 1
    assert seg * S5_SEGMENTS == n_l and cin == LANES
    wspec = lambda shape: pl.BlockSpec((1,) + shape, lambda b, s: (b0 + s, 0, 0))
    st_spec = pl.BlockSpec((nb, 1, 1, S5_BLOCK), lambda b, s: (b, s, 0, 0))
    return pl.pallas_call(
        functools.partial(_s5_kernel, seg=seg, nb=nb),
        grid=(n_b // nb, nblk),
        in_specs=[pl.BlockSpec((nb, n_l, cin), lambda b, s: (b, 0, s)),
                  wspec((cin, 2 * S5_BLOCK)),
                  wspec((1, S5_BLOCK)), wspec((1, S5_BLOCK)), wspec((1, S5_BLOCK)), wspec((1, S5_BLOCK)),
                  st_spec, st_spec,
                  wspec((2 * S5_BLOCK, cin)),
                  pl.BlockSpec((1, cin), lambda b, s: (0, b0 + s))],
        out_specs=[pl.BlockSpec((nb, n_l, cin), lambda b, s: (b, 0, s)), st_spec, st_spec],
        out_shape=[jax.ShapeDtypeStruct((n_b, n_l, nblk * cin), F32),
                   jax.ShapeDtypeStruct((n_b, nblk, 1, S5_BLOCK), F32),
                   jax.ShapeDtypeStruct((n_b, nblk, 1, S5_BLOCK), F32)],
        scratch_shapes=[pltpu.VMEM((nb * n_l, 2 * S5_BLOCK), F32), pltpu.VMEM((nb * n_l, LANES), F32)],
        compiler_params=_cparams("parallel", "parallel"),
        name="s5_scan",
    )(proj, prm["wb"], prm["ar"], prm["ai"], *prm["pows"][seg], h0r, h0i, prm["wc"], prm["d"])


def _s5_params(lam_re, lam_im, log_dt, b_re, b_im, c_re, c_im, d_skip, seg_lens):
    depth = lam_re.shape[0]
    merge = lambda t: t.reshape((-1,) + t.shape[2:])
    lam_re, lam_im, log_dt, b_re, b_im, c_re, c_im = map(merge, (lam_re, lam_im, log_dt, b_re, b_im, c_re, c_im))
    n_g, n_p = lam_re.shape
    dt = jnp.exp(log_dt)[:, None]
    e = jnp.exp(lam_re * dt)
    ab_re, ab_im = e * jnp.cos(lam_im * dt), e * jnp.sin(lam_im * dt)
    den = lam_re * lam_re + lam_im * lam_im
    nr, ni = ab_re - 1.0, ab_im
    q_re = (nr * lam_re + ni * lam_im) / den
    q_im = (ni * lam_re - nr * lam_im) / den
    bb_re = q_re[..., None] * b_re - q_im[..., None] * b_im
    bb_im = q_re[..., None] * b_im + q_im[..., None] * b_re
    gpb = S5_BLOCK // n_p
    nblk = n_g // gpb
    eye = jnp.eye(gpb, dtype=F32)

    def pack_b(t):
        t = t.reshape(nblk, gpb, n_p, SSM_GROUP)
        return jnp.einsum("sgpc,gh->sgchp", t, eye).reshape(nblk, gpb * SSM_GROUP, gpb * n_p).astype(BF16)

    def pack_c(t):
        t = t.reshape(nblk, gpb, SSM_GROUP, n_p)
        return jnp.einsum("sgcp,gh->shpgc", t, eye).reshape(nblk, gpb * n_p, gpb * SSM_GROUP).astype(BF16)

    flat = lambda t: t.reshape(nblk, 1, gpb * n_p)
    nlb = gpb * n_p // LANES
    wb = jnp.stack([pack_b(bb_re).reshape(nblk, -1, nlb, LANES), pack_b(bb_im).reshape(nblk, -1, nlb, LANES)], axis=3)
    wc = jnp.stack([pack_c(c_re).reshape(nblk, nlb, LANES, -1), -pack_c(c_im).reshape(nblk, nlb, LANES, -1)], axis=2)
    out = {"wb": wb.reshape(nblk, -1, 2 * gpb * n_p), "wc": wc.reshape(nblk, 2 * gpb * n_p, -1),
           "ar": flat(ab_re), "ai": flat(ab_im), "d": d_skip.reshape(1, -1), "pows": {}, "nblk": nblk // depth}
    for seg in seg_lens:
        assert seg & (seg - 1) == 0
        pr, pi = ab_re, ab_im
        for _ in range(int(math.log2(seg))):
            pr, pi = pr * pr - pi * pi, 2.0 * pr * pi
        out["pows"][seg] = (flat(pr), flat(pi))
    return out


def _att_prompt_kernel(q_ref, k_ref, v_ref, bias_ref, o_ref, kpad, vpad, *, n_l, scale):
    zpad = jnp.zeros((ATT_REACH, LANES), BF16)
    kpad[0:ATT_REACH, :] = zpad
    vpad[0:ATT_REACH, :] = zpad
    kpad[ATT_REACH:, :] = k_ref[0].astype(BF16)
    vpad[ATT_REACH:, :] = v_ref[0].astype(BF16)
    lane = lax.broadcasted_iota(jnp.int32, (1, LANES), 1)
    heads = (lane < HEAD_DIM, lane >= HEAD_DIM)
    col = lax.broadcasted_iota(jnp.int32, (1, K_WINDOW), 1)
    nt = (((1,), (1,)), ((), ()))

    def block(r0, n_masked):
        q = q_ref[0, pl.ds(r0, Q_BLOCK), :] * scale
        kw = kpad[pl.ds(r0, K_WINDOW), :]
        vw = vpad[pl.ds(r0, K_WINDOW), :]
        s = [lax.dot_general(jnp.where(hd, q, 0.0).astype(BF16), kw, nt, preferred_element_type=F32) + bias_ref[0, h]
             for h, hd in enumerate(heads)]
        if n_masked:
            s = [jnp.where(col < n_masked, NEG_INF, x) for x in s]
        m = [jnp.max(x, axis=-1, keepdims=True) for x in s]
        p = [jnp.exp(x - mx) for x, mx in zip(s, m)]
        l = [jnp.sum(x, axis=-1, keepdims=True) for x in p]
        o = [jnp.dot(x.astype(BF16), vw, preferred_element_type=F32) / lx for x, lx in zip(p, l)]
        o_ref[0, pl.ds(r0, Q_BLOCK), :] = jnp.where(heads[0], o[0], o[1]).astype(o_ref.dtype)

    n_blocks = n_l // Q_BLOCK
    n_first = min(n_blocks, ATT_REACH // Q_BLOCK)
    for qb in range(n_first):
        block(qb * Q_BLOCK, ATT_REACH - qb * Q_BLOCK)

    def body(qb, carry):
        block(pl.multiple_of(qb * Q_BLOCK, Q_BLOCK), 0)
        return carry

    lax.fori_loop(n_first, n_blocks, body, 0, unroll=max(1, n_blocks - n_first))


def _att_prompt(proj, bias, layer, q_col, k_col, v_col, n_pairs):
    n_b, n_l, _ = proj.shape
    assert n_l % Q_BLOCK == 0
    p0 = layer * n_pairs
    spec = lambda c0: pl.BlockSpec((1, n_l, LANES), lambda b, p: (b, 0, c0 + p))
    return pl.pallas_call(
        functools.partial(_att_prompt_kernel, n_l=n_l, scale=HEAD_DIM ** -0.5),
        grid=(n_b, n_pairs),
        in_specs=[spec(q_col), spec(k_col), spec(v_col),
                  pl.BlockSpec((1, 2, Q_BLOCK, K_WINDOW), lambda b, p: (p0 + p, 0, 0, 0))],
        out_specs=pl.BlockSpec((1, n_l, LANES), lambda b, p: (b, 0, p)),
        out_shape=jax.ShapeDtypeStruct((n_b, n_l, n_pairs * LANES), BF16),
        scratch_shapes=[pltpu.VMEM((n_l + ATT_REACH, LANES), BF16), pltpu.VMEM((n_l + ATT_REACH, LANES), BF16)],
        compiler_params=_cparams("parallel", "parallel"),
        name="att_prompt",
    )(proj, proj, proj, bias)


def _bias_kernel(g_ref, bp_ref, bs_ref):
    g = g_ref[0]
    n_q, n_k = bp_ref.shape[1], bp_ref.shape[2]
    toep = pltpu.roll(jnp.broadcast_to(g, (n_q, g.shape[1])), 0, 1, stride=1, stride_axis=0)[:, :n_k]
    qc = lax.broadcasted_iota(jnp.int32, (n_q, n_k), 0) // CHUNK
    kc = lax.broadcasted_iota(jnp.int32, (n_q, n_k), 1) // CHUNK
    bp_ref[0] = jnp.where((kc >= qc) & (kc <= qc + LEFT_CHUNKS), toep, NEG_INF)
    n_s, n_ws = bs_ref.shape[1], bs_ref.shape[2]
    bs_ref[0] = pltpu.roll(jnp.broadcast_to(g, (n_s, g.shape[1])), 0, 1, stride=1, stride_axis=0)[:, :n_ws]


def _rel_bias(table, n_s, n_w):
    n_h = table.shape[0]
    assert n_w == ATT_REACH
    width = 1024
    assert width >= K_WINDOW + Q_BLOCK and width % LANES == 0
    d = np.arange(width)
    d = np.where(d < width - Q_BLOCK, d, d - width)
    idx = np.clip(ATT_REACH - d, -REL_CLIP, REL_CLIP) + REL_CLIP
    diag = table[:, idx].astype(F32).reshape(n_h, 1, width)
    return pl.pallas_call(
        _bias_kernel,
        grid=(n_h,),
        in_specs=[pl.BlockSpec((1, 1, width), lambda h: (h, 0, 0))],
        out_specs=[pl.BlockSpec((1, Q_BLOCK, K_WINDOW), lambda h: (h, 0, 0)),
                   pl.BlockSpec((1, n_s, n_w + n_s), lambda h: (h, 0, 0))],
        out_shape=[jax.ShapeDtypeStruct((n_h, Q_BLOCK, K_WINDOW), F32),
                   jax.ShapeDtypeStruct((n_h, n_s, n_w + n_s), F32)],
        compiler_params=_cparams("parallel"),
        name="rel_bias",
    )(diag)


def _att_step_kernel(q_ref, kn_ref, vn_ref, kc_ref, vc_ref, bc_ref, bn_ref, o_ref, *, n_h, scale):
    n_s = q_ref.shape[1]
    d = q_ref.shape[2]
    q = q_ref[0] * scale
    qt = jnp.concatenate([q] * n_h, axis=0)
    rh = lax.broadcasted_iota(jnp.int32, (n_h * n_s, d), 0) // n_s
    ch = lax.broadcasted_iota(jnp.int32, (n_h * n_s, d), 1) // HEAD_DIM
    qbd = jnp.where(rh == ch, qt, 0.0).astype(BF16)
    nt = (((1,), (1,)), ((), ()))
    s1 = lax.dot_general(qbd, kc_ref[0, 0].astype(BF16), nt, preferred_element_type=F32) + bc_ref[...]
    s2 = lax.dot_general(qbd, kn_ref[0].astype(BF16), nt, preferred_element_type=F32) + bn_ref[...]
    m = jnp.maximum(jnp.max(s1, axis=-1, keepdims=True), jnp.max(s2, axis=-1, keepdims=True))
    p1 = jnp.exp(s1 - m)
    p2 = jnp.exp(s2 - m)
    l = jnp.sum(p1, axis=-1, keepdims=True) + jnp.sum(p2, axis=-1, keepdims=True)
    o = (jnp.dot(p1.astype(BF16), vc_ref[0, 0].astype(BF16), preferred_element_type=F32)
         + jnp.dot(p2.astype(BF16), vn_ref[0].astype(BF16), preferred_element_type=F32)) / l
    first = lax.broadcasted_iota(jnp.int32, (1, LANES), 1) < HEAD_DIM
    for j in range(d // LANES):
        blk = o[:, j * LANES:(j + 1) * LANES]
        lo = blk[(2 * j) * n_s:(2 * j + 1) * n_s, :]
        hi = blk[(2 * j + 1) * n_s:(2 * j + 2) * n_s, :]
        o_ref[0, :, j * LANES:(j + 1) * LANES] = jnp.where(first, lo, hi).astype(o_ref.dtype)


def _att_step(proj, k_cache, v_cache, layer, bias_c, bias_n, q_col, k_col, v_col, n_h):
    n_b, n_s, _ = proj.shape
    n_w, d = k_cache.shape[2], k_cache.shape[3]
    spec = lambda c0: pl.BlockSpec((1, n_s, d), lambda b: (b, 0, c0))
    cspec = pl.BlockSpec((1, 1, n_w, d), lambda b: (layer, b, 0, 0))
    bspec = lambda a: pl.BlockSpec((None,) + a.shape[1:], lambda b: (layer, 0, 0))
    return pl.pallas_call(
        functools.partial(_att_step_kernel, n_h=n_h, scale=HEAD_DIM ** -0.5),
        grid=(n_b,),
        in_specs=[spec(q_col), spec(k_col), spec(v_col), cspec, cspec, bspec(bias_c), bspec(bias_n)],
        out_specs=pl.BlockSpec((1, n_s, d), lambda b: (b, 0, 0)),
        out_shape=jax.ShapeDtypeStruct((n_b, n_s, d), BF16),
        compiler_params=_cparams("parallel"),
        name="att_step",
    )(proj, proj, proj, k_cache, v_cache, bias_c, bias_n)


def _rwkv_prep_kernel(r_ref, k_ref, v_ref, u_ref, sh_ref, mu_ref, w0_ref, w1_ref, w2_ref, a0_ref, a1_ref, a2_ref,
                      kk_ref, ka_ref, nkk_out, w_out, b_out, k_out, r_out, v_out, prev):
    j = pl.program_id(1)

    @pl.when(j == 0)
    def _():
        prev[...] = sh_ref[0]

    n_r = r_ref.shape[1]
    row0 = lax.broadcasted_iota(jnp.int32, (SUBLANES, 1), 0) == 0

    def delta(x, i):
        sh = pltpu.roll(x, 1, axis=0)
        top = jnp.where(row0, prev[i:i + 1, :], sh[:SUBLANES])
        sh = jnp.concatenate([top, sh[SUBLANES:]], axis=0) if n_r > SUBLANES else top
        prev[i:i + 1, :] = x[n_r - 1:, :]
        return sh - x

    rp, kp, vp, up = r_ref[0], k_ref[0], v_ref[0], u_ref[0]
    du = delta(up, 3)
    r = rp + delta(rp, 0) * mu_ref[0:1, :]
    k = kp + delta(kp, 1) * mu_ref[1:2, :]
    v = vp + delta(vp, 2) * mu_ref[2:3, :]
    xw = up + du * mu_ref[3:4, :]
    xa = up + du * mu_ref[4:5, :]
    dot = lambda a, b: jnp.dot(a.astype(BF16), b, preferred_element_type=F32)
    z = -(w0_ref[...] + dot(jnp.tanh(dot(xw, w1_ref[...])), w2_ref[...]))
    softplus = jnp.maximum(z, 0.0) + jnp.log(1.0 + jnp.exp(-jnp.abs(z)))
    w = -softplus - 0.5
    a = jax.nn.sigmoid(a0_ref[...] + dot(dot(xa, a1_ref[...]), a2_ref[...]))
    kk = k * kk_ref[...]
    ss = _head_sum(kk * kk, _block_ones(kk.shape[1]))
    kk = kk * lax.rsqrt(jnp.maximum(ss, 1e-24))
    nkk_out[0] = (-kk).astype(nkk_out.dtype)
    w_out[0] = -jnp.exp(w)
    b_out[0] = (kk * a).astype(b_out.dtype)
    k_out[0] = (k * (1.0 + (a - 1.0) * ka_ref[...])).astype(k_out.dtype)
    r_out[0] = r.astype(r_out.dtype)
    v_out[0] = v.astype(v_out.dtype)


def _layer_spec(a, layer):
    return pl.BlockSpec((None,) + a.shape[1:], lambda *_: (layer,) + (0,) * (a.ndim - 1),
                        pipeline_mode=pl.Buffered(1))


def _rwkv_prep(proj, shift0, lw, col0):
    n_b, n_l, _ = proj.shape
    d = shift0.shape[2]
    tr = min(n_l, 512)
    assert n_l % tr == 0
    spec = lambda c: pl.BlockSpec((1, tr, d), lambda b, j: (b, j, c))
    full = lambda a: _layer_spec(a, lw["layer"])
    ws = [lw["rwkv_mu"], lw["rwkv_w0"], lw["rwkv_w1"], lw["rwkv_w2"], lw["rwkv_a0"], lw["rwkv_a1"], lw["rwkv_a2"],
          lw["rwkv_k_k"], lw["rwkv_k_a"]]
    ospec = pl.BlockSpec((1, tr, d), lambda b, j: (b, j, 0))
    return pl.pallas_call(
        _rwkv_prep_kernel,
        grid=(n_b, n_l // tr),
        in_specs=[spec(col0), spec(col0 + 1), spec(col0 + 2), spec(col0 + 3),
                  pl.BlockSpec((1, 4, d), lambda b, j: (b, 0, 0))] + [full(a) for a in ws],
        out_specs=[ospec] * 6,
        out_shape=[jax.ShapeDtypeStruct((n_b, n_l, d), F32 if i == 1 else BF16) for i in range(6)],
        scratch_shapes=[pltpu.VMEM((4, d), F32)],
        compiler_params=_cparams("parallel", "arbitrary"),
        name="rwkv_prep",
    )(proj, proj, proj, proj, shift0, *ws)


def _rwkv_rec_kernel(nkk_ref, lw_ref, b_ref, k_ref, r_ref, v_ref, s0_ref, rk_ref, g_ref, beta_ref,
                     y_ref, s_out, state, *, nbk, nch, c):
    tb = pl.program_id(2)

    @pl.when(tb == 0)
    def _():
        state[...] = s0_ref[:, 0]

    c2 = 2 * c
    lane = lax.broadcasted_iota(jnp.int32, (1, LANES), 1)
    m0 = (lane < HEAD_DIM).astype(F32)
    m1 = 1.0 - m0
    by_head = lambda x: jnp.concatenate([x * m0, x * m1], axis=0)
    mm = lambda a, b: jnp.dot(a.astype(BF16), b.astype(BF16), preferred_element_type=F32)
    mm_nt = lambda a, b: lax.dot_general(a.astype(BF16), b.astype(BF16), (((1,), (1,)), ((), ())),
                                         preferred_element_type=F32)
    ii = lambda shape, d: lax.broadcasted_iota(jnp.int32, shape, d)
    tri = (ii((c, c), 0) >= ii((c, c), 1)).astype(BF16)
    rr, cc = ii((c2, c2), 0), ii((c2, c2), 1)
    strict = (rr // c == cc // c) & (cc % c < rr % c)
    incl = ii((c, c2), 1) % c <= ii((c, c2), 0)
    eye_t = (rr == cc).astype(F32)
    eye_s = (ii((LANES, LANES), 0) == ii((LANES, LANES), 1)).astype(F32)
    probs = [(bb, ci) for ci in range(nch) for bb in range(nbk)]
    rows = lambda ref, q: ref[q[0], q[1] * c:(q[1] + 1) * c, :].astype(F32)
    each = lambda fn, *lists: [fn(*xs) for xs in zip(*lists)]

    lw = [rows(lw_ref, q) for q in probs]

    def cumsum(x):
        hi = x.astype(BF16)
        lo = (x - hi.astype(F32)).astype(BF16)
        return jnp.dot(tri, hi, preferred_element_type=F32) + jnp.dot(tri, lo, preferred_element_type=F32)

    cum = each(cumsum, lw)
    tot = [x[c - 1:c, :] for x in cum]
    g_c = [jnp.exp(x) for x in tot]
    a_t = each(lambda q, x, l: rows(nkk_ref, q) * jnp.exp(x - l), probs, cum, lw)
    r_t = each(lambda q, x: rows(r_ref, q) * jnp.exp(x), probs, cum)
    ginv = [jnp.exp(-x) for x in cum]
    ghat = each(lambda x, t: jnp.exp(t - x), cum, tot)
    kx = [rows(k_ref, q) for q in probs]
    bx = [rows(b_ref, q) for q in probs]
    a_bd = [by_head(x) for x in a_t]
    lhs = each(lambda a, r: jnp.concatenate([a, r], axis=0), a_bd, r_t)
    if c2 % LANES == 0:
        sc = each(lambda l, k, b, gi: mm_nt(l, jnp.concatenate([by_head(k * gi), by_head(b * gi)], axis=0)),
                  lhs, kx, bx, ginv)
        sc_k, sc_b = [x[:, :c2] for x in sc], [x[:, c2:] for x in sc]
    else:
        sc_k = each(lambda l, k, gi: mm_nt(l, by_head(k * gi)), lhs, kx, ginv)
        sc_b = each(lambda l, b, gi: mm_nt(l, by_head(b * gi)), lhs, bx, ginv)
    l_ak = [jnp.where(strict, x[:c2], 0.0) for x in sc_k]
    l_ab = [jnp.where(strict, x[:c2], 0.0) for x in sc_b]
    m_rk = [jnp.where(incl, x[c2:], 0.0) for x in sc_k]
    m_rb = [jnp.where(incl, x[c2:], 0.0) for x in sc_b]
    col_t = ii((1, c2), 1)
    t0, t1 = (col_t < c).astype(F32), (col_t >= c).astype(F32)
    blockdiag = lambda x: jnp.concatenate([x * t0, x * t1], axis=0)
    pw = [x[:c] + x[c:] for x in l_ab]
    inv = [x + eye_t[:c] + eye_t[c:] for x in pw]
    for _ in range(int(math.log2(c)) - 1):
        pw = each(lambda x: mm(x, blockdiag(x)), pw)
        inv = each(lambda p, x: p + mm(p, blockdiag(x)), inv, pw)
    inv = [blockdiag(x) for x in inv]
    v_bd = [by_head(rows(v_ref, q)) for q in probs]
    x_in = each(lambda a, l, v: jnp.concatenate([a, mm(l, v)], axis=1), a_bd, l_ak, v_bd)
    uu = each(mm, inv, x_in)
    bh_t = each(lambda b, gh: by_head(b * gh).T, bx, ghat)
    kh_t = each(lambda k, gh: by_head(k * gh).T, kx, ghat)
    bu = each(mm, bh_t, uu)
    g_mat = each(lambda gc, x: eye_s * gc + x[:, :LANES], g_c, bu)
    h_mat = each(lambda kt, v, x: mm(kt, v) + x[:, LANES:], kh_t, v_bd, bu)
    mu = each(mm, m_rb, uu)
    y_a = each(lambda r, x: r + x[:, :LANES], r_t, mu)
    y_b = each(lambda mk, v, x: mm(mk, v) + x[:, LANES:], m_rk, v_bd, mu)

    s_cur = [state[bb] for bb in range(nbk)]
    ys = {}
    for i, (bb, ci) in enumerate(probs):
        both = mm(jnp.concatenate([y_a[i], g_mat[i]], axis=0), s_cur[bb])
        ys[(bb, ci)] = both[:c] + y_b[i]
        s_cur[bb] = both[c:] + h_mat[i]
    for bb in range(nbk):
        state[bb] = s_cur[bb]

    @pl.when(tb == pl.num_programs(2) - 1)
    def _():
        s_out[:, 0] = state[...]

    ones = _block_ones(LANES)
    inv_n = 1.0 / HEAD_DIM
    head_sum = lambda x: jnp.dot(x.astype(BF16), ones, preferred_element_type=F32)
    for bb in range(nbk):
        y = jnp.concatenate([ys[(bb, ci)] for ci in range(nch)], axis=0) if nch > 1 else ys[(bb, 0)]
        mean = head_sum(y) * inv_n
        yc = y - mean
        var = head_sum(yc * yc) * inv_n
        yn = yc * lax.rsqrt(var + GN_EPS) * g_ref[...] + beta_ref[...]
        f32 = lambda ref: ref[bb].astype(F32)
        bonus = head_sum(f32(r_ref) * f32(k_ref) * rk_ref[...]) * f32(v_ref)
        y_ref[bb] = (yn + bonus).astype(y_ref.dtype)


def _rwkv_rec(prep, s0, layer, r_k, lnx_g, lnx_b):
    n_b, n_l, d = prep[0].shape
    n_pairs = d // LANES
    c = min(n_l, RWKV_CHUNK)
    nch = min(n_l // c, RWKV_GROUP // 2)
    nbk = min(n_b, RWKV_GROUP // nch)
    lb = nch * c
    assert n_l % lb == 0 and n_b % nbk == 0 and c & (c - 1) == 0
    spec = pl.BlockSpec((nbk, lb, LANES), lambda g, p, t: (g, t, p))
    sspec = pl.BlockSpec((nbk, 1, LANES, LANES), lambda g, p, t: (g, p, 0, 0))
    vec = pl.BlockSpec((None, 1, LANES), lambda g, p, t: (layer, 0, p))
    return pl.pallas_call(
        functools.partial(_rwkv_rec_kernel, nbk=nbk, nch=nch, c=c),
        grid=(n_b // nbk, n_pairs, n_l // lb),
        in_specs=[spec] * 6 + [sspec, vec, vec, vec],
        out_specs=[spec, sspec],
        out_shape=[jax.ShapeDtypeStruct((n_b, n_l, d), BF16),
                   jax.ShapeDtypeStruct((n_b, n_pairs, LANES, LANES), F32)],
        scratch_shapes=[pltpu.VMEM((nbk, LANES, LANES), F32)],
        compiler_params=_cparams("parallel", "parallel", "arbitrary"),
        name="rwkv_rec",
    )(*prep, s0, r_k, lnx_g, lnx_b)


def _pack_pairs(s):
    n_b, n_h, n, _ = s.shape
    st = jnp.swapaxes(s, -1, -2).reshape(n_b, n_h // 2, 2, n, n)
    z = jnp.zeros_like(st[:, :, 0])
    return jnp.concatenate([jnp.concatenate([st[:, :, 0], z], axis=-1),
                            jnp.concatenate([z, st[:, :, 1]], axis=-1)], axis=-2)


def _unpack_pairs(s):
    n_b, n_p, n2, _ = s.shape
    n = n2 // 2
    st = jnp.stack([s[:, :, :n, :n], s[:, :, n:, n:]], axis=2)
    return jnp.swapaxes(st, -1, -2).reshape(n_b, 2 * n_p, n, n)


def _mix_kernel(x_ref, ys_ref, gs_ref, ya_ref, ga_ref, yc_ref, gc_ref, wg_ref, bg_ref, wo_ref, lg_ref, lb_ref,
                y_ref, *maybe_yb_ref, alpha, d_ssm, d_att):
    silu = lambda g: g * jax.nn.sigmoid(g)
    dot = lambda a, b: jnp.dot(a.astype(BF16), b, preferred_element_type=F32)
    n_r = x_ref.shape[0]
    n_part = 2 if n_r % 32 == 0 else 1
    for part in range(n_part):
        rows = slice(part * n_r // n_part, (part + 1) * n_r // n_part)
        ys = ys_ref[rows, :]
        z = 0.5 * ys * (1.0 + jnp.tanh(math.sqrt(2.0 / math.pi) * (ys + 0.044715 * (ys * ys * ys))))
        m_s = z * jax.nn.sigmoid(dot(z, wg_ref[...]) + bg_ref[...]) * silu(gs_ref[rows, :])
        m_a = ya_ref[rows, :] * silu(ga_ref[rows, :])
        m_c = yc_ref[rows, :] * silu(gc_ref[rows, :])
        out = (dot(m_s, wo_ref[0:d_ssm, :]) + dot(m_a, wo_ref[d_ssm:d_ssm + d_att, :])
               + dot(m_c, wo_ref[d_ssm + d_att:, :]))
        h = alpha * x_ref[rows, :] + out
        mu = jnp.mean(h, axis=-1, keepdims=True)
        hc = h - mu
        var = jnp.mean(hc * hc, axis=-1, keepdims=True)
        y = hc * lax.rsqrt(var + LN_EPS) * lg_ref[...] + lb_ref[...]
        y_ref[rows, :] = y
        for yb_ref in maybe_yb_ref:
            yb_ref[rows, :] = y.astype(BF16)


def _mix(x, proj, ys, ya, yc, lw, alpha, gs_col, ga_col, gc_col, want_bf16):
    m, d = x.shape
    d_ssm, d_att, d_rw = ys.shape[1], ya.shape[1], yc.shape[1]
    tm = min(m, 512)
    assert m % tm == 0
    row = lambda w, c: pl.BlockSpec((tm, w), lambda i: (i, c))
    full = lambda a: _layer_spec(a, lw["layer"])
    ws = [lw["w_glu"], lw["b_glu"], lw["w_out"], lw["ln_g"], lw["ln_b"]]
    return pl.pallas_call(
        functools.partial(_mix_kernel, alpha=alpha, d_ssm=d_ssm, d_att=d_att),
        grid=(m // tm,),
        in_specs=[row(d, 0), row(d_ssm, 0), row(d_ssm, gs_col), row(d_att, 0), row(d_att, ga_col),
                  row(d_rw, 0), row(d_rw, gc_col)] + [full(a) for a in ws],
        out_specs=[row(d, 0)] * (2 if want_bf16 else 1),
        out_shape=[jax.ShapeDtypeStruct((m, d), F32)] + [jax.ShapeDtypeStruct((m, d), BF16)] * want_bf16,
        compiler_params=_cparams("parallel"),
        name="mix_out",
    )(x, ys, proj, ya, proj, yc, proj, *ws)


def _layer(x, proj, st, lw, s5p, alpha):
    n_b, n_l, d_model = x.shape
    d_ssm = lw["w_glu"].shape[1]
    n_h = lw["n_att_heads"]
    d_att = n_h * HEAD_DIM
    d_rw = lw["rwkv_w0"].shape[2]
    n_rh = d_rw // HEAD_DIM
    n_blk = d_ssm // LANES
    proj = proj.reshape(n_b, n_l, -1)
    q0 = 2 * d_ssm
    r0 = q0 + 4 * d_att
    assert q0 % d_att == 0 and r0 % d_rw == 0 and d_ssm == d_rw

    if st is None:
        zeros = jnp.zeros((n_b, n_blk, 1, S5_BLOCK), F32)
        h0r = h0i = zeros
        wkv0 = jnp.zeros((n_b, n_rh // 2, LANES, LANES), F32)
        shift0 = jnp.zeros((n_b, 4, d_rw), F32)
    else:
        k_cache, v_cache, h0r, h0i, wkv0, shift0 = st
        h0r = h0r.reshape(n_b, n_blk, 1, S5_BLOCK)
        h0i = h0i.reshape(n_b, n_blk, 1, S5_BLOCK)
        wkv0 = _pack_pairs(wkv0)
        shift0 = shift0.reshape(n_b, 4, d_rw)

    ys, h_re, h_im = _s5(proj, s5p, lw["layer"], h0r, h0i)

    n_keep = min(ATT_REACH, n_l)
    k_rows = proj[:, n_l - n_keep:, q0 + d_att:q0 + 2 * d_att].reshape(n_b, n_keep, n_h, HEAD_DIM)
    v_rows = proj[:, n_l - n_keep:, q0 + 2 * d_att:q0 + 3 * d_att].reshape(n_b, n_keep, n_h, HEAD_DIM)
    if st is None:
        pl0 = q0 // LANES
        ya = _att_prompt(proj, lw["bias_prompt"], lw["layer"], pl0, pl0 + d_att // LANES, pl0 + 2 * d_att // LANES,
                         n_h // 2)
    else:
        c0 = q0 // d_att
        ya = _att_step(proj, k_cache, v_cache, lw["layer"], lw["bias_cache"], lw["bias_new"], c0, c0 + 1, c0 + 2, n_h)

    prep = _rwkv_prep(proj, shift0, lw, r0 // d_rw)
    yc, wkv = _rwkv_rec(prep, wkv0, lw["layer"], lw["rwkv_r_k"], lw["rwkv_lnx_g"], lw["rwkv_lnx_b"])
    shift = proj[:, n_l - 1, r0:r0 + 4 * d_rw]

    m = n_b * n_l
    outs = _mix(x.reshape(m, d_model), proj.reshape(m, -1), ys.reshape(m, d_ssm), ya.reshape(m, d_att),
                yc.reshape(m, d_rw), lw, alpha, 1, (q0 + 3 * d_att) // d_att, (r0 + 4 * d_rw) // d_rw,
                want_bf16=not lw["last"])
    n_g = d_ssm // SSM_GROUP
    states = (k_rows, v_rows, h_re.reshape(n_b, n_g, SSM_STATE), h_im.reshape(n_b, n_g, SSM_STATE),
              _unpack_pairs(wkv), shift)
    y, yb = outs if len(outs) == 2 else (outs[0], None)
    return y.reshape(n_b, n_l, d_model), yb, states


def kernel(x_prompt, x_sample, cache_att_k, cache_att_v, state_ssm_re, state_ssm_im, state_rwkv, state_rwkv_shift, w_in, ssm_lam_re, ssm_lam_im, ssm_log_dt, ssm_b_re, ssm_b_im, ssm_c_re, ssm_c_im, ssm_d, ssm_w_glu, ssm_b_glu, att_rel_bias, rwkv_mu, rwkv_w0, rwkv_w1, rwkv_w2, rwkv_a0, rwkv_a1, rwkv_a2, rwkv_k_k, rwkv_k_a, rwkv_r_k, rwkv_lnx_g, rwkv_lnx_b, w_out, ln_g, ln_b):
    depth = w_in.shape[0]
    alpha = (2.0 * depth) ** 0.25
    y_p, y_s = x_prompt, x_sample
    yb_p = yb_s = None
    seg_lens = sorted({x_prompt.shape[1] // S5_SEGMENTS, x_sample.shape[1] // S5_SEGMENTS})
    p_st, s_st = [], []
    n_d, n_sb, n_w = cache_att_k.shape[:3]
    n_s = x_sample.shape[1]
    n_h = att_rel_bias.shape[1]
    k_cache = cache_att_k.reshape(n_d, n_sb, n_w, -1)
    v_cache = cache_att_v.reshape(n_d, n_sb, n_w, -1)
    row = lambda a: a.reshape(depth, 1, -1)
    bias_p, bias_s = _rel_bias(att_rel_bias.reshape(depth * n_h, -1), n_s, n_w)
    bias_s = bias_s.reshape(depth, n_h * n_s, n_w + n_s)
    params = {"w_in": w_in, "n_att_heads": n_h, "w_glu": ssm_w_glu.astype(BF16), "b_glu": row(ssm_b_glu),
              "rwkv_mu": rwkv_mu, "rwkv_w0": row(rwkv_w0), "rwkv_w1": rwkv_w1.astype(BF16),
              "rwkv_w2": rwkv_w2.astype(BF16), "rwkv_a0": row(rwkv_a0), "rwkv_a1": rwkv_a1.astype(BF16),
              "rwkv_a2": rwkv_a2.astype(BF16), "rwkv_k_k": row(rwkv_k_k), "rwkv_k_a": row(rwkv_k_a),
              "rwkv_r_k": row(rwkv_r_k), "rwkv_lnx_g": row(rwkv_lnx_g), "rwkv_lnx_b": row(rwkv_lnx_b),
              "w_out": w_out.astype(BF16), "ln_g": row(ln_g), "ln_b": row(ln_b),
              "bias_prompt": bias_p.reshape(depth * n_h // 2, 2, Q_BLOCK, K_WINDOW),
              "bias_cache": bias_s[:, :, :n_w], "bias_new": bias_s[:, :, n_w:]}
    s5p = _s5_params(ssm_lam_re, ssm_lam_im, ssm_log_dt, ssm_b_re, ssm_b_im, ssm_c_re, ssm_c_im, ssm_d, seg_lens)
    for l in range(depth):
        lw = dict(params, layer=l, last=(l == depth - 1))
        if yb_p is None:
            flat = lambda a: a.reshape(-1, a.shape[-1])
            yb_p, yb_s = _to_bf16(flat(y_p)), _to_bf16(flat(y_s))
        proj_p, proj_s = _in_proj(yb_p, yb_s, w_in, l)
        y_p, yb_p, st_p = _layer(y_p, proj_p, None, lw, s5p, alpha)
        y_s, yb_s, st_s = _layer(y_s, proj_s, (k_cache, v_cache, state_ssm_re[l], state_ssm_im[l],
                                             state_rwkv[l], state_rwkv_shift[l]), lw, s5p, alpha)
        p_st.append(st_p)
        s_st.append(st_s)
    stacked = lambda states, i: jnp.stack([st[i] for st in states], axis=0)
    return (y_p, y_s) + tuple(stacked(p_st, i) for i in range(6)) + tuple(stacked(s_st, i) for i in range(6))
```

```python
import functools
import math

import jax
import jax.numpy as jnp
import numpy as np
from jax import lax
from jax.experimental import pallas as pl
from jax.experimental.pallas import tpu as pltpu

F32 = jnp.float32
BF16 = jnp.bfloat16

LANES = 128
SUBLANES = 8
VMEM_LIMIT = 56 * 1024 * 1024

CHUNK = 64
LEFT_CHUNKS = 8
ATT_REACH = LEFT_CHUNKS * CHUNK
REL_CLIP = 128
HEAD_DIM = 64
SSM_GROUP = 16
SSM_STATE = 64
RWKV_LORA = 64
NEG_INF = -1e30
GN_EPS = 64e-5
LN_EPS = 1e-5

RWKV_CHUNK = 64
RWKV_GROUP = 16
Q_BLOCK = 4 * CHUNK
K_WINDOW = Q_BLOCK + ATT_REACH
S5_SEGMENTS = SUBLANES
S5_BLOCK = 512


def _cparams(*sem):
    return pltpu.CompilerParams(dimension_semantics=sem, vmem_limit_bytes=VMEM_LIMIT)


def _block_ones(n):
    r = lax.broadcasted_iota(jnp.int32, (n, n), 0) // HEAD_DIM
    c = lax.broadcasted_iota(jnp.int32, (n, n), 1) // HEAD_DIM
    return (r == c).astype(BF16)


def _head_sum(x, ones):
    hi = x.astype(BF16)
    lo = (x - hi.astype(F32)).astype(BF16)
    return jnp.dot(hi, ones, preferred_element_type=F32) + jnp.dot(lo, ones, preferred_element_type=F32)


def _matmul_kernel(x_ref, xs_ref, w_ref, o_ref, os_ref, wb):
    @pl.when(pl.program_id(1) == 0)
    def _():
        wb[...] = w_ref[0].astype(BF16)

    o_ref[...] = jnp.dot(x_ref[...], wb[...], preferred_element_type=F32).astype(o_ref.dtype)

    @pl.when(pl.program_id(1) == pl.num_programs(1) - 1)
    def _():
        os_ref[...] = jnp.dot(xs_ref[...], wb[...], preferred_element_type=F32).astype(os_ref.dtype)


def _cast_kernel(x_ref, o_ref):
    o_ref[...] = x_ref[...].astype(o_ref.dtype)


def _to_bf16(x):
    m, k = x.shape
    tm = min(m, 1024)
    assert m % tm == 0
    return pl.pallas_call(
        _cast_kernel,
        grid=(m // tm,),
        in_specs=[pl.BlockSpec((tm, k), lambda i: (i, 0))],
        out_specs=pl.BlockSpec((tm, k), lambda i: (i, 0)),
        out_shape=jax.ShapeDtypeStruct((m, k), BF16),
        compiler_params=_cparams("parallel"),
        name="to_bf16",
    )(x)


def _in_proj(x, xs, w, layer):
    m, k = x.shape
    ms = xs.shape[0]
    n = w.shape[2]
    tm = min(m, 1024)
    tn = 1536
    assert m % tm == 0 and n % tn == 0 and x.dtype == BF16 and xs.dtype == BF16
    return pl.pallas_call(
        _matmul_kernel,
        grid=(n // tn, m // tm),
        in_specs=[pl.BlockSpec((tm, k), lambda j, i: (i, 0)),
                  pl.BlockSpec((ms, k), lambda j, i: (0, 0)),
                  pl.BlockSpec((1, k, tn), lambda j, i: (layer, 0, j))],
        out_specs=[pl.BlockSpec((tm, tn), lambda j, i: (i, j)),
                   pl.BlockSpec((ms, tn), lambda j, i: (0, j))],
        out_shape=[jax.ShapeDtypeStruct((m, n), BF16), jax.ShapeDtypeStruct((ms, n), BF16)],
        scratch_shapes=[pltpu.VMEM((k, tn), BF16)],
        compiler_params=_cparams("parallel", "arbitrary"),
        name="in_proj",
    )(x, xs, w)


def _s5_kernel(u_ref, wb_ref, ar_ref, ai_ref, pr_ref, pi_ref, h0r_ref, h0i_ref,
               wc_ref, d_ref, y_ref, hr_out, hi_out, h, up, uf, *, seg, nb):
    nlb = h.shape[1] // (2 * LANES)
    n_l = seg * S5_SEGMENTS
    lb = lambda j: slice(j * LANES, (j + 1) * LANES)
    re = lambda j: slice(2 * j * LANES, (2 * j + 1) * LANES)
    im = lambda j: slice((2 * j + 1) * LANES, (2 * j + 2) * LANES)
    aligned = lambda r: r if isinstance(r, int) else pl.multiple_of(r, S5_SEGMENTS)
    rows = lambda bb, i: pl.ds(aligned(bb * n_l + i * S5_SEGMENTS), S5_SEGMENTS)
    strided = lambda i: pl.ds(i, S5_SEGMENTS, stride=seg)
    streams = range(nb)

    def loop(body, init):
        if seg <= 8:
            for i in range(seg):
                init = body(i, init)
            return init
        return lax.fori_loop(0, seg, body, init, unroll=8)

    uf[...] = u_ref[...].astype(F32)

    def interleave(i, c):
        for bb in streams:
            up[rows(bb, i), :] = uf[bb, strided(i), :]
        return c

    loop(interleave, 0)
    u = up[...]
    ub = u.astype(BF16)
    h[...] = jnp.dot(ub, wb_ref[0], preferred_element_type=F32)
    bcast = lambda ref, j: jnp.broadcast_to(ref[0, :, lb(j)], (S5_SEGMENTS, LANES))
    ar = [bcast(ar_ref, j) for j in range(nlb)]
    ai = [bcast(ai_ref, j) for j in range(nlb)]
    chains = [(bb, j) for bb in streams for j in range(nlb)]
    n_ch = len(chains)

    def advance(i, c, store):
        nr, ni = [], []
        for q, (bb, j) in enumerate(chains):
            hr, hi = c[q], c[n_ch + q]
            r = ar[j] * hr - ai[j] * hi + h[rows(bb, i), re(j)]
            m = ar[j] * hi + ai[j] * hr + h[rows(bb, i), im(j)]
            if store:
                h[rows(bb, i), re(j)] = r
                h[rows(bb, i), im(j)] = m
            nr.append(r)
            ni.append(m)
        return tuple(nr + ni)

    zero = jnp.zeros((S5_SEGMENTS, LANES), F32)
    ends = loop(lambda i, c: advance(i, c, False), (zero,) * (2 * n_ch))

    starts_r, starts_i = [], []
    for q, (bb, j) in enumerate(chains):
        er, ei = ends[q], ends[n_ch + q]
        pr, pi = pr_ref[0, :, lb(j)], pi_ref[0, :, lb(j)]
        cr, ci = h0r_ref[bb, 0, :, lb(j)], h0i_ref[bb, 0, :, lb(j)]
        start_r, start_i = [cr], [ci]
        for s in range(S5_SEGMENTS - 1):
            cr, ci = (pr * cr - pi * ci + er[s:s + 1, :], pr * ci + pi * cr + ei[s:s + 1, :])
            start_r.append(cr)
            start_i.append(ci)
        starts_r.append(jnp.concatenate(start_r, axis=0))
        starts_i.append(jnp.concatenate(start_i, axis=0))

    last = loop(lambda i, c: advance(i, c, True), tuple(starts_r + starts_i))
    y = d_ref[...] * u
    for q, (bb, j) in enumerate(chains):
        hr_out[bb, 0, :, lb(j)] = last[q][S5_SEGMENTS - 1:, :]
        hi_out[bb, 0, :, lb(j)] = last[n_ch + q][S5_SEGMENTS - 1:, :]
    y += jnp.dot(h[...].astype(BF16), wc_ref[0], preferred_element_type=F32)
    up[...] = y

    def deinterleave(i, c):
        for bb in streams:
            y_ref[bb, strided(i), :] = up[rows(bb, i), :]
        return c

    loop(deinterleave, 0)


def _s5(proj, prm, layer, h0r, h0i):
    n_b, n_l, _ = proj.shape
    nblk = prm["nblk"]
    cin = prm["wb"].shape[1]
    b0 = layer * nblk
    seg = n_l // S5_SEGMENTS
    nb = n_b if seg <= 8 else 1
    assert seg * S5_SEGMENTS == n_l and cin == LANES
    wspec = lambda shape: pl.BlockSpec((1,) + shape, lambda b, s: (b0 + s, 0, 0))
    st_spec = pl.BlockSpec((nb, 1, 1, S5_BLOCK), lambda b, s: (b, s, 0, 0))
    return pl.pallas_call(
        functools.partial(_s5_kernel, seg=seg, nb=nb),
        grid=(n_b // nb, nblk),
        in_specs=[pl.BlockSpec((nb, n_l, cin), lambda b, s: (b, 0, s)),
                  wspec((cin, 2 * S5_BLOCK)),
                  wspec((1, S5_BLOCK)), wspec((1, S5_BLOCK)), wspec((1, S5_BLOCK)), wspec((1, S5_BLOCK)),
                  st_spec, st_spec,
                  wspec((2 * S5_BLOCK, cin)),
                  pl.BlockSpec((1, cin), lambda b, s: (0, b0 + s))],
        out_specs=[pl.BlockSpec((nb, n_l, cin), lambda b, s: (b, 0, s)), st_spec, st_spec],
        out_shape=[jax.ShapeDtypeStruct((n_b, n_l, nblk * cin), F32),
                   jax.ShapeDtypeStruct((n_b, nblk, 1, S5_BLOCK), F32),
                   jax.ShapeDtypeStruct((n_b, nblk, 1, S5_BLOCK), F32)],
        scratch_shapes=[pltpu.VMEM((nb * n_l, 2 * S5_BLOCK), F32), pltpu.VMEM((nb * n_l, LANES), F32),
                        pltpu.VMEM((nb, n_l, LANES), F32)],
        compiler_params=_cparams("parallel", "parallel"),
        name="s5_scan",
    )(proj, prm["wb"], prm["ar"], prm["ai"], *prm["pows"][seg], h0r, h0i, prm["wc"], prm["d"])


def _s5_params(lam_re, lam_im, log_dt, b_re, b_im, c_re, c_im, d_skip, seg_lens):
    depth = lam_re.shape[0]
    merge = lambda t: t.reshape((-1,) + t.shape[2:])
    lam_re, lam_im, log_dt, b_re, b_im, c_re, c_im = map(merge, (lam_re, lam_im, log_dt, b_re, b_im, c_re, c_im))
    n_g, n_p = lam_re.shape
    dt = jnp.exp(log_dt)[:, None]
    e = jnp.exp(lam_re * dt)
    ab_re, ab_im = e * jnp.cos(lam_im * dt), e * jnp.sin(lam_im * dt)
    den = lam_re * lam_re + lam_im * lam_im
    nr, ni = ab_re - 1.0, ab_im
    q_re = (nr * lam_re + ni * lam_im) / den
    q_im = (ni * lam_re - nr * lam_im) / den
    bb_re = q_re[..., None] * b_re - q_im[..., None] * b_im
    bb_im = q_re[..., None] * b_im + q_im[..., None] * b_re
    gpb = S5_BLOCK // n_p
    nblk = n_g // gpb
    eye = jnp.eye(gpb, dtype=F32)

    def pack_b(t):
        t = t.reshape(nblk, gpb, n_p, SSM_GROUP)
        return jnp.einsum("sgpc,gh->sgchp", t, eye).reshape(nblk, gpb * SSM_GROUP, gpb * n_p).astype(BF16)

    def pack_c(t):
        t = t.reshape(nblk, gpb, SSM_GROUP, n_p)
        return jnp.einsum("sgcp,gh->shpgc", t, eye).reshape(nblk, gpb * n_p, gpb * SSM_GROUP).astype(BF16)

    flat = lambda t: t.reshape(nblk, 1, gpb * n_p)
    nlb = gpb * n_p // LANES
    wb = jnp.stack([pack_b(bb_re).reshape(nblk, -1, nlb, LANES), pack_b(bb_im).reshape(nblk, -1, nlb, LANES)], axis=3)
    wc = jnp.stack([pack_c(c_re).reshape(nblk, nlb, LANES, -1), -pack_c(c_im).reshape(nblk, nlb, LANES, -1)], axis=2)
    out = {"wb": wb.reshape(nblk, -1, 2 * gpb * n_p), "wc": wc.reshape(nblk, 2 * gpb * n_p, -1),
           "ar": flat(ab_re), "ai": flat(ab_im), "d": d_skip.reshape(1, -1), "pows": {}, "nblk": nblk // depth}
    for seg in seg_lens:
        assert seg & (seg - 1) == 0
        pr, pi = ab_re, ab_im
        for _ in range(int(math.log2(seg))):
            pr, pi = pr * pr - pi * pi, 2.0 * pr * pi
        out["pows"][seg] = (flat(pr), flat(pi))
    return out


def _att_prompt_kernel(q_ref, k_ref, v_ref, bias_ref, o_ref, kpad, vpad, *, n_l, scale):
    zpad = jnp.zeros((ATT_REACH, LANES), BF16)
    kpad[0:ATT_REACH, :] = zpad
    vpad[0:ATT_REACH, :] = zpad
    kpad[ATT_REACH:, :] = k_ref[0].astype(BF16)
    vpad[ATT_REACH:, :] = v_ref[0].astype(BF16)
    lane = lax.broadcasted_iota(jnp.int32, (1, LANES), 1)
    heads = (lane < HEAD_DIM, lane >= HEAD_DIM)
    col = lax.broadcasted_iota(jnp.int32, (1, K_WINDOW), 1)
    nt = (((1,), (1,)), ((), ()))

    def block(r0, n_masked):
        q = q_ref[0, pl.ds(r0, Q_BLOCK), :] * scale
        kw = kpad[pl.ds(r0, K_WINDOW), :]
        vw = vpad[pl.ds(r0, K_WINDOW), :]
        s = [lax.dot_general(jnp.where(hd, q, 0.0).astype(BF16), kw, nt, preferred_element_type=F32) + bias_ref[0, h]
             for h, hd in enumerate(heads)]
        if n_masked:
            s = [jnp.where(col < n_masked, NEG_INF, x) for x in s]
        m = [jnp.max(x, axis=-1, keepdims=True) for x in s]
        p = [jnp.exp(x - mx) for x, mx in zip(s, m)]
        l = [jnp.sum(x, axis=-1, keepdims=True) for x in p]
        o = [jnp.dot(x.astype(BF16), vw, preferred_element_type=F32) / lx for x, lx in zip(p, l)]
        o_ref[0, pl.ds(r0, Q_BLOCK), :] = jnp.where(heads[0], o[0], o[1]).astype(o_ref.dtype)

    n_blocks = n_l // Q_BLOCK
    n_first = min(n_blocks, ATT_REACH // Q_BLOCK)
    for qb in range(n_first):
        block(qb * Q_BLOCK, ATT_REACH - qb * Q_BLOCK)

    def body(qb, carry):
        block(pl.multiple_of(qb * Q_BLOCK, Q_BLOCK), 0)
        return carry

    lax.fori_loop(n_first, n_blocks, body, 0, unroll=max(1, n_blocks - n_first))


def _att_prompt(proj, bias, layer, q_col, k_col, v_col, n_pairs):
    n_b, n_l, _ = proj.shape
    assert n_l % Q_BLOCK == 0
    p0 = layer * n_pairs
    spec = lambda c0: pl.BlockSpec((1, n_l, LANES), lambda b, p: (b, 0, c0 + p))
    return pl.pallas_call(
        functools.partial(_att_prompt_kernel, n_l=n_l, scale=HEAD_DIM ** -0.5),
        grid=(n_b, n_pairs),
        in_specs=[spec(q_col), spec(k_col), spec(v_col),
                  pl.BlockSpec((1, 2, Q_BLOCK, K_WINDOW), lambda b, p: (p0 + p, 0, 0, 0))],
        out_specs=pl.BlockSpec((1, n_l, LANES), lambda b, p: (b, 0, p)),
        out_shape=jax.ShapeDtypeStruct((n_b, n_l, n_pairs * LANES), BF16),
        scratch_shapes=[pltpu.VMEM((n_l + ATT_REACH, LANES), BF16), pltpu.VMEM((n_l + ATT_REACH, LANES), BF16)],
        compiler_params=_cparams("parallel", "parallel"),
        name="att_prompt",
    )(proj, proj, proj, bias)


def _bias_kernel(g_ref, bp_ref, bs_ref):
    g = g_ref[0]
    n_q, n_k = bp_ref.shape[1], bp_ref.shape[2]
    toep = pltpu.roll(jnp.broadcast_to(g, (n_q, g.shape[1])), 0, 1, stride=1, stride_axis=0)[:, :n_k]
    qc = lax.broadcasted_iota(jnp.int32, (n_q, n_k), 0) // CHUNK
    kc = lax.broadcasted_iota(jnp.int32, (n_q, n_k), 1) // CHUNK
    bp_ref[0] = jnp.where((kc >= qc) & (kc <= qc + LEFT_CHUNKS), toep, NEG_INF)
    n_s, n_ws = bs_ref.shape[1], bs_ref.shape[2]
    bs_ref[0] = pltpu.roll(jnp.broadcast_to(g, (n_s, g.shape[1])), 0, 1, stride=1, stride_axis=0)[:, :n_ws]


def _rel_bias(table, n_s, n_w):
    n_h = table.shape[0]
    assert n_w == ATT_REACH
    width = 1024
    assert width >= K_WINDOW + Q_BLOCK and width % LANES == 0
    d = np.arange(width)
    d = np.where(d < width - Q_BLOCK, d, d - width)
    idx = np.clip(ATT_REACH - d, -REL_CLIP, REL_CLIP) + REL_CLIP
    diag = table[:, idx].astype(F32).reshape(n_h, 1, width)
    return pl.pallas_call(
        _bias_kernel,
        grid=(n_h,),
        in_specs=[pl.BlockSpec((1, 1, width), lambda h: (h, 0, 0))],
        out_specs=[pl.BlockSpec((1, Q_BLOCK, K_WINDOW), lambda h: (h, 0, 0)),
                   pl.BlockSpec((1, n_s, n_w + n_s), lambda h: (h, 0, 0))],
        out_shape=[jax.ShapeDtypeStruct((n_h, Q_BLOCK, K_WINDOW), F32),
                   jax.ShapeDtypeStruct((n_h, n_s, n_w + n_s), F32)],
        compiler_params=_cparams("parallel"),
        name="rel_bias",
    )(diag)


def _att_step_kernel(q_ref, kn_ref, vn_ref, kc_ref, vc_ref, bc_ref, bn_ref, o_ref, *, n_h, scale):
    n_s = q_ref.shape[1]
    d = q_ref.shape[2]
    q = q_ref[0] * scale
    qt = jnp.concatenate([q] * n_h, axis=0)
    rh = lax.broadcasted_iota(jnp.int32, (n_h * n_s, d), 0) // n_s
    ch = lax.broadcasted_iota(jnp.int32, (n_h * n_s, d), 1) // HEAD_DIM
    qbd = jnp.where(rh == ch, qt, 0.0).astype(BF16)
    nt = (((1,), (1,)), ((), ()))
    s1 = lax.dot_general(qbd, kc_ref[0, 0].astype(BF16), nt, preferred_element_type=F32) + bc_ref[...]
    s2 = lax.dot_general(qbd, kn_ref[0].astype(BF16), nt, preferred_element_type=F32) + bn_ref[...]
    m = jnp.maximum(jnp.max(s1, axis=-1, keepdims=True), jnp.max(s2, axis=-1, keepdims=True))
    p1 = jnp.exp(s1 - m)
    p2 = jnp.exp(s2 - m)
    l = jnp.sum(p1, axis=-1, keepdims=True) + jnp.sum(p2, axis=-1, keepdims=True)
    o = (jnp.dot(p1.astype(BF16), vc_ref[0, 0].astype(BF16), preferred_element_type=F32)
         + jnp.dot(p2.astype(BF16), vn_ref[0].astype(BF16), preferred_element_type=F32)) / l
    first = lax.broadcasted_iota(jnp.int32, (1, LANES), 1) < HEAD_DIM
    for j in range(d // LANES):
        blk = o[:, j * LANES:(j + 1) * LANES]
        lo = blk[(2 * j) * n_s:(2 * j + 1) * n_s, :]
        hi = blk[(2 * j + 1) * n_s:(2 * j + 2) * n_s, :]
        o_ref[0, :, j * LANES:(j + 1) * LANES] = jnp.where(first, lo, hi).astype(o_ref.dtype)


def _att_step(proj, k_cache, v_cache, layer, bias_c, bias_n, q_col, k_col, v_col, n_h):
    n_b, n_s, _ = proj.shape
    n_w, d = k_cache.shape[2], k_cache.shape[3]
    spec = lambda c0: pl.BlockSpec((1, n_s, d), lambda b: (b, 0, c0))
    cspec = pl.BlockSpec((1, 1, n_w, d), lambda b: (layer, b, 0, 0))
    bspec = lambda a: pl.BlockSpec((None,) + a.shape[1:], lambda b: (layer, 0, 0))
    return pl.pallas_call(
        functools.partial(_att_step_kernel, n_h=n_h, scale=HEAD_DIM ** -0.5),
        grid=(n_b,),
        in_specs=[spec(q_col), spec(k_col), spec(v_col), cspec, cspec, bspec(bias_c), bspec(bias_n)],
        out_specs=pl.BlockSpec((1, n_s, d), lambda b: (b, 0, 0)),
        out_shape=jax.ShapeDtypeStruct((n_b, n_s, d), BF16),
        compiler_params=_cparams("parallel"),
        name="att_step",
    )(proj, proj, proj, k_cache, v_cache, bias_c, bias_n)


def _rwkv_prep_kernel(r_ref, k_ref, v_ref, u_ref, sh_ref, mu_ref, w0_ref, w1_ref, w2_ref, a0_ref, a1_ref, a2_ref,
                      kk_ref, ka_ref, nkk_out, w_out, b_out, k_out, r_out, v_out, prev):
    j = pl.program_id(1)

    @pl.when(j == 0)
    def _():
        prev[...] = sh_ref[0]

    n_r = r_ref.shape[1]
    row0 = lax.broadcasted_iota(jnp.int32, (SUBLANES, 1), 0) == 0

    def delta(x, i):
        sh = pltpu.roll(x, 1, axis=0)
        top = jnp.where(row0, prev[i:i + 1, :], sh[:SUBLANES])
        sh = jnp.concatenate([top, sh[SUBLANES:]], axis=0) if n_r > SUBLANES else top
        prev[i:i + 1, :] = x[n_r - 1:, :]
        return sh - x

    rp, kp, vp, up = (ref[0].astype(F32) for ref in (r_ref, k_ref, v_ref, u_ref))
    du = delta(up, 3)
    r = rp + delta(rp, 0) * mu_ref[0:1, :]
    k = kp + delta(kp, 1) * mu_ref[1:2, :]
    v = vp + delta(vp, 2) * mu_ref[2:3, :]
    xw = up + du * mu_ref[3:4, :]
    xa = up + du * mu_ref[4:5, :]
    dot = lambda a, b: jnp.dot(a.astype(BF16), b, preferred_element_type=F32)
    z = -(w0_ref[...] + dot(jnp.tanh(dot(xw, w1_ref[...])), w2_ref[...]))
    softplus = jnp.maximum(z, 0.0) + jnp.log(1.0 + jnp.exp(-jnp.abs(z)))
    w = -softplus - 0.5
    a = jax.nn.sigmoid(a0_ref[...] + dot(dot(xa, a1_ref[...]), a2_ref[...]))
    kk = k * kk_ref[...]
    ss = _head_sum(kk * kk, _block_ones(kk.shape[1]))
    kk = kk * lax.rsqrt(jnp.maximum(ss, 1e-24))
    nkk_out[0] = (-kk).astype(nkk_out.dtype)
    w_out[0] = -jnp.exp(w)
    b_out[0] = (kk * a).astype(b_out.dtype)
    k_out[0] = (k * (1.0 + (a - 1.0) * ka_ref[...])).astype(k_out.dtype)
    r_out[0] = r.astype(r_out.dtype)
    v_out[0] = v.astype(v_out.dtype)


def _layer_spec(a, layer):
    return pl.BlockSpec((None,) + a.shape[1:], lambda *_: (layer,) + (0,) * (a.ndim - 1),
                        pipeline_mode=pl.Buffered(1))


def _rwkv_prep(proj, shift0, lw, col0):
    n_b, n_l, _ = proj.shape
    d = shift0.shape[2]
    tr = min(n_l, 512)
    assert n_l % tr == 0
    spec = lambda c: pl.BlockSpec((1, tr, d), lambda b, j: (b, j, c))
    full = lambda a: _layer_spec(a, lw["layer"])
    ws = [lw["rwkv_mu"], lw["rwkv_w0"], lw["rwkv_w1"], lw["rwkv_w2"], lw["rwkv_a0"], lw["rwkv_a1"], lw["rwkv_a2"],
          lw["rwkv_k_k"], lw["rwkv_k_a"]]
    ospec = pl.BlockSpec((1, tr, d), lambda b, j: (b, j, 0))
    return pl.pallas_call(
        _rwkv_prep_kernel,
        grid=(n_b, n_l // tr),
        in_specs=[spec(col0), spec(col0 + 1), spec(col0 + 2), spec(col0 + 3),
                  pl.BlockSpec((1, 4, d), lambda b, j: (b, 0, 0))] + [full(a) for a in ws],
        out_specs=[ospec] * 6,
        out_shape=[jax.ShapeDtypeStruct((n_b, n_l, d), F32 if i == 1 else BF16) for i in range(6)],
        scratch_shapes=[pltpu.VMEM((4, d), F32)],
        compiler_params=_cparams("parallel", "arbitrary"),
        name="rwkv_prep",
    )(proj, proj, proj, proj, shift0, *ws)


def _rwkv_rec_kernel(nkk_ref, lw_ref, b_ref, k_ref, r_ref, v_ref, s0_ref, rk_ref, g_ref, beta_ref,
                     y_ref, s_out, state, *, nbk, nch, c):
    tb = pl.program_id(2)

    @pl.when(tb == 0)
    def _():
        state[...] = s0_ref[:, 0]

    c2 = 2 * c
    lane = lax.broadcasted_iota(jnp.int32, (1, LANES), 1)
    m0 = (lane < HEAD_DIM).astype(F32)
    m1 = 1.0 - m0
    by_head = lambda x: jnp.concatenate([x * m0, x * m1], axis=0)
    mm = lambda a, b: jnp.dot(a.astype(BF16), b.astype(BF16), preferred_element_type=F32)
    mm_nt = lambda a, b: lax.dot_general(a.astype(BF16), b.astype(BF16), (((1,), (1,)), ((), ())),
                                         preferred_element_type=F32)
    ii = lambda shape, d: lax.broadcasted_iota(jnp.int32, shape, d)
    tri = (ii((c, c), 0) >= ii((c, c), 1)).astype(BF16)
    rr, cc = ii((c2, c2), 0), ii((c2, c2), 1)
    strict = (rr // c == cc // c) & (cc % c < rr % c)
    incl = ii((c, c2), 1) % c <= ii((c, c2), 0)
    eye_t = (rr == cc).astype(F32)
    eye_s = (ii((LANES, LANES), 0) == ii((LANES, LANES), 1)).astype(F32)
    probs = [(bb, ci) for ci in range(nch) for bb in range(nbk)]
    rows = lambda ref, q: ref[q[0], q[1] * c:(q[1] + 1) * c, :].astype(F32)
    each = lambda fn, *lists: [fn(*xs) for xs in zip(*lists)]

    lw = [rows(lw_ref, q) for q in probs]

    def cumsum(x):
        hi = x.astype(BF16)
        lo = (x - hi.astype(F32)).astype(BF16)
        return jnp.dot(tri, hi, preferred_element_type=F32) + jnp.dot(tri, lo, preferred_element_type=F32)

    cum = each(cumsum, lw)
    tot = [x[c - 1:c, :] for x in cum]
    g_c = [jnp.exp(x) for x in tot]
    a_t = each(lambda q, x, l: rows(nkk_ref, q) * jnp.exp(x - l), probs, cum, lw)
    r_t = each(lambda q, x: rows(r_ref, q) * jnp.exp(x), probs, cum)
    ginv = [jnp.exp(-x) for x in cum]
    ghat = each(lambda x, t: jnp.exp(t - x), cum, tot)
    kx = [rows(k_ref, q) for q in probs]
    bx = [rows(b_ref, q) for q in probs]
    a_bd = [by_head(x) for x in a_t]
    lhs = each(lambda a, r: jnp.concatenate([a, r], axis=0), a_bd, r_t)
    if c2 % LANES == 0:
        sc = each(lambda l, k, b, gi: mm_nt(l, jnp.concatenate([by_head(k * gi), by_head(b * gi)], axis=0)),
                  lhs, kx, bx, ginv)
        sc_k, sc_b = [x[:, :c2] for x in sc], [x[:, c2:] for x in sc]
    else:
        sc_k = each(lambda l, k, gi: mm_nt(l, by_head(k * gi)), lhs, kx, ginv)
        sc_b = each(lambda l, b, gi: mm_nt(l, by_head(b * gi)), lhs, bx, ginv)
    l_ak = [jnp.where(strict, x[:c2], 0.0) for x in sc_k]
    l_ab = [jnp.where(strict, x[:c2], 0.0) for x in sc_b]
    m_rk = [jnp.where(incl, x[c2:], 0.0) for x in sc_k]
    m_rb = [jnp.where(incl, x[c2:], 0.0) for x in sc_b]
    col_t = ii((1, c2), 1)
    t0, t1 = (col_t < c).astype(F32), (col_t >= c).astype(F32)
    blockdiag = lambda x: jnp.concatenate([x * t0, x * t1], axis=0)
    pw = [x[:c] + x[c:] for x in l_ab]
    inv = [x + eye_t[:c] + eye_t[c:] for x in pw]
    for _ in range(int(math.log2(c)) - 1):
        pw = each(lambda x: mm(x, blockdiag(x)), pw)
        inv = each(lambda p, x: p + mm(p, blockdiag(x)), inv, pw)
    inv = [blockdiag(x) for x in inv]
    v_bd = [by_head(rows(v_ref, q)) for q in probs]
    x_in = each(lambda a, l, v: jnp.concatenate([a, mm(l, v)], axis=1), a_bd, l_ak, v_bd)
    uu = each(mm, inv, x_in)
    bh_t = each(lambda b, gh: by_head(b * gh).T, bx, ghat)
    kh_t = each(lambda k, gh: by_head(k * gh).T, kx, ghat)
    bu = each(mm, bh_t, uu)
    g_mat = each(lambda gc, x: eye_s * gc + x[:, :LANES], g_c, bu)
    h_mat = each(lambda kt, v, x: mm(kt, v) + x[:, LANES:], kh_t, v_bd, bu)
    mu = each(mm, m_rb, uu)
    y_a = each(lambda r, x: r + x[:, :LANES], r_t, mu)
    y_b = each(lambda mk, v, x: mm(mk, v) + x[:, LANES:], m_rk, v_bd, mu)

    s_cur = [state[bb] for bb in range(nbk)]
    ys = {}
    for i, (bb, ci) in enumerate(probs):
        both = mm(jnp.concatenate([y_a[i], g_mat[i]], axis=0), s_cur[bb])
        ys[(bb, ci)] = both[:c] + y_b[i]
        s_cur[bb] = both[c:] + h_mat[i]
    for bb in range(nbk):
        state[bb] = s_cur[bb]

    @pl.when(tb == pl.num_programs(2) - 1)
    def _():
        s_out[:, 0] = state[...]

    ones = _block_ones(LANES)
    inv_n = 1.0 / HEAD_DIM
    head_sum = lambda x: jnp.dot(x.astype(BF16), ones, preferred_element_type=F32)
    for bb in range(nbk):
        y = jnp.concatenate([ys[(bb, ci)] for ci in range(nch)], axis=0) if nch > 1 else ys[(bb, 0)]
        mean = head_sum(y) * inv_n
        yc = y - mean
        var = head_sum(yc * yc) * inv_n
        yn = yc * lax.rsqrt(var + GN_EPS) * g_ref[...] + beta_ref[...]
        f32 = lambda ref: ref[bb].astype(F32)
        bonus = head_sum(f32(r_ref) * f32(k_ref) * rk_ref[...]) * f32(v_ref)
        y_ref[bb] = (yn + bonus).astype(y_ref.dtype)


def _rwkv_rec(prep, s0, layer, r_k, lnx_g, lnx_b):
    n_b, n_l, d = prep[0].shape
    n_pairs = d // LANES
    c = min(n_l, RWKV_CHUNK)
    nch = min(n_l // c, RWKV_GROUP // 2)
    nbk = min(n_b, RWKV_GROUP // nch)
    lb = nch * c
    assert n_l % lb == 0 and n_b % nbk == 0 and c & (c - 1) == 0
    spec = pl.BlockSpec((nbk, lb, LANES), lambda g, p, t: (g, t, p))
    sspec = pl.BlockSpec((nbk, 1, LANES, LANES), lambda g, p, t: (g, p, 0, 0))
    vec = pl.BlockSpec((None, 1, LANES), lambda g, p, t: (layer, 0, p))
    return pl.pallas_call(
        functools.partial(_rwkv_rec_kernel, nbk=nbk, nch=nch, c=c),
        grid=(n_b // nbk, n_pairs, n_l // lb),
        in_specs=[spec] * 6 + [sspec, vec, vec, vec],
        out_specs=[spec, sspec],
        out_shape=[jax.ShapeDtypeStruct((n_b, n_l, d), BF16),
                   jax.ShapeDtypeStruct((n_b, n_pairs, LANES, LANES), F32)],
        scratch_shapes=[pltpu.VMEM((nbk, LANES, LANES), F32)],
        compiler_params=_cparams("parallel", "parallel", "arbitrary"),
        name="rwkv_rec",
    )(*prep, s0, r_k, lnx_g, lnx_b)


def _pack_pairs(s):
    n_b, n_h, n, _ = s.shape
    st = jnp.swapaxes(s, -1, -2).reshape(n_b, n_h // 2, 2, n, n)
    z = jnp.zeros_like(st[:, :, 0])
    return jnp.concatenate([jnp.concatenate([st[:, :, 0], z], axis=-1),
                            jnp.concatenate([z, st[:, :, 1]], axis=-1)], axis=-2)


def _unpack_pairs(s):
    n_b, n_p, n2, _ = s.shape
    n = n2 // 2
    st = jnp.stack([s[:, :, :n, :n], s[:, :, n:, n:]], axis=2)
    return jnp.swapaxes(st, -1, -2).reshape(n_b, 2 * n_p, n, n)


def _mix_kernel(x_ref, ys_ref, gs_ref, ya_ref, ga_ref, yc_ref, gc_ref, wg_ref, bg_ref, wo_ref, lg_ref, lb_ref,
                y_ref, *maybe_yb_ref, alpha, d_ssm, d_att):
    silu = lambda g: g * jax.nn.sigmoid(g)
    dot = lambda a, b: jnp.dot(a.astype(BF16), b, preferred_element_type=F32)
    n_r = x_ref.shape[0]
    n_part = 2 if n_r % 32 == 0 else 1
    for part in range(n_part):
        rows = slice(part * n_r // n_part, (part + 1) * n_r // n_part)
        ys = ys_ref[rows, :]
        z = 0.5 * ys * (1.0 + jnp.tanh(math.sqrt(2.0 / math.pi) * (ys + 0.044715 * (ys * ys * ys))))
        gate = lambda ref: silu(ref[rows, :].astype(F32))
        m_s = z * jax.nn.sigmoid(dot(z, wg_ref[...]) + bg_ref[...]) * gate(gs_ref)
        m_a = ya_ref[rows, :] * gate(ga_ref)
        m_c = yc_ref[rows, :] * gate(gc_ref)
        out = (dot(m_s, wo_ref[0:d_ssm, :]) + dot(m_a, wo_ref[d_ssm:d_ssm + d_att, :])
               + dot(m_c, wo_ref[d_ssm + d_att:, :]))
        h = alpha * x_ref[rows, :] + out
        mu = jnp.mean(h, axis=-1, keepdims=True)
        hc = h - mu
        var = jnp.mean(hc * hc, axis=-1, keepdims=True)
        y = hc * lax.rsqrt(var + LN_EPS) * lg_ref[...] + lb_ref[...]
        y_ref[rows, :] = y
        for yb_ref in maybe_yb_ref:
            yb_ref[rows, :] = y.astype(BF16)


def _mix(x, proj, ys, ya, yc, lw, alpha, gs_col, ga_col, gc_col, want_bf16):
    m, d = x.shape
    d_ssm, d_att, d_rw = ys.shape[1], ya.shape[1], yc.shape[1]
    tm = min(m, 512)
    assert m % tm == 0
    row = lambda w, c: pl.BlockSpec((tm, w), lambda i: (i, c))
    full = lambda a: _layer_spec(a, lw["layer"])
    ws = [lw["w_glu"], lw["b_glu"], lw["w_out"], lw["ln_g"], lw["ln_b"]]
    return pl.pallas_call(
        functools.partial(_mix_kernel, alpha=alpha, d_ssm=d_ssm, d_att=d_att),
        grid=(m // tm,),
        in_specs=[row(d, 0), row(d_ssm, 0), row(d_ssm, gs_col), row(d_att, 0), row(d_att, ga_col),
                  row(d_rw, 0), row(d_rw, gc_col)] + [full(a) for a in ws],
        out_specs=[row(d, 0)] * (2 if want_bf16 else 1),
        out_shape=[jax.ShapeDtypeStruct((m, d), F32)] + [jax.ShapeDtypeStruct((m, d), BF16)] * want_bf16,
        compiler_params=_cparams("parallel"),
        name="mix_out",
    )(x, ys, proj, ya, proj, yc, proj, *ws)


def _layer(x, proj, st, lw, s5p, alpha):
    n_b, n_l, d_model = x.shape
    d_ssm = lw["w_glu"].shape[1]
    n_h = lw["n_att_heads"]
    d_att = n_h * HEAD_DIM
    d_rw = lw["rwkv_w0"].shape[2]
    n_rh = d_rw // HEAD_DIM
    n_blk = d_ssm // LANES
    proj = proj.reshape(n_b, n_l, -1)
    q0 = 2 * d_ssm
    r0 = q0 + 4 * d_att
    assert q0 % d_att == 0 and r0 % d_rw == 0 and d_ssm == d_rw

    if st is None:
        zeros = jnp.zeros((n_b, n_blk, 1, S5_BLOCK), F32)
        h0r = h0i = zeros
        wkv0 = jnp.zeros((n_b, n_rh // 2, LANES, LANES), F32)
        shift0 = jnp.zeros((n_b, 4, d_rw), F32)
    else:
        k_cache, v_cache, h0r, h0i, wkv0, shift0 = st
        h0r = h0r.reshape(n_b, n_blk, 1, S5_BLOCK)
        h0i = h0i.reshape(n_b, n_blk, 1, S5_BLOCK)
        wkv0 = _pack_pairs(wkv0)
        shift0 = shift0.reshape(n_b, 4, d_rw)

    ys, h_re, h_im = _s5(proj, s5p, lw["layer"], h0r, h0i)

    n_keep = min(ATT_REACH, n_l)
    keep = lambda c0: proj[:, n_l - n_keep:, c0:c0 + d_att].astype(F32).reshape(n_b, n_keep, n_h, HEAD_DIM)
    k_rows, v_rows = keep(q0 + d_att), keep(q0 + 2 * d_att)
    if st is None:
        pl0 = q0 // LANES
        ya = _att_prompt(proj, lw["bias_prompt"], lw["layer"], pl0, pl0 + d_att // LANES, pl0 + 2 * d_att // LANES,
                         n_h // 2)
    else:
        c0 = q0 // d_att
        ya = _att_step(proj, k_cache, v_cache, lw["layer"], lw["bias_cache"], lw["bias_new"], c0, c0 + 1, c0 + 2, n_h)

    prep = _rwkv_prep(proj, shift0, lw, r0 // d_rw)
    yc, wkv = _rwkv_rec(prep, wkv0, lw["layer"], lw["rwkv_r_k"], lw["rwkv_lnx_g"], lw["rwkv_lnx_b"])
    shift = proj[:, n_l - 1, r0:r0 + 4 * d_rw].astype(F32)

    m = n_b * n_l
    outs = _mix(x.reshape(m, d_model), proj.reshape(m, -1), ys.reshape(m, d_ssm), ya.reshape(m, d_att),
                yc.reshape(m, d_rw), lw, alpha, 1, (q0 + 3 * d_att) // d_att, (r0 + 4 * d_rw) // d_rw,
                want_bf16=not lw["last"])
    n_g = d_ssm // SSM_GROUP
    states = (k_rows, v_rows, h_re.reshape(n_b, n_g, SSM_STATE), h_im.reshape(n_b, n_g, SSM_STATE),
              _unpack_pairs(wkv), shift)
    y, yb = outs if len(outs) == 2 else (outs[0], None)
    return y.reshape(n_b, n_l, d_model), yb, states


def kernel(x_prompt, x_sample, cache_att_k, cache_att_v, state_ssm_re, state_ssm_im, state_rwkv, state_rwkv_shift, w_in, ssm_lam_re, ssm_lam_im, ssm_log_dt, ssm_b_re, ssm_b_im, ssm_c_re, ssm_c_im, ssm_d, ssm_w_glu, ssm_b_glu, att_rel_bias, rwkv_mu, rwkv_w0, rwkv_w1, rwkv_w2, rwkv_a0, rwkv_a1, rwkv_a2, rwkv_k_k, rwkv_k_a, rwkv_r_k, rwkv_lnx_g, rwkv_lnx_b, w_out, ln_g, ln_b):
    depth = w_in.shape[0]
    alpha = (2.0 * depth) ** 0.25
    y_p, y_s = x_prompt, x_sample
    yb_p = yb_s = None
    seg_lens = sorted({x_prompt.shape[1] // S5_SEGMENTS, x_sample.shape[1] // S5_SEGMENTS})
    p_st, s_st = [], []
    n_d, n_sb, n_w = cache_att_k.shape[:3]
    n_s = x_sample.shape[1]
    n_h = att_rel_bias.shape[1]
    k_cache = cache_att_k.reshape(n_d, n_sb, n_w, -1)
    v_cache = cache_att_v.reshape(n_d, n_sb, n_w, -1)
    row = lambda a: a.reshape(depth, 1, -1)
    bias_p, bias_s = _rel_bias(att_rel_bias.reshape(depth * n_h, -1), n_s, n_w)
    bias_s = bias_s.reshape(depth, n_h * n_s, n_w + n_s)
    params = {"w_in": w_in, "n_att_heads": n_h, "w_glu": ssm_w_glu.astype(BF16), "b_glu": row(ssm_b_glu),
              "rwkv_mu": rwkv_mu, "rwkv_w0": row(rwkv_w0), "rwkv_w1": rwkv_w1.astype(BF16),
              "rwkv_w2": rwkv_w2.astype(BF16), "rwkv_a0": row(rwkv_a0), "rwkv_a1": rwkv_a1.astype(BF16),
              "rwkv_a2": rwkv_a2.astype(BF16), "rwkv_k_k": row(rwkv_k_k), "rwkv_k_a": row(rwkv_k_a),
              "rwkv_r_k": row(rwkv_r_k), "rwkv_lnx_g": row(rwkv_lnx_g), "rwkv_lnx_b": row(rwkv_lnx_b),
              "w_out": w_out.astype(BF16), "ln_g": row(ln_g), "ln_b": row(ln_b),
              "bias_prompt": bias_p.reshape(depth * n_h // 2, 2, Q_BLOCK, K_WINDOW),
              "bias_cache": bias_s[:, :, :n_w], "bias_new": bias_s[:, :, n_w:]}
    s5p = _s5_params(ssm_lam_re, ssm_lam_im, ssm_log_dt, ssm_b_re, ssm_b_im, ssm_c_re, ssm_c_im, ssm_d, seg_lens)
    for l in range(depth):
        lw = dict(params, layer=l, last=(l == depth - 1))
        if yb_p is None:
            flat = lambda a: a.reshape(-1, a.shape[-1])
            yb_p, yb_s = _to_bf16(flat(y_p)), _to_bf16(flat(y_s))
        proj_p, proj_s = _in_proj(yb_p, yb_s, w_in, l)
        y_p, yb_p, st_p = _layer(y_p, proj_p, None, lw, s5p, alpha)
        y_s, yb_s, st_s = _layer(y_s, proj_s, (k_cache, v_cache, state_ssm_re[l], state_ssm_im[l],
                                             state_rwkv[l], state_rwkv_shift[l]), lw, s5p, alpha)
        p_st.append(st_p)
        s_st.append(st_s)
    stacked = lambda states, i: jnp.stack([st[i] for st in states], axis=0)
    return (y_p, y_s) + tuple(stacked(p_st, i) for i in range(6)) + tuple(stacked(s_st, i) for i in range(6))
```

```python
import functools
import math

import jax
import jax.numpy as jnp
import numpy as np
from jax import lax
from jax.experimental import pallas as pl
from jax.experimental.pallas import tpu as pltpu

F32 = jnp.float32
BF16 = jnp.bfloat16

LANES = 128
SUBLANES = 8
VMEM_LIMIT = 56 * 1024 * 1024

CHUNK = 64
LEFT_CHUNKS = 8
ATT_REACH = LEFT_CHUNKS * CHUNK
REL_CLIP = 128
HEAD_DIM = 64
SSM_GROUP = 16
SSM_STATE = 64
RWKV_LORA = 64
NEG_INF = -1e30
GN_EPS = 64e-5
LN_EPS = 1e-5

RWKV_CHUNK = 64
RWKV_GROUP = 16
Q_BLOCK = 4 * CHUNK
K_WINDOW = Q_BLOCK + ATT_REACH
S5_SEGMENTS = SUBLANES
S5_BLOCK = 512


def _cparams(*sem):
    return pltpu.CompilerParams(dimension_semantics=sem, vmem_limit_bytes=VMEM_LIMIT)


def _block_ones(n):
    r = lax.broadcasted_iota(jnp.int32, (n, n), 0) // HEAD_DIM
    c = lax.broadcasted_iota(jnp.int32, (n, n), 1) // HEAD_DIM
    return (r == c).astype(BF16)


def _head_sum(x, ones):
    hi = x.astype(BF16)
    lo = (x - hi.astype(F32)).astype(BF16)
    return jnp.dot(hi, ones, preferred_element_type=F32) + jnp.dot(lo, ones, preferred_element_type=F32)


def _matmul_kernel(x_ref, xs_ref, w_ref, o_ref, os_ref, wb):
    @pl.when(pl.program_id(1) == 0)
    def _():
        wb[...] = w_ref[0].astype(BF16)

    o_ref[...] = jnp.dot(x_ref[...], wb[...], preferred_element_type=F32).astype(o_ref.dtype)

    @pl.when(pl.program_id(1) == pl.num_programs(1) - 1)
    def _():
        os_ref[...] = jnp.dot(xs_ref[...], wb[...], preferred_element_type=F32).astype(os_ref.dtype)


def _cast_kernel(x_ref, o_ref):
    o_ref[...] = x_ref[...].astype(o_ref.dtype)


def _to_bf16(x):
    m, k = x.shape
    tm = min(m, 1024)
    assert m % tm == 0
    return pl.pallas_call(
        _cast_kernel,
        grid=(m // tm,),
        in_specs=[pl.BlockSpec((tm, k), lambda i: (i, 0))],
        out_specs=pl.BlockSpec((tm, k), lambda i: (i, 0)),
        out_shape=jax.ShapeDtypeStruct((m, k), BF16),
        compiler_params=_cparams("parallel"),
        name="to_bf16",
    )(x)


def _in_proj(x, xs, w, layer):
    m, k = x.shape
    ms = xs.shape[0]
    n = w.shape[2]
    tm = min(m, 1024)
    tn = 1536
    assert m % tm == 0 and n % tn == 0 and x.dtype == BF16 and xs.dtype == BF16
    return pl.pallas_call(
        _matmul_kernel,
        grid=(n // tn, m // tm),
        in_specs=[pl.BlockSpec((tm, k), lambda j, i: (i, 0)),
                  pl.BlockSpec((ms, k), lambda j, i: (0, 0)),
                  pl.BlockSpec((1, k, tn), lambda j, i: (layer, 0, j))],
        out_specs=[pl.BlockSpec((tm, tn), lambda j, i: (i, j)),
                   pl.BlockSpec((ms, tn), lambda j, i: (0, j))],
        out_shape=[jax.ShapeDtypeStruct((m, n), BF16), jax.ShapeDtypeStruct((ms, n), BF16)],
        scratch_shapes=[pltpu.VMEM((k, tn), BF16)],
        compiler_params=_cparams("parallel", "arbitrary"),
        name="in_proj",
    )(x, xs, w)


def _s5_kernel(u_ref, wb_ref, ar_ref, ai_ref, pr_ref, pi_ref, h0r_ref, h0i_ref,
               wc_ref, d_ref, y_ref, hr_out, hi_out, h, up, uf, *, seg, nb):
    nlb = h.shape[1] // (2 * LANES)
    n_l = seg * S5_SEGMENTS
    lb = lambda j: slice(j * LANES, (j + 1) * LANES)
    re = lambda j: slice(2 * j * LANES, (2 * j + 1) * LANES)
    im = lambda j: slice((2 * j + 1) * LANES, (2 * j + 2) * LANES)
    aligned = lambda r: r if isinstance(r, int) else pl.multiple_of(r, S5_SEGMENTS)
    rows = lambda bb, i: pl.ds(aligned(bb * n_l + i * S5_SEGMENTS), S5_SEGMENTS)
    strided = lambda i: pl.ds(i, S5_SEGMENTS, stride=seg)
    streams = range(nb)

    def loop(body, init):
        if seg <= 8:
            for i in range(seg):
                init = body(i, init)
            return init
        return lax.fori_loop(0, seg, body, init, unroll=8)

    uf[...] = u_ref[...].astype(F32)

    def interleave(i, c):
        for bb in streams:
            up[rows(bb, i), :] = uf[bb, strided(i), :]
        return c

    loop(interleave, 0)
    u = up[...]
    ub = u.astype(BF16)
    h[...] = jnp.dot(ub, wb_ref[0], preferred_element_type=F32)
    bcast = lambda ref, j: jnp.broadcast_to(ref[0, :, lb(j)], (S5_SEGMENTS, LANES))
    ar = [bcast(ar_ref, j) for j in range(nlb)]
    ai = [bcast(ai_ref, j) for j in range(nlb)]
    chains = [(bb, j) for bb in streams for j in range(nlb)]
    n_ch = len(chains)

    def advance(i, c, store):
        nr, ni = [], []
        for q, (bb, j) in enumerate(chains):
            hr, hi = c[q], c[n_ch + q]
            r = ar[j] * hr - ai[j] * hi + h[rows(bb, i), re(j)]
            m = ar[j] * hi + ai[j] * hr + h[rows(bb, i), im(j)]
            if store:
                h[rows(bb, i), re(j)] = r
                h[rows(bb, i), im(j)] = m
            nr.append(r)
            ni.append(m)
        return tuple(nr + ni)

    zero = jnp.zeros((S5_SEGMENTS, LANES), F32)
    ends = loop(lambda i, c: advance(i, c, False), (zero,) * (2 * n_ch))

    starts_r, starts_i = [], []
    for q, (bb, j) in enumerate(chains):
        er, ei = ends[q], ends[n_ch + q]
        pr, pi = pr_ref[0, :, lb(j)], pi_ref[0, :, lb(j)]
        cr, ci = h0r_ref[bb, 0, :, lb(j)], h0i_ref[bb, 0, :, lb(j)]
        start_r, start_i = [cr], [ci]
        for s in range(S5_SEGMENTS - 1):
            cr, ci = (pr * cr - pi * ci + er[s:s + 1, :], pr * ci + pi * cr + ei[s:s + 1, :])
            start_r.append(cr)
            start_i.append(ci)
        starts_r.append(jnp.concatenate(start_r, axis=0))
        starts_i.append(jnp.concatenate(start_i, axis=0))

    last = loop(lambda i, c: advance(i, c, True), tuple(starts_r + starts_i))
    y = d_ref[...] * u
    for q, (bb, j) in enumerate(chains):
        hr_out[bb, 0, :, lb(j)] = last[q][S5_SEGMENTS - 1:, :]
        hi_out[bb, 0, :, lb(j)] = last[n_ch + q][S5_SEGMENTS - 1:, :]
    y += jnp.dot(h[...].astype(BF16), wc_ref[0], preferred_element_type=F32)
    up[...] = y

    def deinterleave(i, c):
        for bb in streams:
            y_ref[bb, strided(i), :] = up[rows(bb, i), :]
        return c

    loop(deinterleave, 0)


def _s5(proj, prm, layer, h0r, h0i):
    n_b, n_l, _ = proj.shape
    nblk = prm["nblk"]
    cin = prm["wb"].shape[1]
    b0 = layer * nblk
    seg = n_l // S5_SEGMENTS
    nb = n_b if seg <= 8 else 1
    assert seg * S5_SEGMENTS == n_l and cin == LANES
    wspec = lambda shape: pl.BlockSpec((1,) + shape, lambda b, s: (b0 + s, 0, 0))
    st_spec = pl.BlockSpec((nb, 1, 1, S5_BLOCK), lambda b, s: (b, s, 0, 0))
    return pl.pallas_call(
        functools.partial(_s5_kernel, seg=seg, nb=nb),
        grid=(n_b // nb, nblk),
        in_specs=[pl.BlockSpec((nb, n_l, cin), lambda b, s: (b, 0, s)),
                  wspec((cin, 2 * S5_BLOCK)),
                  wspec((1, S5_BLOCK)), wspec((1, S5_BLOCK)), wspec((1, S5_BLOCK)), wspec((1, S5_BLOCK)),
                  st_spec, st_spec,
                  wspec((2 * S5_BLOCK, cin)),
                  pl.BlockSpec((1, cin), lambda b, s: (0, b0 + s))],
        out_specs=[pl.BlockSpec((nb, n_l, cin), lambda b, s: (b, 0, s)), st_spec, st_spec],
        out_shape=[jax.ShapeDtypeStruct((n_b, n_l, nblk * cin), F32),
                   jax.ShapeDtypeStruct((n_b, nblk, 1, S5_BLOCK), F32),
                   jax.ShapeDtypeStruct((n_b, nblk, 1, S5_BLOCK), F32)],
        scratch_shapes=[pltpu.VMEM((nb * n_l, 2 * S5_BLOCK), F32), pltpu.VMEM((nb * n_l, LANES), F32),
                        pltpu.VMEM((nb, n_l, LANES), F32)],
        compiler_params=_cparams("parallel", "parallel"),
        name="s5_scan",
    )(proj, prm["wb"], prm["ar"], prm["ai"], *prm["pows"][seg], h0r, h0i, prm["wc"], prm["d"])


def _s5_params(lam_re, lam_im, log_dt, b_re, b_im, c_re, c_im, d_skip, seg_lens):
    depth = lam_re.shape[0]
    merge = lambda t: t.reshape((-1,) + t.shape[2:])
    lam_re, lam_im, log_dt, b_re, b_im, c_re, c_im = map(merge, (lam_re, lam_im, log_dt, b_re, b_im, c_re, c_im))
    n_g, n_p = lam_re.shape
    dt = jnp.exp(log_dt)[:, None]
    e = jnp.exp(lam_re * dt)
    ab_re, ab_im = e * jnp.cos(lam_im * dt), e * jnp.sin(lam_im * dt)
    den = lam_re * lam_re + lam_im * lam_im
    nr, ni = ab_re - 1.0, ab_im
    q_re = (nr * lam_re + ni * lam_im) / den
    q_im = (ni * lam_re - nr * lam_im) / den
    bb_re = q_re[..., None] * b_re - q_im[..., None] * b_im
    bb_im = q_re[..., None] * b_im + q_im[..., None] * b_re
    gpb = S5_BLOCK // n_p
    nblk = n_g // gpb
    eye = jnp.eye(gpb, dtype=F32)

    def pack_b(t):
        t = t.reshape(nblk, gpb, n_p, SSM_GROUP)
        return jnp.einsum("sgpc,gh->sgchp", t, eye).reshape(nblk, gpb * SSM_GROUP, gpb * n_p).astype(BF16)

    def pack_c(t):
        t = t.reshape(nblk, gpb, SSM_GROUP, n_p)
        return jnp.einsum("sgcp,gh->shpgc", t, eye).reshape(nblk, gpb * n_p, gpb * SSM_GROUP).astype(BF16)

    flat = lambda t: t.reshape(nblk, 1, gpb * n_p)
    nlb = gpb * n_p // LANES
    wb = jnp.stack([pack_b(bb_re).reshape(nblk, -1, nlb, LANES), pack_b(bb_im).reshape(nblk, -1, nlb, LANES)], axis=3)
    wc = jnp.stack([pack_c(c_re).reshape(nblk, nlb, LANES, -1), -pack_c(c_im).reshape(nblk, nlb, LANES, -1)], axis=2)
    out = {"wb": wb.reshape(nblk, -1, 2 * gpb * n_p), "wc": wc.reshape(nblk, 2 * gpb * n_p, -1),
           "ar": flat(ab_re), "ai": flat(ab_im), "d": d_skip.reshape(1, -1), "pows": {}, "nblk": nblk // depth}
    for seg in seg_lens:
        assert seg & (seg - 1) == 0
        pr, pi = ab_re, ab_im
        for _ in range(int(math.log2(seg))):
            pr, pi = pr * pr - pi * pi, 2.0 * pr * pi
        out["pows"][seg] = (flat(pr), flat(pi))
    return out


def _att_prompt_kernel(q_ref, k_ref, v_ref, bias_ref, o_ref, kpad, vpad, *, n_l, scale):
    zpad = jnp.zeros((ATT_REACH, LANES), BF16)
    kpad[0:ATT_REACH, :] = zpad
    vpad[0:ATT_REACH, :] = zpad
    kpad[ATT_REACH:, :] = k_ref[0].astype(BF16)
    vpad[ATT_REACH:, :] = v_ref[0].astype(BF16)
    lane = lax.broadcasted_iota(jnp.int32, (1, LANES), 1)
    heads = (lane < HEAD_DIM, lane >= HEAD_DIM)
    col = lax.broadcasted_iota(jnp.int32, (1, K_WINDOW), 1)
    nt = (((1,), (1,)), ((), ()))

    def block(r0, n_masked):
        q = q_ref[0, pl.ds(r0, Q_BLOCK), :] * scale
        kw = kpad[pl.ds(r0, K_WINDOW), :]
        vw = vpad[pl.ds(r0, K_WINDOW), :]
        s = [lax.dot_general(jnp.where(hd, q, 0.0).astype(BF16), kw, nt, preferred_element_type=F32) + bias_ref[0, h]
             for h, hd in enumerate(heads)]
        if n_masked:
            s = [jnp.where(col < n_masked, NEG_INF, x) for x in s]
        m = [jnp.max(x, axis=-1, keepdims=True) for x in s]
        p = [jnp.exp(x - mx) for x, mx in zip(s, m)]
        l = [jnp.sum(x, axis=-1, keepdims=True) for x in p]
        o = [jnp.dot(x.astype(BF16), vw, preferred_element_type=F32) / lx for x, lx in zip(p, l)]
        o_ref[0, pl.ds(r0, Q_BLOCK), :] = jnp.where(heads[0], o[0], o[1]).astype(o_ref.dtype)

    n_blocks = n_l // Q_BLOCK
    n_first = min(n_blocks, ATT_REACH // Q_BLOCK)
    for qb in range(n_first):
        block(qb * Q_BLOCK, ATT_REACH - qb * Q_BLOCK)

    def body(qb, carry):
        block(pl.multiple_of(qb * Q_BLOCK, Q_BLOCK), 0)
        return carry

    lax.fori_loop(n_first, n_blocks, body, 0, unroll=max(1, n_blocks - n_first))


def _att_prompt(proj, bias, layer, q_col, k_col, v_col, n_pairs):
    n_b, n_l, _ = proj.shape
    assert n_l % Q_BLOCK == 0
    p0 = layer * n_pairs
    spec = lambda c0: pl.BlockSpec((1, n_l, LANES), lambda b, p: (b, 0, c0 + p))
    return pl.pallas_call(
        functools.partial(_att_prompt_kernel, n_l=n_l, scale=HEAD_DIM ** -0.5),
        grid=(n_b, n_pairs),
        in_specs=[spec(q_col), spec(k_col), spec(v_col),
                  pl.BlockSpec((1, 2, Q_BLOCK, K_WINDOW), lambda b, p: (p0 + p, 0, 0, 0))],
        out_specs=pl.BlockSpec((1, n_l, LANES), lambda b, p: (b, 0, p)),
        out_shape=jax.ShapeDtypeStruct((n_b, n_l, n_pairs * LANES), BF16),
        scratch_shapes=[pltpu.VMEM((n_l + ATT_REACH, LANES), BF16), pltpu.VMEM((n_l + ATT_REACH, LANES), BF16)],
        compiler_params=_cparams("parallel", "parallel"),
        name="att_prompt",
    )(proj, proj, proj, bias)


def _bias_kernel(g_ref, bp_ref, bs_ref):
    g = g_ref[0]
    n_q, n_k = bp_ref.shape[1], bp_ref.shape[2]
    toep = pltpu.roll(jnp.broadcast_to(g, (n_q, g.shape[1])), 0, 1, stride=1, stride_axis=0)[:, :n_k]
    qc = lax.broadcasted_iota(jnp.int32, (n_q, n_k), 0) // CHUNK
    kc = lax.broadcasted_iota(jnp.int32, (n_q, n_k), 1) // CHUNK
    bp_ref[0] = jnp.where((kc >= qc) & (kc <= qc + LEFT_CHUNKS), toep, NEG_INF)
    n_s, n_ws = bs_ref.shape[1], bs_ref.shape[2]
    bs_ref[0] = pltpu.roll(jnp.broadcast_to(g, (n_s, g.shape[1])), 0, 1, stride=1, stride_axis=0)[:, :n_ws]


def _rel_bias(table, n_s, n_w):
    n_h = table.shape[0]
    assert n_w == ATT_REACH
    width = 1024
    assert width >= K_WINDOW + Q_BLOCK and width % LANES == 0
    d = np.arange(width)
    d = np.where(d < width - Q_BLOCK, d, d - width)
    idx = np.clip(ATT_REACH - d, -REL_CLIP, REL_CLIP) + REL_CLIP
    diag = table[:, idx].astype(F32).reshape(n_h, 1, width)
    return pl.pallas_call(
        _bias_kernel,
        grid=(n_h,),
        in_specs=[pl.BlockSpec((1, 1, width), lambda h: (h, 0, 0))],
        out_specs=[pl.BlockSpec((1, Q_BLOCK, K_WINDOW), lambda h: (h, 0, 0)),
                   pl.BlockSpec((1, n_s, n_w + n_s), lambda h: (h, 0, 0))],
        out_shape=[jax.ShapeDtypeStruct((n_h, Q_BLOCK, K_WINDOW), F32),
                   jax.ShapeDtypeStruct((n_h, n_s, n_w + n_s), F32)],
        compiler_params=_cparams("parallel"),
        name="rel_bias",
    )(diag)


def _att_step_kernel(q_ref, kn_ref, vn_ref, kc_ref, vc_ref, bc_ref, bn_ref, o_ref, *, n_h, scale):
    n_s = q_ref.shape[1]
    d = q_ref.shape[2]
    q = q_ref[0] * scale
    qt = jnp.concatenate([q] * n_h, axis=0)
    rh = lax.broadcasted_iota(jnp.int32, (n_h * n_s, d), 0) // n_s
    ch = lax.broadcasted_iota(jnp.int32, (n_h * n_s, d), 1) // HEAD_DIM
    qbd = jnp.where(rh == ch, qt, 0.0).astype(BF16)
    nt = (((1,), (1,)), ((), ()))
    s1 = lax.dot_general(qbd, kc_ref[0, 0].astype(BF16), nt, preferred_element_type=F32) + bc_ref[...]
    s2 = lax.dot_general(qbd, kn_ref[0].astype(BF16), nt, preferred_element_type=F32) + bn_ref[...]
    m = jnp.maximum(jnp.max(s1, axis=-1, keepdims=True), jnp.max(s2, axis=-1, keepdims=True))
    p1 = jnp.exp(s1 - m)
    p2 = jnp.exp(s2 - m)
    l = jnp.sum(p1, axis=-1, keepdims=True) + jnp.sum(p2, axis=-1, keepdims=True)
    o = (jnp.dot(p1.astype(BF16), vc_ref[0, 0].astype(BF16), preferred_element_type=F32)
         + jnp.dot(p2.astype(BF16), vn_ref[0].astype(BF16), preferred_element_type=F32)) / l
    first = lax.broadcasted_iota(jnp.int32, (1, LANES), 1) < HEAD_DIM
    for j in range(d // LANES):
        blk = o[:, j * LANES:(j + 1) * LANES]
        lo = blk[(2 * j) * n_s:(2 * j + 1) * n_s, :]
        hi = blk[(2 * j + 1) * n_s:(2 * j + 2) * n_s, :]
        o_ref[0, :, j * LANES:(j + 1) * LANES] = jnp.where(first, lo, hi).astype(o_ref.dtype)


def _att_step(proj, k_cache, v_cache, layer, bias_c, bias_n, q_col, k_col, v_col, n_h):
    n_b, n_s, _ = proj.shape
    n_w, d = k_cache.shape[2], k_cache.shape[3]
    spec = lambda c0: pl.BlockSpec((1, n_s, d), lambda b: (b, 0, c0))
    cspec = pl.BlockSpec((1, 1, n_w, d), lambda b: (layer, b, 0, 0))
    bspec = lambda a: pl.BlockSpec((None,) + a.shape[1:], lambda b: (layer, 0, 0))
    return pl.pallas_call(
        functools.partial(_att_step_kernel, n_h=n_h, scale=HEAD_DIM ** -0.5),
        grid=(n_b,),
        in_specs=[spec(q_col), spec(k_col), spec(v_col), cspec, cspec, bspec(bias_c), bspec(bias_n)],
        out_specs=pl.BlockSpec((1, n_s, d), lambda b: (b, 0, 0)),
        out_shape=jax.ShapeDtypeStruct((n_b, n_s, d), BF16),
        compiler_params=_cparams("parallel"),
        name="att_step",
    )(proj, proj, proj, k_cache, v_cache, bias_c, bias_n)


def _rwkv_prep_kernel(r_ref, k_ref, v_ref, u_ref, sh_ref, mu_ref, w0_ref, w1_ref, w2_ref, a0_ref, a1_ref, a2_ref,
                      kk_ref, ka_ref, nkk_out, w_out, b_out, k_out, r_out, v_out, prev):
    j = pl.program_id(1)

    @pl.when(j == 0)
    def _():
        prev[...] = sh_ref[0]

    n_r = r_ref.shape[1]
    row0 = lax.broadcasted_iota(jnp.int32, (SUBLANES, 1), 0) == 0

    def delta(x, i):
        sh = pltpu.roll(x, 1, axis=0)
        top = jnp.where(row0, prev[i:i + 1, :], sh[:SUBLANES])
        sh = jnp.concatenate([top, sh[SUBLANES:]], axis=0) if n_r > SUBLANES else top
        prev[i:i + 1, :] = x[n_r - 1:, :]
        return sh - x

    rp, kp, vp, up = (ref[0].astype(F32) for ref in (r_ref, k_ref, v_ref, u_ref))
    du = delta(up, 3)
    r = rp + delta(rp, 0) * mu_ref[0:1, :]
    k = kp + delta(kp, 1) * mu_ref[1:2, :]
    v = vp + delta(vp, 2) * mu_ref[2:3, :]
    xw = up + du * mu_ref[3:4, :]
    xa = up + du * mu_ref[4:5, :]
    dot = lambda a, b: jnp.dot(a.astype(BF16), b, preferred_element_type=F32)
    z = -(w0_ref[...] + dot(jnp.tanh(dot(xw, w1_ref[...])), w2_ref[...]))
    softplus = jnp.maximum(z, 0.0) + jnp.log(1.0 + jnp.exp(-jnp.abs(z)))
    w = -softplus - 0.5
    a = jax.nn.sigmoid(a0_ref[...] + dot(dot(xa, a1_ref[...]), a2_ref[...]))
    kk = k * kk_ref[...]
    ss = _head_sum(kk * kk, _block_ones(kk.shape[1]))
    kk = kk * lax.rsqrt(jnp.maximum(ss, 1e-24))
    nkk_out[0] = (-kk).astype(nkk_out.dtype)
    w_out[0] = -jnp.exp(w)
    b_out[0] = (kk * a).astype(b_out.dtype)
    k_out[0] = (k * (1.0 + (a - 1.0) * ka_ref[...])).astype(k_out.dtype)
    r_out[0] = r.astype(r_out.dtype)
    v_out[0] = v.astype(v_out.dtype)


def _layer_spec(a, layer):
    return pl.BlockSpec((None,) + a.shape[1:], lambda *_: (layer,) + (0,) * (a.ndim - 1),
                        pipeline_mode=pl.Buffered(1))


def _rwkv_prep(proj, shift0, lw, col0):
    n_b, n_l, _ = proj.shape
    d = shift0.shape[2]
    tr = min(n_l, 512)
    assert n_l % tr == 0
    spec = lambda c: pl.BlockSpec((1, tr, d), lambda b, j: (b, j, c))
    full = lambda a: _layer_spec(a, lw["layer"])
    ws = [lw["rwkv_mu"], lw["rwkv_w0"], lw["rwkv_w1"], lw["rwkv_w2"], lw["rwkv_a0"], lw["rwkv_a1"], lw["rwkv_a2"],
          lw["rwkv_k_k"], lw["rwkv_k_a"]]
    ospec = pl.BlockSpec((1, tr, d), lambda b, j: (b, j, 0))
    return pl.pallas_call(
        _rwkv_prep_kernel,
        grid=(n_b, n_l // tr),
        in_specs=[spec(col0), spec(col0 + 1), spec(col0 + 2), spec(col0 + 3),
                  pl.BlockSpec((1, 4, d), lambda b, j: (b, 0, 0))] + [full(a) for a in ws],
        out_specs=[ospec] * 6,
        out_shape=[jax.ShapeDtypeStruct((n_b, n_l, d), F32 if i == 1 else BF16) for i in range(6)],
        scratch_shapes=[pltpu.VMEM((4, d), F32)],
        compiler_params=_cparams("parallel", "arbitrary"),
        name="rwkv_prep",
    )(proj, proj, proj, proj, shift0, *ws)


def _rwkv_rec_kernel(nkk_ref, lw_ref, b_ref, k_ref, r_ref, v_ref, s0_ref, rk_ref, g_ref, beta_ref,
                     y_ref, s_out, state, *, nbk, nch, c):
    tb = pl.program_id(2)

    @pl.when(tb == 0)
    def _():
        state[...] = s0_ref[:, 0]

    c2 = 2 * c
    lane = lax.broadcasted_iota(jnp.int32, (1, LANES), 1)
    m0 = (lane < HEAD_DIM).astype(F32)
    m1 = 1.0 - m0
    by_head = lambda x: jnp.concatenate([x * m0, x * m1], axis=0)
    mm = lambda a, b: jnp.dot(a.astype(BF16), b.astype(BF16), preferred_element_type=F32)
    mm_nt = lambda a, b: lax.dot_general(a.astype(BF16), b.astype(BF16), (((1,), (1,)), ((), ())),
                                         preferred_element_type=F32)
    ii = lambda shape, d: lax.broadcasted_iota(jnp.int32, shape, d)
    tri = (ii((c, c), 0) >= ii((c, c), 1)).astype(BF16)
    rr, cc = ii((c2, c2), 0), ii((c2, c2), 1)
    strict = (rr // c == cc // c) & (cc % c < rr % c)
    incl = ii((c, c2), 1) % c <= ii((c, c2), 0)
    eye_t = (rr == cc).astype(F32)
    eye_s = (ii((LANES, LANES), 0) == ii((LANES, LANES), 1)).astype(F32)
    probs = [(bb, ci) for ci in range(nch) for bb in range(nbk)]
    rows = lambda ref, q: ref[q[0], q[1] * c:(q[1] + 1) * c, :].astype(F32)
    each = lambda fn, *lists: [fn(*xs) for xs in zip(*lists)]

    lw = [rows(lw_ref, q) for q in probs]

    def cumsum(x):
        hi = x.astype(BF16)
        lo = (x - hi.astype(F32)).astype(BF16)
        return jnp.dot(tri, hi, preferred_element_type=F32) + jnp.dot(tri, lo, preferred_element_type=F32)

    cum = each(cumsum, lw)
    tot = [x[c - 1:c, :] for x in cum]
    g_c = [jnp.exp(x) for x in tot]
    a_t = each(lambda q, x, l: rows(nkk_ref, q) * jnp.exp(x - l), probs, cum, lw)
    r_t = each(lambda q, x: rows(r_ref, q) * jnp.exp(x), probs, cum)
    ginv = [jnp.exp(-x) for x in cum]
    ghat = each(lambda x, t: jnp.exp(t - x), cum, tot)
    kx = [rows(k_ref, q) for q in probs]
    bx = [rows(b_ref, q) for q in probs]
    a_bd = [by_head(x) for x in a_t]
    lhs = each(lambda a, r: jnp.concatenate([a, r], axis=0), a_bd, r_t)
    if c2 % LANES == 0:
        sc = each(lambda l, k, b, gi: mm_nt(l, jnp.concatenate([by_head(k * gi), by_head(b * gi)], axis=0)),
                  lhs, kx, bx, ginv)
        sc_k, sc_b = [x[:, :c2] for x in sc], [x[:, c2:] for x in sc]
    else:
        sc_k = each(lambda l, k, gi: mm_nt(l, by_head(k * gi)), lhs, kx, ginv)
        sc_b = each(lambda l, b, gi: mm_nt(l, by_head(b * gi)), lhs, bx, ginv)
    l_ak = [jnp.where(strict, x[:c2], 0.0) for x in sc_k]
    l_ab = [jnp.where(strict, x[:c2], 0.0) for x in sc_b]
    m_rk = [jnp.where(incl, x[c2:], 0.0) for x in sc_k]
    m_rb = [jnp.where(incl, x[c2:], 0.0) for x in sc_b]
    col_t = ii((1, c2), 1)
    t0, t1 = (col_t < c).astype(F32), (col_t >= c).astype(F32)
    blockdiag = lambda x: jnp.concatenate([x * t0, x * t1], axis=0)
    pw = [x[:c] + x[c:] for x in l_ab]
    inv = [x + eye_t[:c] + eye_t[c:] for x in pw]
    for _ in range(int(math.log2(c)) - 1):
        pw = each(lambda x: mm(x, blockdiag(x)), pw)
        inv = each(lambda p, x: p + mm(p, blockdiag(x)), inv, pw)
    inv = [blockdiag(x) for x in inv]
    v_bd = [by_head(rows(v_ref, q)) for q in probs]
    x_in = each(lambda a, l, v: jnp.concatenate([a, mm(l, v)], axis=1), a_bd, l_ak, v_bd)
    uu = each(mm, inv, x_in)
    bh_t = each(lambda b, gh: by_head(b * gh).T, bx, ghat)
    kh_t = each(lambda k, gh: by_head(k * gh).T, kx, ghat)
    bu = each(mm, bh_t, uu)
    g_mat = each(lambda gc, x: eye_s * gc + x[:, :LANES], g_c, bu)
    h_mat = each(lambda kt, v, x: mm(kt, v) + x[:, LANES:], kh_t, v_bd, bu)
    mu = each(mm, m_rb, uu)
    y_a = each(lambda r, x: r + x[:, :LANES], r_t, mu)
    y_b = each(lambda mk, v, x: mm(mk, v) + x[:, LANES:], m_rk, v_bd, mu)

    s_cur = [state[bb] for bb in range(nbk)]
    ys = {}
    for i, (bb, ci) in enumerate(probs):
        both = mm(jnp.concatenate([y_a[i], g_mat[i]], axis=0), s_cur[bb])
        ys[(bb, ci)] = both[:c] + y_b[i]
        s_cur[bb] = both[c:] + h_mat[i]
    for bb in range(nbk):
        state[bb] = s_cur[bb]

    @pl.when(tb == pl.num_programs(2) - 1)
    def _():
        s_out[:, 0] = state[...]

    ones = _block_ones(LANES)
    inv_n = 1.0 / HEAD_DIM
    head_sum = lambda x: jnp.dot(x.astype(BF16), ones, preferred_element_type=F32)
    for bb in range(nbk):
        y = jnp.concatenate([ys[(bb, ci)] for ci in range(nch)], axis=0) if nch > 1 else ys[(bb, 0)]
        mean = head_sum(y) * inv_n
        yc = y - mean
        var = head_sum(yc * yc) * inv_n
        yn = yc * lax.rsqrt(var + GN_EPS) * g_ref[...] + beta_ref[...]
        f32 = lambda ref: ref[bb].astype(F32)
        bonus = head_sum(f32(r_ref) * f32(k_ref) * rk_ref[...]) * f32(v_ref)
        y_ref[bb] = (yn + bonus).astype(y_ref.dtype)


def _rwkv_rec(prep, s0, layer, r_k, lnx_g, lnx_b):
    n_b, n_l, d = prep[0].shape
    n_pairs = d // LANES
    c = min(n_l, RWKV_CHUNK)
    nch = min(n_l // c, RWKV_GROUP // 2)
    nbk = min(n_b, RWKV_GROUP // nch)
    lb = nch * c
    assert n_l % lb == 0 and n_b % nbk == 0 and c & (c - 1) == 0
    spec = pl.BlockSpec((nbk, lb, LANES), lambda g, p, t: (g, t, p))
    sspec = pl.BlockSpec((nbk, 1, LANES, LANES), lambda g, p, t: (g, p, 0, 0))
    vec = pl.BlockSpec((None, 1, LANES), lambda g, p, t: (layer, 0, p))
    return pl.pallas_call(
        functools.partial(_rwkv_rec_kernel, nbk=nbk, nch=nch, c=c),
        grid=(n_b // nbk, n_pairs, n_l // lb),
        in_specs=[spec] * 6 + [sspec, vec, vec, vec],
        out_specs=[spec, sspec],
        out_shape=[jax.ShapeDtypeStruct((n_b, n_l, d), BF16),
                   jax.ShapeDtypeStruct((n_b, n_pairs, LANES, LANES), F32)],
        scratch_shapes=[pltpu.VMEM((nbk, LANES, LANES), F32)],
        compiler_params=_cparams("parallel", "parallel", "arbitrary"),
        name="rwkv_rec",
    )(*prep, s0, r_k, lnx_g, lnx_b)


def _pack_pairs(s):
    n_b, n_h, n, _ = s.shape
    st = jnp.swapaxes(s, -1, -2).reshape(n_b, n_h // 2, 2, n, n)
    z = jnp.zeros_like(st[:, :, 0])
    return jnp.concatenate([jnp.concatenate([st[:, :, 0], z], axis=-1),
                            jnp.concatenate([z, st[:, :, 1]], axis=-1)], axis=-2)


def _unpack_pairs(s):
    n_b, n_p, n2, _ = s.shape
    n = n2 // 2
    st = jnp.stack([s[:, :, :n, :n], s[:, :, n:, n:]], axis=2)
    return jnp.swapaxes(st, -1, -2).reshape(n_b, 2 * n_p, n, n)


N_MIX_IN = 7


def _mix_kernel(*refs, alpha, d_ssm, d_att, n_out):
    main, side = refs[:N_MIX_IN], refs[N_MIX_IN:2 * N_MIX_IN]
    wg_ref, bg_ref, wo_ref, lg_ref, lb_ref = refs[2 * N_MIX_IN:2 * N_MIX_IN + 5]
    outs = refs[2 * N_MIX_IN + 5:]
    silu = lambda g: g * jax.nn.sigmoid(g)
    dot = lambda a, b: jnp.dot(a.astype(BF16), b, preferred_element_type=F32)

    def mix_rows(ins, out_refs):
        x_ref, ys_ref, gs_ref, ya_ref, ga_ref, yc_ref, gc_ref = ins
        n_r = x_ref.shape[0]
        n_part = 2 if n_r % 32 == 0 else 1
        for part in range(n_part):
            rows = slice(part * n_r // n_part, (part + 1) * n_r // n_part)
            ys = ys_ref[rows, :]
            z = 0.5 * ys * (1.0 + jnp.tanh(math.sqrt(2.0 / math.pi) * (ys + 0.044715 * (ys * ys * ys))))
            gate = lambda ref: silu(ref[rows, :].astype(F32))
            m_s = z * jax.nn.sigmoid(dot(z, wg_ref[...]) + bg_ref[...]) * gate(gs_ref)
            m_a = ya_ref[rows, :] * gate(ga_ref)
            m_c = yc_ref[rows, :] * gate(gc_ref)
            out = (dot(m_s, wo_ref[0:d_ssm, :]) + dot(m_a, wo_ref[d_ssm:d_ssm + d_att, :])
                   + dot(m_c, wo_ref[d_ssm + d_att:, :]))
            h = alpha * x_ref[rows, :] + out
            mu = jnp.mean(h, axis=-1, keepdims=True)
            hc = h - mu
            var = jnp.mean(hc * hc, axis=-1, keepdims=True)
            y = hc * lax.rsqrt(var + LN_EPS) * lg_ref[...] + lb_ref[...]
            out_refs[0][rows, :] = y
            for yb_ref in out_refs[1:]:
                yb_ref[rows, :] = y.astype(BF16)

    mix_rows(main, outs[:n_out])

    @pl.when(pl.program_id(0) == pl.num_programs(0) - 1)
    def _():
        mix_rows(side, outs[n_out:])


def _mix(main, side, lw, alpha, gs_col, ga_col, gc_col, want_bf16):
    x, proj, ys, ya, yc = main
    m, d = x.shape
    ms = side[0].shape[0]
    d_ssm, d_att, d_rw = ys.shape[1], ya.shape[1], yc.shape[1]
    tm = min(m, 512)
    assert m % tm == 0
    row = lambda w, c: pl.BlockSpec((tm, w), lambda i: (i, c))
    fixed = lambda w, c: pl.BlockSpec((ms, w), lambda i: (0, c))
    specs = lambda blk: [blk(d, 0), blk(d_ssm, 0), blk(d_ssm, gs_col), blk(d_att, 0), blk(d_att, ga_col),
                         blk(d_rw, 0), blk(d_rw, gc_col)]
    operands = lambda t: (t[0], t[2], t[1], t[3], t[1], t[4], t[1])
    full = lambda a: _layer_spec(a, lw["layer"])
    ws = [lw["w_glu"], lw["b_glu"], lw["w_out"], lw["ln_g"], lw["ln_b"]]
    n_out = 2 if want_bf16 else 1
    shapes = lambda rows: [jax.ShapeDtypeStruct((rows, d), F32)] + [jax.ShapeDtypeStruct((rows, d), BF16)] * want_bf16
    return pl.pallas_call(
        functools.partial(_mix_kernel, alpha=alpha, d_ssm=d_ssm, d_att=d_att, n_out=n_out),
        grid=(m // tm,),
        in_specs=specs(row) + specs(fixed) + [full(a) for a in ws],
        out_specs=[row(d, 0)] * n_out + [fixed(d, 0)] * n_out,
        out_shape=shapes(m) + shapes(ms),
        compiler_params=_cparams("arbitrary"),
        name="mix_out",
    )(*operands(main), *operands(side), *ws)


def _layer(x, proj, st, lw, s5p):
    n_b, n_l, d_model = x.shape
    d_ssm = lw["w_glu"].shape[1]
    n_h = lw["n_att_heads"]
    d_att = n_h * HEAD_DIM
    d_rw = lw["rwkv_w0"].shape[2]
    n_rh = d_rw // HEAD_DIM
    n_blk = d_ssm // LANES
    proj = proj.reshape(n_b, n_l, -1)
    q0 = 2 * d_ssm
    r0 = q0 + 4 * d_att
    assert q0 % d_att == 0 and r0 % d_rw == 0 and d_ssm == d_rw

    if st is None:
        zeros = jnp.zeros((n_b, n_blk, 1, S5_BLOCK), F32)
        h0r = h0i = zeros
        wkv0 = jnp.zeros((n_b, n_rh // 2, LANES, LANES), F32)
        shift0 = jnp.zeros((n_b, 4, d_rw), F32)
    else:
        k_cache, v_cache, h0r, h0i, wkv0, shift0 = st
        h0r = h0r.reshape(n_b, n_blk, 1, S5_BLOCK)
        h0i = h0i.reshape(n_b, n_blk, 1, S5_BLOCK)
        wkv0 = _pack_pairs(wkv0)
        shift0 = shift0.reshape(n_b, 4, d_rw)

    ys, h_re, h_im = _s5(proj, s5p, lw["layer"], h0r, h0i)

    n_keep = min(ATT_REACH, n_l)
    keep = lambda c0: proj[:, n_l - n_keep:, c0:c0 + d_att].astype(F32).reshape(n_b, n_keep, n_h, HEAD_DIM)
    k_rows, v_rows = keep(q0 + d_att), keep(q0 + 2 * d_att)
    if st is None:
        pl0 = q0 // LANES
        ya = _att_prompt(proj, lw["bias_prompt"], lw["layer"], pl0, pl0 + d_att // LANES, pl0 + 2 * d_att // LANES,
                         n_h // 2)
    else:
        c0 = q0 // d_att
        ya = _att_step(proj, k_cache, v_cache, lw["layer"], lw["bias_cache"], lw["bias_new"], c0, c0 + 1, c0 + 2, n_h)

    prep = _rwkv_prep(proj, shift0, lw, r0 // d_rw)
    yc, wkv = _rwkv_rec(prep, wkv0, lw["layer"], lw["rwkv_r_k"], lw["rwkv_lnx_g"], lw["rwkv_lnx_b"])
    shift = proj[:, n_l - 1, r0:r0 + 4 * d_rw].astype(F32)

    m = n_b * n_l
    mix_in = (x.reshape(m, d_model), proj.reshape(m, -1), ys.reshape(m, d_ssm), ya.reshape(m, d_att),
              yc.reshape(m, d_rw))
    gate_cols = (1, (q0 + 3 * d_att) // d_att, (r0 + 4 * d_rw) // d_rw)
    n_g = d_ssm // SSM_GROUP
    states = (k_rows, v_rows, h_re.reshape(n_b, n_g, SSM_STATE), h_im.reshape(n_b, n_g, SSM_STATE),
              _unpack_pairs(wkv), shift)
    return mix_in, gate_cols, states


def kernel(x_prompt, x_sample, cache_att_k, cache_att_v, state_ssm_re, state_ssm_im, state_rwkv, state_rwkv_shift, w_in, ssm_lam_re, ssm_lam_im, ssm_log_dt, ssm_b_re, ssm_b_im, ssm_c_re, ssm_c_im, ssm_d, ssm_w_glu, ssm_b_glu, att_rel_bias, rwkv_mu, rwkv_w0, rwkv_w1, rwkv_w2, rwkv_a0, rwkv_a1, rwkv_a2, rwkv_k_k, rwkv_k_a, rwkv_r_k, rwkv_lnx_g, rwkv_lnx_b, w_out, ln_g, ln_b):
    depth = w_in.shape[0]
    alpha = (2.0 * depth) ** 0.25
    y_p, y_s = x_prompt, x_sample
    yb_p = yb_s = None
    seg_lens = sorted({x_prompt.shape[1] // S5_SEGMENTS, x_sample.shape[1] // S5_SEGMENTS})
    p_st, s_st = [], []
    n_d, n_sb, n_w = cache_att_k.shape[:3]
    n_s = x_sample.shape[1]
    n_h = att_rel_bias.shape[1]
    k_cache = cache_att_k.reshape(n_d, n_sb, n_w, -1)
    v_cache = cache_att_v.reshape(n_d, n_sb, n_w, -1)
    row = lambda a: a.reshape(depth, 1, -1)
    bias_p, bias_s = _rel_bias(att_rel_bias.reshape(depth * n_h, -1), n_s, n_w)
    bias_s = bias_s.reshape(depth, n_h * n_s, n_w + n_s)
    params = {"w_in": w_in, "n_att_heads": n_h, "w_glu": ssm_w_glu.astype(BF16), "b_glu": row(ssm_b_glu),
              "rwkv_mu": rwkv_mu, "rwkv_w0": row(rwkv_w0), "rwkv_w1": rwkv_w1.astype(BF16),
              "rwkv_w2": rwkv_w2.astype(BF16), "rwkv_a0": row(rwkv_a0), "rwkv_a1": rwkv_a1.astype(BF16),
              "rwkv_a2": rwkv_a2.astype(BF16), "rwkv_k_k": row(rwkv_k_k), "rwkv_k_a": row(rwkv_k_a),
              "rwkv_r_k": row(rwkv_r_k), "rwkv_lnx_g": row(rwkv_lnx_g), "rwkv_lnx_b": row(rwkv_lnx_b),
              "w_out": w_out.astype(BF16), "ln_g": row(ln_g), "ln_b": row(ln_b),
              "bias_prompt": bias_p.reshape(depth * n_h // 2, 2, Q_BLOCK, K_WINDOW),
              "bias_cache": bias_s[:, :, :n_w], "bias_new": bias_s[:, :, n_w:]}
    s5p = _s5_params(ssm_lam_re, ssm_lam_im, ssm_log_dt, ssm_b_re, ssm_b_im, ssm_c_re, ssm_c_im, ssm_d, seg_lens)
    for l in range(depth):
        lw = dict(params, layer=l)
        if yb_p is None:
            flat = lambda a: a.reshape(-1, a.shape[-1])
            yb_p, yb_s = _to_bf16(flat(y_p)), _to_bf16(flat(y_s))
        proj_p, proj_s = _in_proj(yb_p, yb_s, w_in, l)
        mix_p, gate_cols, st_p = _layer(y_p, proj_p, None, lw, s5p)
        mix_s, _, st_s = _layer(y_s, proj_s, (k_cache, v_cache, state_ssm_re[l], state_ssm_im[l],
                                              state_rwkv[l], state_rwkv_shift[l]), lw, s5p)
        last = l == depth - 1
        outs = _mix(mix_p, mix_s, lw, alpha, *gate_cols, want_bf16=not last)
        (y_p, y_s), (yb_p, yb_s) = (outs if last else outs[0::2]), ((None, None) if last else outs[1::2])
        y_p, y_s = y_p.reshape(x_prompt.shape), y_s.reshape(x_sample.shape)
        p_st.append(st_p)
        s_st.append(st_s)
    stacked = lambda states, i: jnp.stack([st[i] for st in states], axis=0)
    return (y_p, y_s) + tuple(stacked(p_st, i) for i in range(6)) + tuple(stacked(s_st, i) for i in range(6))
```

```python
import functools
import math

import jax
import jax.numpy as jnp
import numpy as np
from jax import lax
from jax.experimental import pallas as pl
from jax.experimental.pallas import tpu as pltpu

F32 = jnp.float32
BF16 = jnp.bfloat16

LANES = 128
SUBLANES = 8
VMEM_LIMIT = 56 * 1024 * 1024

CHUNK = 64
LEFT_CHUNKS = 8
ATT_REACH = LEFT_CHUNKS * CHUNK
REL_CLIP = 128
HEAD_DIM = 64
SSM_GROUP = 16
SSM_STATE = 64
NEG_INF = -1e30
GN_EPS = 64e-5
LN_EPS = 1e-5

PROJ_TILE = (1024, 1536)
CAST_ROWS = 1024
PREP_ROWS = 512
MIX_ROWS = 512
S5_UNROLL = 8
RWKV_CHUNK = 64
RWKV_GROUP = 16
Q_BLOCK = 4 * CHUNK
K_WINDOW = Q_BLOCK + ATT_REACH
S5_SEGMENTS = SUBLANES
S5_BLOCK = 512


def _cparams(*sem):
    return pltpu.CompilerParams(dimension_semantics=sem, vmem_limit_bytes=VMEM_LIMIT)


def _block_ones(n):
    r = lax.broadcasted_iota(jnp.int32, (n, n), 0) // HEAD_DIM
    c = lax.broadcasted_iota(jnp.int32, (n, n), 1) // HEAD_DIM
    return (r == c).astype(BF16)


def _head_sum(x, ones):
    hi = x.astype(BF16)
    lo = (x - hi.astype(F32)).astype(BF16)
    return jnp.dot(hi, ones, preferred_element_type=F32) + jnp.dot(lo, ones, preferred_element_type=F32)


def _matmul_kernel(x_ref, xs_ref, w_ref, o_ref, os_ref, wb):
    @pl.when(pl.program_id(1) == 0)
    def _():
        wb[...] = w_ref[0].astype(BF16)

    o_ref[...] = jnp.dot(x_ref[...], wb[...], preferred_element_type=F32).astype(o_ref.dtype)

    @pl.when(pl.program_id(1) == pl.num_programs(1) - 1)
    def _():
        os_ref[...] = jnp.dot(xs_ref[...], wb[...], preferred_element_type=F32).astype(os_ref.dtype)


def _cast_kernel(x_ref, o_ref):
    o_ref[...] = x_ref[...].astype(o_ref.dtype)


def _to_bf16(x):
    m, k = x.shape
    tm = min(m, CAST_ROWS)
    assert m % tm == 0
    return pl.pallas_call(
        _cast_kernel,
        grid=(m // tm,),
        in_specs=[pl.BlockSpec((tm, k), lambda i: (i, 0))],
        out_specs=pl.BlockSpec((tm, k), lambda i: (i, 0)),
        out_shape=jax.ShapeDtypeStruct((m, k), BF16),
        compiler_params=_cparams("parallel"),
        name="to_bf16",
    )(x)


def _in_proj(x, xs, w, layer):
    m, k = x.shape
    ms = xs.shape[0]
    n = w.shape[2]
    tm, tn = min(m, PROJ_TILE[0]), PROJ_TILE[1]
    assert m % tm == 0 and n % tn == 0 and x.dtype == BF16 and xs.dtype == BF16
    return pl.pallas_call(
        _matmul_kernel,
        grid=(n // tn, m // tm),
        in_specs=[pl.BlockSpec((tm, k), lambda j, i: (i, 0)),
                  pl.BlockSpec((ms, k), lambda j, i: (0, 0)),
                  pl.BlockSpec((1, k, tn), lambda j, i: (layer, 0, j))],
        out_specs=[pl.BlockSpec((tm, tn), lambda j, i: (i, j)),
                   pl.BlockSpec((ms, tn), lambda j, i: (0, j))],
        out_shape=[jax.ShapeDtypeStruct((m, n), BF16), jax.ShapeDtypeStruct((ms, n), BF16)],
        scratch_shapes=[pltpu.VMEM((k, tn), BF16)],
        compiler_params=_cparams("parallel", "arbitrary"),
        name="in_proj",
    )(x, xs, w)


def _s5_kernel(u_ref, wb_ref, ar_ref, ai_ref, pr_ref, pi_ref, h0r_ref, h0i_ref,
               wc_ref, d_ref, y_ref, hr_out, hi_out, h, up, uf, *, seg, nb):
    nlb = h.shape[1] // (2 * LANES)
    n_l = seg * S5_SEGMENTS
    lb = lambda j: slice(j * LANES, (j + 1) * LANES)
    re = lambda j: slice(2 * j * LANES, (2 * j + 1) * LANES)
    im = lambda j: slice((2 * j + 1) * LANES, (2 * j + 2) * LANES)
    aligned = lambda r: r if isinstance(r, int) else pl.multiple_of(r, S5_SEGMENTS)
    rows = lambda bb, i: pl.ds(aligned(bb * n_l + i * S5_SEGMENTS), S5_SEGMENTS)
    strided = lambda i: pl.ds(i, S5_SEGMENTS, stride=seg)
    streams = range(nb)

    def loop(body, init):
        if seg <= S5_UNROLL:
            for i in range(seg):
                init = body(i, init)
            return init
        return lax.fori_loop(0, seg, body, init, unroll=S5_UNROLL)

    uf[...] = u_ref[...].astype(F32)

    def interleave(i, c):
        for bb in streams:
            up[rows(bb, i), :] = uf[bb, strided(i), :]
        return c

    loop(interleave, 0)
    u = up[...]
    ub = u.astype(BF16)
    h[...] = jnp.dot(ub, wb_ref[0], preferred_element_type=F32)
    bcast = lambda ref, j: jnp.broadcast_to(ref[0, :, lb(j)], (S5_SEGMENTS, LANES))
    ar = [bcast(ar_ref, j) for j in range(nlb)]
    ai = [bcast(ai_ref, j) for j in range(nlb)]
    chains = [(bb, j) for bb in streams for j in range(nlb)]
    n_ch = len(chains)

    def advance(i, c, store):
        nr, ni = [], []
        for q, (bb, j) in enumerate(chains):
            hr, hi = c[q], c[n_ch + q]
            r = ar[j] * hr - ai[j] * hi + h[rows(bb, i), re(j)]
            m = ar[j] * hi + ai[j] * hr + h[rows(bb, i), im(j)]
            if store:
                h[rows(bb, i), re(j)] = r
                h[rows(bb, i), im(j)] = m
            nr.append(r)
            ni.append(m)
        return tuple(nr + ni)

    zero = jnp.zeros((S5_SEGMENTS, LANES), F32)
    ends = loop(lambda i, c: advance(i, c, False), (zero,) * (2 * n_ch))

    starts_r, starts_i = [], []
    for q, (bb, j) in enumerate(chains):
        er, ei = ends[q], ends[n_ch + q]
        pr, pi = pr_ref[0, :, lb(j)], pi_ref[0, :, lb(j)]
        cr, ci = h0r_ref[bb, 0, :, lb(j)], h0i_ref[bb, 0, :, lb(j)]
        start_r, start_i = [cr], [ci]
        for s in range(S5_SEGMENTS - 1):
            cr, ci = (pr * cr - pi * ci + er[s:s + 1, :], pr * ci + pi * cr + ei[s:s + 1, :])
            start_r.append(cr)
            start_i.append(ci)
        starts_r.append(jnp.concatenate(start_r, axis=0))
        starts_i.append(jnp.concatenate(start_i, axis=0))

    last = loop(lambda i, c: advance(i, c, True), tuple(starts_r + starts_i))
    y = d_ref[...] * u
    for q, (bb, j) in enumerate(chains):
        hr_out[bb, 0, :, lb(j)] = last[q][S5_SEGMENTS - 1:, :]
        hi_out[bb, 0, :, lb(j)] = last[n_ch + q][S5_SEGMENTS - 1:, :]
    y += jnp.dot(h[...].astype(BF16), wc_ref[0], preferred_element_type=F32)
    up[...] = y

    def deinterleave(i, c):
        for bb in streams:
            y_ref[bb, strided(i), :] = up[rows(bb, i), :]
        return c

    loop(deinterleave, 0)


def _s5(proj, prm, layer, h0r, h0i):
    n_b, n_l, _ = proj.shape
    nblk = prm["nblk"]
    cin = prm["wb"].shape[1]
    b0 = layer * nblk
    seg = n_l // S5_SEGMENTS
    nb = n_b if seg <= S5_UNROLL else 1
    assert seg * S5_SEGMENTS == n_l and cin == LANES
    wspec = lambda shape: pl.BlockSpec((1,) + shape, lambda b, s: (b0 + s, 0, 0))
    st_spec = pl.BlockSpec((nb, 1, 1, S5_BLOCK), lambda b, s: (b, s, 0, 0))
    return pl.pallas_call(
        functools.partial(_s5_kernel, seg=seg, nb=nb),
        grid=(n_b // nb, nblk),
        in_specs=[pl.BlockSpec((nb, n_l, cin), lambda b, s: (b, 0, s)),
                  wspec((cin, 2 * S5_BLOCK)),
                  wspec((1, S5_BLOCK)), wspec((1, S5_BLOCK)), wspec((1, S5_BLOCK)), wspec((1, S5_BLOCK)),
                  st_spec, st_spec,
                  wspec((2 * S5_BLOCK, cin)),
                  pl.BlockSpec((1, cin), lambda b, s: (0, b0 + s))],
        out_specs=[pl.BlockSpec((nb, n_l, cin), lambda b, s: (b, 0, s)), st_spec, st_spec],
        out_shape=[jax.ShapeDtypeStruct((n_b, n_l, nblk * cin), F32),
                   jax.ShapeDtypeStruct((n_b, nblk, 1, S5_BLOCK), F32),
                   jax.ShapeDtypeStruct((n_b, nblk, 1, S5_BLOCK), F32)],
        scratch_shapes=[pltpu.VMEM((nb * n_l, 2 * S5_BLOCK), F32), pltpu.VMEM((nb * n_l, LANES), F32),
                        pltpu.VMEM((nb, n_l, LANES), F32)],
        compiler_params=_cparams("parallel", "parallel"),
        name="s5_scan",
    )(proj, prm["wb"], prm["ar"], prm["ai"], *prm["pows"][seg], h0r, h0i, prm["wc"], prm["d"])


def _s5_params(lam_re, lam_im, log_dt, b_re, b_im, c_re, c_im, d_skip, seg_lens):
    depth = lam_re.shape[0]
    merge = lambda t: t.reshape((-1,) + t.shape[2:])
    lam_re, lam_im, log_dt, b_re, b_im, c_re, c_im = map(merge, (lam_re, lam_im, log_dt, b_re, b_im, c_re, c_im))
    n_g, n_p = lam_re.shape
    dt = jnp.exp(log_dt)[:, None]
    e = jnp.exp(lam_re * dt)
    ab_re, ab_im = e * jnp.cos(lam_im * dt), e * jnp.sin(lam_im * dt)
    den = lam_re * lam_re + lam_im * lam_im
    nr, ni = ab_re - 1.0, ab_im
    q_re = (nr * lam_re + ni * lam_im) / den
    q_im = (ni * lam_re - nr * lam_im) / den
    bb_re = q_re[..., None] * b_re - q_im[..., None] * b_im
    bb_im = q_re[..., None] * b_im + q_im[..., None] * b_re
    gpb = S5_BLOCK // n_p
    nblk = n_g // gpb
    eye = jnp.eye(gpb, dtype=F32)

    def pack_b(t):
        t = t.reshape(nblk, gpb, n_p, SSM_GROUP)
        return jnp.einsum("sgpc,gh->sgchp", t, eye).reshape(nblk, gpb * SSM_GROUP, gpb * n_p).astype(BF16)

    def pack_c(t):
        t = t.reshape(nblk, gpb, SSM_GROUP, n_p)
        return jnp.einsum("sgcp,gh->shpgc", t, eye).reshape(nblk, gpb * n_p, gpb * SSM_GROUP).astype(BF16)

    flat = lambda t: t.reshape(nblk, 1, gpb * n_p)
    nlb = gpb * n_p // LANES
    wb = jnp.stack([pack_b(bb_re).reshape(nblk, -1, nlb, LANES), pack_b(bb_im).reshape(nblk, -1, nlb, LANES)], axis=3)
    wc = jnp.stack([pack_c(c_re).reshape(nblk, nlb, LANES, -1), -pack_c(c_im).reshape(nblk, nlb, LANES, -1)], axis=2)
    out = {"wb": wb.reshape(nblk, -1, 2 * gpb * n_p), "wc": wc.reshape(nblk, 2 * gpb * n_p, -1),
           "ar": flat(ab_re), "ai": flat(ab_im), "d": d_skip.reshape(1, -1), "pows": {}, "nblk": nblk // depth}
    for seg in seg_lens:
        assert seg & (seg - 1) == 0
        pr, pi = ab_re, ab_im
        for _ in range(int(math.log2(seg))):
            pr, pi = pr * pr - pi * pi, 2.0 * pr * pi
        out["pows"][seg] = (flat(pr), flat(pi))
    return out


def _att_prompt_kernel(q_ref, k_ref, v_ref, bias_ref, o_ref, kpad, vpad, *, n_l, scale):
    zpad = jnp.zeros((ATT_REACH, LANES), BF16)
    kpad[0:ATT_REACH, :] = zpad
    vpad[0:ATT_REACH, :] = zpad
    kpad[ATT_REACH:, :] = k_ref[0].astype(BF16)
    vpad[ATT_REACH:, :] = v_ref[0].astype(BF16)
    lane = lax.broadcasted_iota(jnp.int32, (1, LANES), 1)
    heads = (lane < HEAD_DIM, lane >= HEAD_DIM)
    col = lax.broadcasted_iota(jnp.int32, (1, K_WINDOW), 1)
    nt = (((1,), (1,)), ((), ()))

    def block(r0, n_masked):
        q = q_ref[0, pl.ds(r0, Q_BLOCK), :] * scale
        kw = kpad[pl.ds(r0, K_WINDOW), :]
        vw = vpad[pl.ds(r0, K_WINDOW), :]
        s = [lax.dot_general(jnp.where(hd, q, 0.0).astype(BF16), kw, nt, preferred_element_type=F32) + bias_ref[0, h]
             for h, hd in enumerate(heads)]
        if n_masked:
            s = [jnp.where(col < n_masked, NEG_INF, x) for x in s]
        m = [jnp.max(x, axis=-1, keepdims=True) for x in s]
        p = [jnp.exp(x - mx) for x, mx in zip(s, m)]
        l = [jnp.sum(x, axis=-1, keepdims=True) for x in p]
        o = [jnp.dot(x.astype(BF16), vw, preferred_element_type=F32) / lx for x, lx in zip(p, l)]
        o_ref[0, pl.ds(r0, Q_BLOCK), :] = jnp.where(heads[0], o[0], o[1]).astype(o_ref.dtype)

    n_blocks = n_l // Q_BLOCK
    n_first = min(n_blocks, ATT_REACH // Q_BLOCK)
    for qb in range(n_first):
        block(qb * Q_BLOCK, ATT_REACH - qb * Q_BLOCK)

    def body(qb, carry):
        block(pl.multiple_of(qb * Q_BLOCK, Q_BLOCK), 0)
        return carry

    lax.fori_loop(n_first, n_blocks, body, 0, unroll=max(1, n_blocks - n_first))


def _att_prompt(proj, bias, layer, q_col, k_col, v_col, n_pairs):
    n_b, n_l, _ = proj.shape
    assert n_l % Q_BLOCK == 0
    p0 = layer * n_pairs
    spec = lambda c0: pl.BlockSpec((1, n_l, LANES), lambda b, p: (b, 0, c0 + p))
    return pl.pallas_call(
        functools.partial(_att_prompt_kernel, n_l=n_l, scale=HEAD_DIM ** -0.5),
        grid=(n_b, n_pairs),
        in_specs=[spec(q_col), spec(k_col), spec(v_col),
                  pl.BlockSpec((1, 2, Q_BLOCK, K_WINDOW), lambda b, p: (p0 + p, 0, 0, 0))],
        out_specs=pl.BlockSpec((1, n_l, LANES), lambda b, p: (b, 0, p)),
        out_shape=jax.ShapeDtypeStruct((n_b, n_l, n_pairs * LANES), BF16),
        scratch_shapes=[pltpu.VMEM((n_l + ATT_REACH, LANES), BF16), pltpu.VMEM((n_l + ATT_REACH, LANES), BF16)],
        compiler_params=_cparams("parallel", "parallel"),
        name="att_prompt",
    )(proj, proj, proj, bias)


def _bias_kernel(g_ref, bp_ref, bs_ref):
    g = g_ref[0]
    n_q, n_k = bp_ref.shape[1], bp_ref.shape[2]
    toep = pltpu.roll(jnp.broadcast_to(g, (n_q, g.shape[1])), 0, 1, stride=1, stride_axis=0)[:, :n_k]
    qc = lax.broadcasted_iota(jnp.int32, (n_q, n_k), 0) // CHUNK
    kc = lax.broadcasted_iota(jnp.int32, (n_q, n_k), 1) // CHUNK
    bp_ref[0] = jnp.where((kc >= qc) & (kc <= qc + LEFT_CHUNKS), toep, NEG_INF)
    n_s, n_ws = bs_ref.shape[1], bs_ref.shape[2]
    bs_ref[0] = pltpu.roll(jnp.broadcast_to(g, (n_s, g.shape[1])), 0, 1, stride=1, stride_axis=0)[:, :n_ws]


def _rel_bias(table, n_s, n_w):
    n_h = table.shape[0]
    assert n_w == ATT_REACH
    width = pl.cdiv(K_WINDOW + Q_BLOCK, LANES) * LANES
    d = np.arange(width)
    d = np.where(d < width - Q_BLOCK, d, d - width)
    idx = np.clip(ATT_REACH - d, -REL_CLIP, REL_CLIP) + REL_CLIP
    diag = table[:, idx].astype(F32).reshape(n_h, 1, width)
    return pl.pallas_call(
        _bias_kernel,
        grid=(n_h,),
        in_specs=[pl.BlockSpec((1, 1, width), lambda h: (h, 0, 0))],
        out_specs=[pl.BlockSpec((1, Q_BLOCK, K_WINDOW), lambda h: (h, 0, 0)),
                   pl.BlockSpec((1, n_s, n_w + n_s), lambda h: (h, 0, 0))],
        out_shape=[jax.ShapeDtypeStruct((n_h, Q_BLOCK, K_WINDOW), F32),
                   jax.ShapeDtypeStruct((n_h, n_s, n_w + n_s), F32)],
        compiler_params=_cparams("parallel"),
        name="rel_bias",
    )(diag)


def _att_step_kernel(q_ref, kn_ref, vn_ref, kc_ref, vc_ref, bc_ref, bn_ref, o_ref, *, n_h, scale):
    n_s = q_ref.shape[1]
    d = q_ref.shape[2]
    q = q_ref[0] * scale
    qt = jnp.concatenate([q] * n_h, axis=0)
    rh = lax.broadcasted_iota(jnp.int32, (n_h * n_s, d), 0) // n_s
    ch = lax.broadcasted_iota(jnp.int32, (n_h * n_s, d), 1) // HEAD_DIM
    qbd = jnp.where(rh == ch, qt, 0.0).astype(BF16)
    nt = (((1,), (1,)), ((), ()))
    s1 = lax.dot_general(qbd, kc_ref[0, 0].astype(BF16), nt, preferred_element_type=F32) + bc_ref[...]
    s2 = lax.dot_general(qbd, kn_ref[0].astype(BF16), nt, preferred_element_type=F32) + bn_ref[...]
    m = jnp.maximum(jnp.max(s1, axis=-1, keepdims=True), jnp.max(s2, axis=-1, keepdims=True))
    p1 = jnp.exp(s1 - m)
    p2 = jnp.exp(s2 - m)
    l = jnp.sum(p1, axis=-1, keepdims=True) + jnp.sum(p2, axis=-1, keepdims=True)
    o = (jnp.dot(p1.astype(BF16), vc_ref[0, 0].astype(BF16), preferred_element_type=F32)
         + jnp.dot(p2.astype(BF16), vn_ref[0].astype(BF16), preferred_element_type=F32)) / l
    first = lax.broadcasted_iota(jnp.int32, (1, LANES), 1) < HEAD_DIM
    for j in range(d // LANES):
        blk = o[:, j * LANES:(j + 1) * LANES]
        lo = blk[(2 * j) * n_s:(2 * j + 1) * n_s, :]
        hi = blk[(2 * j + 1) * n_s:(2 * j + 2) * n_s, :]
        o_ref[0, :, j * LANES:(j + 1) * LANES] = jnp.where(first, lo, hi).astype(o_ref.dtype)


def _att_step(proj, k_cache, v_cache, layer, bias_c, bias_n, q_col, k_col, v_col, n_h):
    n_b, n_s, _ = proj.shape
    n_w, d = k_cache.shape[2], k_cache.shape[3]
    spec = lambda c0: pl.BlockSpec((1, n_s, d), lambda b: (b, 0, c0))
    cspec = pl.BlockSpec((1, 1, n_w, d), lambda b: (layer, b, 0, 0))
    bspec = lambda a: pl.BlockSpec((None,) + a.shape[1:], lambda b: (layer, 0, 0))
    return pl.pallas_call(
        functools.partial(_att_step_kernel, n_h=n_h, scale=HEAD_DIM ** -0.5),
        grid=(n_b,),
        in_specs=[spec(q_col), spec(k_col), spec(v_col), cspec, cspec, bspec(bias_c), bspec(bias_n)],
        out_specs=pl.BlockSpec((1, n_s, d), lambda b: (b, 0, 0)),
        out_shape=jax.ShapeDtypeStruct((n_b, n_s, d), BF16),
        compiler_params=_cparams("parallel"),
        name="att_step",
    )(proj, proj, proj, k_cache, v_cache, bias_c, bias_n)


def _rwkv_prep_kernel(r_ref, k_ref, v_ref, u_ref, sh_ref, mu_ref, w0_ref, w1_ref, w2_ref, a0_ref, a1_ref, a2_ref,
                      kk_ref, ka_ref, nkk_out, w_out, b_out, k_out, r_out, v_out, prev):
    j = pl.program_id(1)

    @pl.when(j == 0)
    def _():
        prev[...] = sh_ref[0]

    n_r = r_ref.shape[1]
    row0 = lax.broadcasted_iota(jnp.int32, (SUBLANES, 1), 0) == 0

    def delta(x, i):
        sh = pltpu.roll(x, 1, axis=0)
        top = jnp.where(row0, prev[i:i + 1, :], sh[:SUBLANES])
        sh = jnp.concatenate([top, sh[SUBLANES:]], axis=0) if n_r > SUBLANES else top
        prev[i:i + 1, :] = x[n_r - 1:, :]
        return sh - x

    rp, kp, vp, up = (ref[0].astype(F32) for ref in (r_ref, k_ref, v_ref, u_ref))
    du = delta(up, 3)
    r = rp + delta(rp, 0) * mu_ref[0:1, :]
    k = kp + delta(kp, 1) * mu_ref[1:2, :]
    v = vp + delta(vp, 2) * mu_ref[2:3, :]
    xw = up + du * mu_ref[3:4, :]
    xa = up + du * mu_ref[4:5, :]
    dot = lambda a, b: jnp.dot(a.astype(BF16), b, preferred_element_type=F32)
    z = -(w0_ref[...] + dot(jnp.tanh(dot(xw, w1_ref[...])), w2_ref[...]))
    softplus = jnp.maximum(z, 0.0) + jnp.log(1.0 + jnp.exp(-jnp.abs(z)))
    w = -softplus - 0.5
    a = jax.nn.sigmoid(a0_ref[...] + dot(dot(xa, a1_ref[...]), a2_ref[...]))
    kk = k * kk_ref[...]
    ss = _head_sum(kk * kk, _block_ones(kk.shape[1]))
    kk = kk * lax.rsqrt(jnp.maximum(ss, 1e-24))
    nkk_out[0] = (-kk).astype(nkk_out.dtype)
    w_out[0] = -jnp.exp(w)
    b_out[0] = (kk * a).astype(b_out.dtype)
    k_out[0] = (k * (1.0 + (a - 1.0) * ka_ref[...])).astype(k_out.dtype)
    r_out[0] = r.astype(r_out.dtype)
    v_out[0] = v.astype(v_out.dtype)


def _layer_spec(a, layer):
    return pl.BlockSpec((None,) + a.shape[1:], lambda *_: (layer,) + (0,) * (a.ndim - 1),
                        pipeline_mode=pl.Buffered(1))


def _rwkv_prep(proj, shift0, lw, col0):
    n_b, n_l, _ = proj.shape
    d = shift0.shape[2]
    tr = min(n_l, PREP_ROWS)
    assert n_l % tr == 0
    spec = lambda c: pl.BlockSpec((1, tr, d), lambda b, j: (b, j, c))
    full = lambda a: _layer_spec(a, lw["layer"])
    ws = [lw["rwkv_mu"], lw["rwkv_w0"], lw["rwkv_w1"], lw["rwkv_w2"], lw["rwkv_a0"], lw["rwkv_a1"], lw["rwkv_a2"],
          lw["rwkv_k_k"], lw["rwkv_k_a"]]
    ospec = pl.BlockSpec((1, tr, d), lambda b, j: (b, j, 0))
    return pl.pallas_call(
        _rwkv_prep_kernel,
        grid=(n_b, n_l // tr),
        in_specs=[spec(col0), spec(col0 + 1), spec(col0 + 2), spec(col0 + 3),
                  pl.BlockSpec((1, 4, d), lambda b, j: (b, 0, 0))] + [full(a) for a in ws],
        out_specs=[ospec] * 6,
        out_shape=[jax.ShapeDtypeStruct((n_b, n_l, d), F32 if i == 1 else BF16) for i in range(6)],
        scratch_shapes=[pltpu.VMEM((4, d), F32)],
        compiler_params=_cparams("parallel", "arbitrary"),
        name="rwkv_prep",
    )(proj, proj, proj, proj, shift0, *ws)


def _rwkv_rec_kernel(nkk_ref, lw_ref, b_ref, k_ref, r_ref, v_ref, s0_ref, rk_ref, g_ref, beta_ref,
                     y_ref, s_out, state, *, nbk, nch, c):
    tb = pl.program_id(2)

    @pl.when(tb == 0)
    def _():
        state[...] = s0_ref[:, 0]

    c2 = 2 * c
    lane = lax.broadcasted_iota(jnp.int32, (1, LANES), 1)
    m0 = (lane < HEAD_DIM).astype(F32)
    m1 = 1.0 - m0
    by_head = lambda x: jnp.concatenate([x * m0, x * m1], axis=0)
    mm = lambda a, b: jnp.dot(a.astype(BF16), b.astype(BF16), preferred_element_type=F32)
    mm_nt = lambda a, b: lax.dot_general(a.astype(BF16), b.astype(BF16), (((1,), (1,)), ((), ())),
                                         preferred_element_type=F32)
    ii = lambda shape, d: lax.broadcasted_iota(jnp.int32, shape, d)
    tri = (ii((c, c), 0) >= ii((c, c), 1)).astype(BF16)
    rr, cc = ii((c2, c2), 0), ii((c2, c2), 1)
    strict = (rr // c == cc // c) & (cc % c < rr % c)
    incl = ii((c, c2), 1) % c <= ii((c, c2), 0)
    eye_t = (rr == cc).astype(F32)
    eye_s = (ii((LANES, LANES), 0) == ii((LANES, LANES), 1)).astype(F32)
    probs = [(bb, ci) for ci in range(nch) for bb in range(nbk)]
    rows = lambda ref, q: ref[q[0], q[1] * c:(q[1] + 1) * c, :].astype(F32)
    each = lambda fn, *lists: [fn(*xs) for xs in zip(*lists)]

    lw = [rows(lw_ref, q) for q in probs]

    def cumsum(x):
        hi = x.astype(BF16)
        lo = (x - hi.astype(F32)).astype(BF16)
        return jnp.dot(tri, hi, preferred_element_type=F32) + jnp.dot(tri, lo, preferred_element_type=F32)

    cum = each(cumsum, lw)
    tot = [x[c - 1:c, :] for x in cum]
    g_c = [jnp.exp(x) for x in tot]
    a_t = each(lambda q, x, l: rows(nkk_ref, q) * jnp.exp(x - l), probs, cum, lw)
    r_t = each(lambda q, x: rows(r_ref, q) * jnp.exp(x), probs, cum)
    ginv = [jnp.exp(-x) for x in cum]
    ghat = each(lambda x, t: jnp.exp(t - x), cum, tot)
    kx = [rows(k_ref, q) for q in probs]
    bx = [rows(b_ref, q) for q in probs]
    a_bd = [by_head(x) for x in a_t]
    lhs = each(lambda a, r: jnp.concatenate([a, r], axis=0), a_bd, r_t)
    if c2 % LANES == 0:
        sc = each(lambda l, k, b, gi: mm_nt(l, jnp.concatenate([by_head(k * gi), by_head(b * gi)], axis=0)),
                  lhs, kx, bx, ginv)
        sc_k, sc_b = [x[:, :c2] for x in sc], [x[:, c2:] for x in sc]
    else:
        sc_k = each(lambda l, k, gi: mm_nt(l, by_head(k * gi)), lhs, kx, ginv)
        sc_b = each(lambda l, b, gi: mm_nt(l, by_head(b * gi)), lhs, bx, ginv)
    l_ak = [jnp.where(strict, x[:c2], 0.0) for x in sc_k]
    l_ab = [jnp.where(strict, x[:c2], 0.0) for x in sc_b]
    m_rk = [jnp.where(incl, x[c2:], 0.0) for x in sc_k]
    m_rb = [jnp.where(incl, x[c2:], 0.0) for x in sc_b]
    col_t = ii((1, c2), 1)
    t0, t1 = (col_t < c).astype(F32), (col_t >= c).astype(F32)
    blockdiag = lambda x: jnp.concatenate([x * t0, x * t1], axis=0)
    pw = [x[:c] + x[c:] for x in l_ab]
    inv = [x + eye_t[:c] + eye_t[c:] for x in pw]
    for _ in range(int(math.log2(c)) - 1):
        pw = each(lambda x: mm(x, blockdiag(x)), pw)
        inv = each(lambda p, x: p + mm(p, blockdiag(x)), inv, pw)
    v_bd = [by_head(rows(v_ref, q)) for q in probs]
    x_in = each(lambda a, l, v: jnp.concatenate([a, mm(l, v)], axis=1), a_bd, l_ak, v_bd)
    m0w, m1w = jnp.concatenate([m0, m0], axis=1), jnp.concatenate([m1, m1], axis=1)
    uu = each(lambda t, x: (lambda u: jnp.concatenate([u * m0w, u * m1w], axis=0))(mm(t, x)), inv, x_in)
    bh_t = each(lambda b, gh: by_head(b * gh).T, bx, ghat)
    kh_t = each(lambda k, gh: by_head(k * gh).T, kx, ghat)
    bu = each(mm, bh_t, uu)
    g_mat = each(lambda gc, x: eye_s * gc + x[:, :LANES], g_c, bu)
    h_mat = each(lambda kt, v, x: mm(kt, v) + x[:, LANES:], kh_t, v_bd, bu)
    mu = each(mm, m_rb, uu)
    y_a = each(lambda r, x: r + x[:, :LANES], r_t, mu)
    y_b = each(lambda mk, v, x: mm(mk, v) + x[:, LANES:], m_rk, v_bd, mu)

    s_cur = [state[bb] for bb in range(nbk)]
    ys = {}
    for i, (bb, ci) in enumerate(probs):
        both = mm(jnp.concatenate([y_a[i], g_mat[i]], axis=0), s_cur[bb])
        ys[(bb, ci)] = both[:c] + y_b[i]
        s_cur[bb] = both[c:] + h_mat[i]
    for bb in range(nbk):
        state[bb] = s_cur[bb]

    @pl.when(tb == pl.num_programs(2) - 1)
    def _():
        s_out[:, 0] = state[...]

    ones = _block_ones(LANES)
    inv_n = 1.0 / HEAD_DIM
    head_sum = lambda x: jnp.dot(x.astype(BF16), ones, preferred_element_type=F32)
    for bb in range(nbk):
        y = jnp.concatenate([ys[(bb, ci)] for ci in range(nch)], axis=0) if nch > 1 else ys[(bb, 0)]
        mean = head_sum(y) * inv_n
        yc = y - mean
        var = head_sum(yc * yc) * inv_n
        yn = yc * lax.rsqrt(var + GN_EPS) * g_ref[...] + beta_ref[...]
        f32 = lambda ref: ref[bb].astype(F32)
        bonus = head_sum(f32(r_ref) * f32(k_ref) * rk_ref[...]) * f32(v_ref)
        y_ref[bb] = (yn + bonus).astype(y_ref.dtype)


def _rwkv_rec(prep, s0, layer, r_k, lnx_g, lnx_b):
    n_b, n_l, d = prep[0].shape
    n_pairs = d // LANES
    c = min(n_l, RWKV_CHUNK)
    nch = min(n_l // c, RWKV_GROUP // 2)
    nbk = min(n_b, RWKV_GROUP // nch)
    lb = nch * c
    assert n_l % lb == 0 and n_b % nbk == 0 and c & (c - 1) == 0
    spec = pl.BlockSpec((nbk, lb, LANES), lambda g, p, t: (g, t, p))
    sspec = pl.BlockSpec((nbk, 1, LANES, LANES), lambda g, p, t: (g, p, 0, 0))
    vec = pl.BlockSpec((None, 1, LANES), lambda g, p, t: (layer, 0, p))
    return pl.pallas_call(
        functools.partial(_rwkv_rec_kernel, nbk=nbk, nch=nch, c=c),
        grid=(n_b // nbk, n_pairs, n_l // lb),
        in_specs=[spec] * 6 + [sspec, vec, vec, vec],
        out_specs=[spec, sspec],
        out_shape=[jax.ShapeDtypeStruct((n_b, n_l, d), BF16),
                   jax.ShapeDtypeStruct((n_b, n_pairs, LANES, LANES), F32)],
        scratch_shapes=[pltpu.VMEM((nbk, LANES, LANES), F32)],
        compiler_params=_cparams("parallel", "parallel", "arbitrary"),
        name="rwkv_rec",
    )(*prep, s0, r_k, lnx_g, lnx_b)


def _pack_pairs(s):
    n_b, n_h, n, _ = s.shape
    st = jnp.swapaxes(s, -1, -2).reshape(n_b, n_h // 2, 2, n, n)
    z = jnp.zeros_like(st[:, :, 0])
    return jnp.concatenate([jnp.concatenate([st[:, :, 0], z], axis=-1),
                            jnp.concatenate([z, st[:, :, 1]], axis=-1)], axis=-2)


def _unpack_pairs(s):
    n_b, n_p, n2, _ = s.shape
    n = n2 // 2
    st = jnp.stack([s[:, :, :n, :n], s[:, :, n:, n:]], axis=2)
    return jnp.swapaxes(st, -1, -2).reshape(n_b, 2 * n_p, n, n)


N_MIX_IN = 7


def _mix_kernel(*refs, alpha, d_ssm, d_att, n_out):
    main, side = refs[:N_MIX_IN], refs[N_MIX_IN:2 * N_MIX_IN]
    wg_ref, bg_ref, wo_ref, lg_ref, lb_ref = refs[2 * N_MIX_IN:2 * N_MIX_IN + 5]
    outs = refs[2 * N_MIX_IN + 5:]
    silu = lambda g: g * jax.nn.sigmoid(g)
    dot = lambda a, b: jnp.dot(a.astype(BF16), b, preferred_element_type=F32)

    def mix_rows(ins, out_refs):
        x_ref, ys_ref, gs_ref, ya_ref, ga_ref, yc_ref, gc_ref = ins
        n_r = x_ref.shape[0]
        n_part = 2 if n_r % 32 == 0 else 1
        for part in range(n_part):
            rows = slice(part * n_r // n_part, (part + 1) * n_r // n_part)
            ys = ys_ref[rows, :]
            z = 0.5 * ys * (1.0 + jnp.tanh(math.sqrt(2.0 / math.pi) * (ys + 0.044715 * (ys * ys * ys))))
            gate = lambda ref: silu(ref[rows, :].astype(F32))
            m_s = z * jax.nn.sigmoid(dot(z, wg_ref[...]) + bg_ref[...]) * gate(gs_ref)
            m_a = ya_ref[rows, :] * gate(ga_ref)
            m_c = yc_ref[rows, :] * gate(gc_ref)
            out = (dot(m_s, wo_ref[0:d_ssm, :]) + dot(m_a, wo_ref[d_ssm:d_ssm + d_att, :])
                   + dot(m_c, wo_ref[d_ssm + d_att:, :]))
            h = alpha * x_ref[rows, :] + out
            mu = jnp.mean(h, axis=-1, keepdims=True)
            hc = h - mu
            var = jnp.mean(hc * hc, axis=-1, keepdims=True)
            y = hc * lax.rsqrt(var + LN_EPS) * lg_ref[...] + lb_ref[...]
            out_refs[0][rows, :] = y
            for yb_ref in out_refs[1:]:
                yb_ref[rows, :] = y.astype(BF16)

    mix_rows(main, outs[:n_out])

    @pl.when(pl.program_id(0) == pl.num_programs(0) - 1)
    def _():
        mix_rows(side, outs[n_out:])


def _mix(main, side, lw, alpha, gs_col, ga_col, gc_col, want_bf16):
    x, proj, ys, ya, yc = main
    m, d = x.shape
    ms = side[0].shape[0]
    d_ssm, d_att, d_rw = ys.shape[1], ya.shape[1], yc.shape[1]
    tm = min(m, MIX_ROWS)
    assert m % tm == 0
    row = lambda w, c: pl.BlockSpec((tm, w), lambda i: (i, c))
    fixed = lambda w, c: pl.BlockSpec((ms, w), lambda i: (0, c))
    specs = lambda blk: [blk(d, 0), blk(d_ssm, 0), blk(d_ssm, gs_col), blk(d_att, 0), blk(d_att, ga_col),
                         blk(d_rw, 0), blk(d_rw, gc_col)]
    operands = lambda t: (t[0], t[2], t[1], t[3], t[1], t[4], t[1])
    full = lambda a: _layer_spec(a, lw["layer"])
    ws = [lw["w_glu"], lw["b_glu"], lw["w_out"], lw["ln_g"], lw["ln_b"]]
    n_out = 2 if want_bf16 else 1
    shapes = lambda rows: [jax.ShapeDtypeStruct((rows, d), F32)] + [jax.ShapeDtypeStruct((rows, d), BF16)] * want_bf16
    return pl.pallas_call(
        functools.partial(_mix_kernel, alpha=alpha, d_ssm=d_ssm, d_att=d_att, n_out=n_out),
        grid=(m // tm,),
        in_specs=specs(row) + specs(fixed) + [full(a) for a in ws],
        out_specs=[row(d, 0)] * n_out + [fixed(d, 0)] * n_out,
        out_shape=shapes(m) + shapes(ms),
        compiler_params=_cparams("arbitrary"),
        name="mix_out",
    )(*operands(main), *operands(side), *ws)


def _layer(x, proj, st, lw, s5p):
    n_b, n_l, d_model = x.shape
    d_ssm = lw["w_glu"].shape[1]
    n_h = lw["n_att_heads"]
    d_att = n_h * HEAD_DIM
    d_rw = lw["rwkv_w0"].shape[2]
    n_rh = d_rw // HEAD_DIM
    n_blk = d_ssm // LANES
    proj = proj.reshape(n_b, n_l, -1)
    q0 = 2 * d_ssm
    r0 = q0 + 4 * d_att
    assert q0 % d_att == 0 and r0 % d_rw == 0 and d_ssm == d_rw

    if st is None:
        zeros = jnp.zeros((n_b, n_blk, 1, S5_BLOCK), F32)
        h0r = h0i = zeros
        wkv0 = jnp.zeros((n_b, n_rh // 2, LANES, LANES), F32)
        shift0 = jnp.zeros((n_b, 4, d_rw), F32)
    else:
        k_cache, v_cache, h0r, h0i, wkv0, shift0 = st
        h0r = h0r.reshape(n_b, n_blk, 1, S5_BLOCK)
        h0i = h0i.reshape(n_b, n_blk, 1, S5_BLOCK)
        wkv0 = _pack_pairs(wkv0)
        shift0 = shift0.reshape(n_b, 4, d_rw)

    ys, h_re, h_im = _s5(proj, s5p, lw["layer"], h0r, h0i)

    n_keep = min(ATT_REACH, n_l)
    keep = lambda c0: proj[:, n_l - n_keep:, c0:c0 + d_att].astype(F32).reshape(n_b, n_keep, n_h, HEAD_DIM)
    k_rows, v_rows = keep(q0 + d_att), keep(q0 + 2 * d_att)
    if st is None:
        pl0 = q0 // LANES
        ya = _att_prompt(proj, lw["bias_prompt"], lw["layer"], pl0, pl0 + d_att // LANES, pl0 + 2 * d_att // LANES,
                         n_h // 2)
    else:
        c0 = q0 // d_att
        ya = _att_step(proj, k_cache, v_cache, lw["layer"], lw["bias_cache"], lw["bias_new"], c0, c0 + 1, c0 + 2, n_h)

    prep = _rwkv_prep(proj, shift0, lw, r0 // d_rw)
    yc, wkv = _rwkv_rec(prep, wkv0, lw["layer"], lw["rwkv_r_k"], lw["rwkv_lnx_g"], lw["rwkv_lnx_b"])
    shift = proj[:, n_l - 1, r0:r0 + 4 * d_rw].astype(F32)

    m = n_b * n_l
    mix_in = (x.reshape(m, d_model), proj.reshape(m, -1), ys.reshape(m, d_ssm), ya.reshape(m, d_att),
              yc.reshape(m, d_rw))
    gate_cols = (1, (q0 + 3 * d_att) // d_att, (r0 + 4 * d_rw) // d_rw)
    n_g = d_ssm // SSM_GROUP
    states = (k_rows, v_rows, h_re.reshape(n_b, n_g, SSM_STATE), h_im.reshape(n_b, n_g, SSM_STATE),
              _unpack_pairs(wkv), shift)
    return mix_in, gate_cols, states


def kernel(x_prompt, x_sample, cache_att_k, cache_att_v, state_ssm_re, state_ssm_im, state_rwkv, state_rwkv_shift, w_in, ssm_lam_re, ssm_lam_im, ssm_log_dt, ssm_b_re, ssm_b_im, ssm_c_re, ssm_c_im, ssm_d, ssm_w_glu, ssm_b_glu, att_rel_bias, rwkv_mu, rwkv_w0, rwkv_w1, rwkv_w2, rwkv_a0, rwkv_a1, rwkv_a2, rwkv_k_k, rwkv_k_a, rwkv_r_k, rwkv_lnx_g, rwkv_lnx_b, w_out, ln_g, ln_b):
    depth = w_in.shape[0]
    alpha = (2.0 * depth) ** 0.25
    y_p, y_s = x_prompt, x_sample
    yb_p = yb_s = None
    seg_lens = sorted({x_prompt.shape[1] // S5_SEGMENTS, x_sample.shape[1] // S5_SEGMENTS})
    p_st, s_st = [], []
    n_d, n_sb, n_w = cache_att_k.shape[:3]
    n_s = x_sample.shape[1]
    n_h = att_rel_bias.shape[1]
    k_cache = cache_att_k.reshape(n_d, n_sb, n_w, -1)
    v_cache = cache_att_v.reshape(n_d, n_sb, n_w, -1)
    row = lambda a: a.reshape(depth, 1, -1)
    bias_p, bias_s = _rel_bias(att_rel_bias.reshape(depth * n_h, -1), n_s, n_w)
    bias_s = bias_s.reshape(depth, n_h * n_s, n_w + n_s)
    params = {"w_in": w_in, "n_att_heads": n_h, "w_glu": ssm_w_glu.astype(BF16), "b_glu": row(ssm_b_glu),
              "rwkv_mu": rwkv_mu, "rwkv_w0": row(rwkv_w0), "rwkv_w1": rwkv_w1.astype(BF16),
              "rwkv_w2": rwkv_w2.astype(BF16), "rwkv_a0": row(rwkv_a0), "rwkv_a1": rwkv_a1.astype(BF16),
              "rwkv_a2": rwkv_a2.astype(BF16), "rwkv_k_k": row(rwkv_k_k), "rwkv_k_a": row(rwkv_k_a),
              "rwkv_r_k": row(rwkv_r_k), "rwkv_lnx_g": row(rwkv_lnx_g), "rwkv_lnx_b": row(rwkv_lnx_b),
              "w_out": w_out.astype(BF16), "ln_g": row(ln_g), "ln_b": row(ln_b),
              "bias_prompt": bias_p.reshape(depth * n_h // 2, 2, Q_BLOCK, K_WINDOW),
              "bias_cache": bias_s[:, :, :n_w], "bias_new": bias_s[:, :, n_w:]}
    s5p = _s5_params(ssm_lam_re, ssm_lam_im, ssm_log_dt, ssm_b_re, ssm_b_im, ssm_c_re, ssm_c_im, ssm_d, seg_lens)
    for l in range(depth):
        lw = dict(params, layer=l)
        if yb_p is None:
            flat = lambda a: a.reshape(-1, a.shape[-1])
            yb_p, yb_s = _to_bf16(flat(y_p)), _to_bf16(flat(y_s))
        proj_p, proj_s = _in_proj(yb_p, yb_s, w_in, l)
        mix_p, gate_cols, st_p = _layer(y_p, proj_p, None, lw, s5p)
        mix_s, _, st_s = _layer(y_s, proj_s, (k_cache, v_cache, state_ssm_re[l], state_ssm_im[l],
                                              state_rwkv[l], state_rwkv_shift[l]), lw, s5p)
        last = l == depth - 1
        outs = _mix(mix_p, mix_s, lw, alpha, *gate_cols, want_bf16=not last)
        (y_p, y_s), (yb_p, yb_s) = (outs if last else outs[0::2]), ((None, None) if last else outs[1::2])
        y_p, y_s = y_p.reshape(x_prompt.shape), y_s.reshape(x_sample.shape)
        p_st.append(st_p)
        s_st.append(st_s)
    stacked = lambda states, i: jnp.stack([st[i] for st in states], axis=0)
    return (y_p, y_s) + tuple(stacked(p_st, i) for i in range(6)) + tuple(stacked(s_st, i) for i in range(6))
```

```python
import functools
import math

import jax
import jax.numpy as jnp
import numpy as np
from jax import lax
from jax.experimental import pallas as pl
from jax.experimental.pallas import tpu as pltpu

F32 = jnp.float32
BF16 = jnp.bfloat16

LANES = 128
SUBLANES = 8
VMEM_LIMIT = 56 * 1024 * 1024

CHUNK = 64
LEFT_CHUNKS = 8
ATT_REACH = LEFT_CHUNKS * CHUNK
REL_CLIP = 128
HEAD_DIM = 64
SSM_GROUP = 16
SSM_STATE = 64
NEG_INF = -1e30
GN_EPS = 64e-5
LN_EPS = 1e-5

PROJ_TILE = (1024, 1536)
CAST_ROWS = 1024
PREP_ROWS = 512
MIX_ROWS = 512
S5_UNROLL = 8
RWKV_CHUNK = 64
RWKV_GROUP = 16
Q_BLOCK = 4 * CHUNK
K_WINDOW = Q_BLOCK + ATT_REACH
S5_SEGMENTS = SUBLANES
S5_BLOCK = 512


def _cparams(*sem):
    return pltpu.CompilerParams(dimension_semantics=sem, vmem_limit_bytes=VMEM_LIMIT)


def _block_ones(n):
    r = lax.broadcasted_iota(jnp.int32, (n, n), 0) // HEAD_DIM
    c = lax.broadcasted_iota(jnp.int32, (n, n), 1) // HEAD_DIM
    return (r == c).astype(BF16)


def _sigmoid(x):
    return 0.5 * (1.0 + jnp.tanh(0.5 * x))


def _matmul_kernel(x_ref, xs_ref, w_ref, o_ref, os_ref, wb):
    @pl.when(pl.program_id(1) == 0)
    def _():
        wb[...] = w_ref[0].astype(BF16)

    o_ref[...] = jnp.dot(x_ref[...], wb[...], preferred_element_type=F32).astype(o_ref.dtype)

    @pl.when(pl.program_id(1) == pl.num_programs(1) - 1)
    def _():
        os_ref[...] = jnp.dot(xs_ref[...], wb[...], preferred_element_type=F32).astype(os_ref.dtype)


def _cast_kernel(x_ref, o_ref):
    o_ref[...] = x_ref[...].astype(o_ref.dtype)


def _to_bf16(x):
    m, k = x.shape
    tm = min(m, CAST_ROWS)
    assert m % tm == 0
    return pl.pallas_call(
        _cast_kernel,
        grid=(m // tm,),
        in_specs=[pl.BlockSpec((tm, k), lambda i: (i, 0))],
        out_specs=pl.BlockSpec((tm, k), lambda i: (i, 0)),
        out_shape=jax.ShapeDtypeStruct((m, k), BF16),
        compiler_params=_cparams("parallel"),
        name="to_bf16",
    )(x)


def _in_proj(x, xs, w, layer):
    m, k = x.shape
    ms = xs.shape[0]
    n = w.shape[2]
    tm, tn = min(m, PROJ_TILE[0]), PROJ_TILE[1]
    assert m % tm == 0 and n % tn == 0 and x.dtype == BF16 and xs.dtype == BF16
    return pl.pallas_call(
        _matmul_kernel,
        grid=(n // tn, m // tm),
        in_specs=[pl.BlockSpec((tm, k), lambda j, i: (i, 0)),
                  pl.BlockSpec((ms, k), lambda j, i: (0, 0)),
                  pl.BlockSpec((1, k, tn), lambda j, i: (layer, 0, j))],
        out_specs=[pl.BlockSpec((tm, tn), lambda j, i: (i, j)),
                   pl.BlockSpec((ms, tn), lambda j, i: (0, j))],
        out_shape=[jax.ShapeDtypeStruct((m, n), BF16), jax.ShapeDtypeStruct((ms, n), BF16)],
        scratch_shapes=[pltpu.VMEM((k, tn), BF16)],
        compiler_params=_cparams("parallel", "arbitrary"),
        name="in_proj",
    )(x, xs, w)


def _s5_kernel(u_ref, wb_ref, ar_ref, ai_ref, pr_ref, pi_ref, h0r_ref, h0i_ref,
               wc_ref, d_ref, y_ref, hr_out, hi_out, h, up, uf, *, seg, nb):
    nlb = h.shape[1] // (2 * LANES)
    n_l = seg * S5_SEGMENTS
    lb = lambda j: slice(j * LANES, (j + 1) * LANES)
    re = lambda j: slice(2 * j * LANES, (2 * j + 1) * LANES)
    im = lambda j: slice((2 * j + 1) * LANES, (2 * j + 2) * LANES)
    aligned = lambda r: r if isinstance(r, int) else pl.multiple_of(r, S5_SEGMENTS)
    rows = lambda bb, i: pl.ds(aligned(bb * n_l + i * S5_SEGMENTS), S5_SEGMENTS)
    strided = lambda i: pl.ds(i, S5_SEGMENTS, stride=seg)
    streams = range(nb)

    def loop(body, init):
        if seg <= S5_UNROLL:
            for i in range(seg):
                init = body(i, init)
            return init
        return lax.fori_loop(0, seg, body, init, unroll=S5_UNROLL)

    uf[...] = u_ref[...].astype(F32)

    def interleave(i, c):
        for bb in streams:
            up[rows(bb, i), :] = uf[bb, strided(i), :]
        return c

    loop(interleave, 0)
    u = up[...]
    ub = u.astype(BF16)
    h[...] = jnp.dot(ub, wb_ref[0], preferred_element_type=F32)
    bcast = lambda ref, j: jnp.broadcast_to(ref[0, :, lb(j)], (S5_SEGMENTS, LANES))
    ar = [bcast(ar_ref, j) for j in range(nlb)]
    ai = [bcast(ai_ref, j) for j in range(nlb)]
    chains = [(bb, j) for bb in streams for j in range(nlb)]
    n_ch = len(chains)

    def advance(i, c, store):
        nr, ni = [], []
        for q, (bb, j) in enumerate(chains):
            hr, hi = c[q], c[n_ch + q]
            r = ar[j] * hr - ai[j] * hi + h[rows(bb, i), re(j)]
            m = ar[j] * hi + ai[j] * hr + h[rows(bb, i), im(j)]
            if store:
                h[rows(bb, i), re(j)] = r
                h[rows(bb, i), im(j)] = m
            nr.append(r)
            ni.append(m)
        return tuple(nr + ni)

    zero = jnp.zeros((S5_SEGMENTS, LANES), F32)
    ends = loop(lambda i, c: advance(i, c, False), (zero,) * (2 * n_ch))

    starts_r, starts_i = [], []
    for q, (bb, j) in enumerate(chains):
        er, ei = ends[q], ends[n_ch + q]
        pr, pi = pr_ref[0, :, lb(j)], pi_ref[0, :, lb(j)]
        cr, ci = h0r_ref[bb, 0, :, lb(j)], h0i_ref[bb, 0, :, lb(j)]
        start_r, start_i = [cr], [ci]
        for s in range(S5_SEGMENTS - 1):
            cr, ci = (pr * cr - pi * ci + er[s:s + 1, :], pr * ci + pi * cr + ei[s:s + 1, :])
            start_r.append(cr)
            start_i.append(ci)
        starts_r.append(jnp.concatenate(start_r, axis=0))
        starts_i.append(jnp.concatenate(start_i, axis=0))

    last = loop(lambda i, c: advance(i, c, True), tuple(starts_r + starts_i))
    y = d_ref[...] * u
    for q, (bb, j) in enumerate(chains):
        hr_out[bb, 0, :, lb(j)] = last[q][S5_SEGMENTS - 1:, :]
        hi_out[bb, 0, :, lb(j)] = last[n_ch + q][S5_SEGMENTS - 1:, :]
    y += jnp.dot(h[...].astype(BF16), wc_ref[0], preferred_element_type=F32)
    up[...] = y

    def deinterleave(i, c):
        for bb in streams:
            y_ref[bb, strided(i), :] = up[rows(bb, i), :]
        return c

    loop(deinterleave, 0)


def _s5(proj, prm, layer, h0r, h0i):
    n_b, n_l, _ = proj.shape
    nblk = prm["nblk"]
    cin = prm["wb"].shape[1]
    b0 = layer * nblk
    seg = n_l // S5_SEGMENTS
    nb = n_b if seg <= S5_UNROLL else 1
    assert seg * S5_SEGMENTS == n_l and cin == LANES
    wspec = lambda shape: pl.BlockSpec((1,) + shape, lambda b, s: (b0 + s, 0, 0))
    st_spec = pl.BlockSpec((nb, 1, 1, S5_BLOCK), lambda b, s: (b, s, 0, 0))
    return pl.pallas_call(
        functools.partial(_s5_kernel, seg=seg, nb=nb),
        grid=(n_b // nb, nblk),
        in_specs=[pl.BlockSpec((nb, n_l, cin), lambda b, s: (b, 0, s)),
                  wspec((cin, 2 * S5_BLOCK)),
                  wspec((1, S5_BLOCK)), wspec((1, S5_BLOCK)), wspec((1, S5_BLOCK)), wspec((1, S5_BLOCK)),
                  st_spec, st_spec,
                  wspec((2 * S5_BLOCK, cin)),
                  pl.BlockSpec((1, cin), lambda b, s: (0, b0 + s))],
        out_specs=[pl.BlockSpec((nb, n_l, cin), lambda b, s: (b, 0, s)), st_spec, st_spec],
        out_shape=[jax.ShapeDtypeStruct((n_b, n_l, nblk * cin), F32),
                   jax.ShapeDtypeStruct((n_b, nblk, 1, S5_BLOCK), F32),
                   jax.ShapeDtypeStruct((n_b, nblk, 1, S5_BLOCK), F32)],
        scratch_shapes=[pltpu.VMEM((nb * n_l, 2 * S5_BLOCK), F32), pltpu.VMEM((nb * n_l, LANES), F32),
                        pltpu.VMEM((nb, n_l, LANES), F32)],
        compiler_params=_cparams("parallel", "parallel"),
        name="s5_scan",
    )(proj, prm["wb"], prm["ar"], prm["ai"], *prm["pows"][seg], h0r, h0i, prm["wc"], prm["d"])


def _s5_params(lam_re, lam_im, log_dt, b_re, b_im, c_re, c_im, d_skip, seg_lens):
    depth = lam_re.shape[0]
    merge = lambda t: t.reshape((-1,) + t.shape[2:])
    lam_re, lam_im, log_dt, b_re, b_im, c_re, c_im = map(merge, (lam_re, lam_im, log_dt, b_re, b_im, c_re, c_im))
    n_g, n_p = lam_re.shape
    dt = jnp.exp(log_dt)[:, None]
    e = jnp.exp(lam_re * dt)
    ab_re, ab_im = e * jnp.cos(lam_im * dt), e * jnp.sin(lam_im * dt)
    den = lam_re * lam_re + lam_im * lam_im
    nr, ni = ab_re - 1.0, ab_im
    q_re = (nr * lam_re + ni * lam_im) / den
    q_im = (ni * lam_re - nr * lam_im) / den
    bb_re = q_re[..., None] * b_re - q_im[..., None] * b_im
    bb_im = q_re[..., None] * b_im + q_im[..., None] * b_re
    gpb = S5_BLOCK // n_p
    nblk = n_g // gpb
    eye = jnp.eye(gpb, dtype=F32)

    def pack_b(t):
        t = t.reshape(nblk, gpb, n_p, SSM_GROUP)
        return jnp.einsum("sgpc,gh->sgchp", t, eye).reshape(nblk, gpb * SSM_GROUP, gpb * n_p).astype(BF16)

    def pack_c(t):
        t = t.reshape(nblk, gpb, SSM_GROUP, n_p)
        return jnp.einsum("sgcp,gh->shpgc", t, eye).reshape(nblk, gpb * n_p, gpb * SSM_GROUP).astype(BF16)

    flat = lambda t: t.reshape(nblk, 1, gpb * n_p)
    nlb = gpb * n_p // LANES
    wb = jnp.stack([pack_b(bb_re).reshape(nblk, -1, nlb, LANES), pack_b(bb_im).reshape(nblk, -1, nlb, LANES)], axis=3)
    wc = jnp.stack([pack_c(c_re).reshape(nblk, nlb, LANES, -1), -pack_c(c_im).reshape(nblk, nlb, LANES, -1)], axis=2)
    out = {"wb": wb.reshape(nblk, -1, 2 * gpb * n_p), "wc": wc.reshape(nblk, 2 * gpb * n_p, -1),
           "ar": flat(ab_re), "ai": flat(ab_im), "d": d_skip.reshape(1, -1), "pows": {}, "nblk": nblk // depth}
    for seg in seg_lens:
        assert seg & (seg - 1) == 0
        pr, pi = ab_re, ab_im
        for _ in range(int(math.log2(seg))):
            pr, pi = pr * pr - pi * pi, 2.0 * pr * pi
        out["pows"][seg] = (flat(pr), flat(pi))
    return out


def _att_prompt_kernel(q_ref, k_ref, v_ref, bias_ref, o_ref, kpad, vpad, *, n_l, scale):
    zpad = jnp.zeros((ATT_REACH, LANES), BF16)
    kpad[0:ATT_REACH, :] = zpad
    vpad[0:ATT_REACH, :] = zpad
    kpad[ATT_REACH:, :] = k_ref[0].astype(BF16)
    vpad[ATT_REACH:, :] = v_ref[0].astype(BF16)
    lane = lax.broadcasted_iota(jnp.int32, (1, LANES), 1)
    heads = (lane < HEAD_DIM, lane >= HEAD_DIM)
    col = lax.broadcasted_iota(jnp.int32, (1, K_WINDOW), 1)
    nt = (((1,), (1,)), ((), ()))

    def block(r0, n_masked):
        q = q_ref[0, pl.ds(r0, Q_BLOCK), :] * scale
        kw = kpad[pl.ds(r0, K_WINDOW), :]
        vw = vpad[pl.ds(r0, K_WINDOW), :]
        s = [lax.dot_general(jnp.where(hd, q, 0.0).astype(BF16), kw, nt, preferred_element_type=F32) + bias_ref[0, h]
             for h, hd in enumerate(heads)]
        if n_masked:
            s = [jnp.where(col < n_masked, NEG_INF, x) for x in s]
        m = [jnp.max(x, axis=-1, keepdims=True) for x in s]
        p = [jnp.exp(x - mx) for x, mx in zip(s, m)]
        l = [jnp.sum(x, axis=-1, keepdims=True) for x in p]
        o = [jnp.dot(x.astype(BF16), vw, preferred_element_type=F32) / lx for x, lx in zip(p, l)]
        o_ref[0, pl.ds(r0, Q_BLOCK), :] = jnp.where(heads[0], o[0], o[1]).astype(o_ref.dtype)

    n_blocks = n_l // Q_BLOCK
    n_first = min(n_blocks, ATT_REACH // Q_BLOCK)
    for qb in range(n_first):
        block(qb * Q_BLOCK, ATT_REACH - qb * Q_BLOCK)

    def body(qb, carry):
        block(pl.multiple_of(qb * Q_BLOCK, Q_BLOCK), 0)
        return carry

    lax.fori_loop(n_first, n_blocks, body, 0, unroll=max(1, n_blocks - n_first))


def _att_prompt(proj, bias, layer, q_col, k_col, v_col, n_pairs):
    n_b, n_l, _ = proj.shape
    assert n_l % Q_BLOCK == 0
    p0 = layer * n_pairs
    spec = lambda c0: pl.BlockSpec((1, n_l, LANES), lambda b, p: (b, 0, c0 + p))
    return pl.pallas_call(
        functools.partial(_att_prompt_kernel, n_l=n_l, scale=HEAD_DIM ** -0.5),
        grid=(n_b, n_pairs),
        in_specs=[spec(q_col), spec(k_col), spec(v_col),
                  pl.BlockSpec((1, 2, Q_BLOCK, K_WINDOW), lambda b, p: (p0 + p, 0, 0, 0))],
        out_specs=pl.BlockSpec((1, n_l, LANES), lambda b, p: (b, 0, p)),
        out_shape=jax.ShapeDtypeStruct((n_b, n_l, n_pairs * LANES), BF16),
        scratch_shapes=[pltpu.VMEM((n_l + ATT_REACH, LANES), BF16), pltpu.VMEM((n_l + ATT_REACH, LANES), BF16)],
        compiler_params=_cparams("parallel", "parallel"),
        name="att_prompt",
    )(proj, proj, proj, bias)


def _bias_kernel(g_ref, bp_ref, bs_ref):
    g = g_ref[0]
    n_q, n_k = bp_ref.shape[1], bp_ref.shape[2]
    toep = pltpu.roll(jnp.broadcast_to(g, (n_q, g.shape[1])), 0, 1, stride=1, stride_axis=0)[:, :n_k]
    qc = lax.broadcasted_iota(jnp.int32, (n_q, n_k), 0) // CHUNK
    kc = lax.broadcasted_iota(jnp.int32, (n_q, n_k), 1) // CHUNK
    bp_ref[0] = jnp.where((kc >= qc) & (kc <= qc + LEFT_CHUNKS), toep, NEG_INF)
    n_s, n_ws = bs_ref.shape[1], bs_ref.shape[2]
    bs_ref[0] = pltpu.roll(jnp.broadcast_to(g, (n_s, g.shape[1])), 0, 1, stride=1, stride_axis=0)[:, :n_ws]


def _rel_bias(table, n_s, n_w):
    n_h = table.shape[0]
    assert n_w == ATT_REACH
    width = pl.cdiv(K_WINDOW + Q_BLOCK, LANES) * LANES
    d = np.arange(width)
    d = np.where(d < width - Q_BLOCK, d, d - width)
    idx = np.clip(ATT_REACH - d, -REL_CLIP, REL_CLIP) + REL_CLIP
    diag = table[:, idx].astype(F32).reshape(n_h, 1, width)
    return pl.pallas_call(
        _bias_kernel,
        grid=(n_h,),
        in_specs=[pl.BlockSpec((1, 1, width), lambda h: (h, 0, 0))],
        out_specs=[pl.BlockSpec((1, Q_BLOCK, K_WINDOW), lambda h: (h, 0, 0)),
                   pl.BlockSpec((1, n_s, n_w + n_s), lambda h: (h, 0, 0))],
        out_shape=[jax.ShapeDtypeStruct((n_h, Q_BLOCK, K_WINDOW), F32),
                   jax.ShapeDtypeStruct((n_h, n_s, n_w + n_s), F32)],
        compiler_params=_cparams("parallel"),
        name="rel_bias",
    )(diag)


def _att_step_kernel(q_ref, kn_ref, vn_ref, kc_ref, vc_ref, bc_ref, bn_ref, o_ref, *, n_h, scale):
    n_s = q_ref.shape[1]
    d = q_ref.shape[2]
    q = q_ref[0] * scale
    qt = jnp.concatenate([q] * n_h, axis=0)
    rh = lax.broadcasted_iota(jnp.int32, (n_h * n_s, d), 0) // n_s
    ch = lax.broadcasted_iota(jnp.int32, (n_h * n_s, d), 1) // HEAD_DIM
    qbd = jnp.where(rh == ch, qt, 0.0).astype(BF16)
    nt = (((1,), (1,)), ((), ()))
    s1 = lax.dot_general(qbd, kc_ref[0, 0].astype(BF16), nt, preferred_element_type=F32) + bc_ref[...]
    s2 = lax.dot_general(qbd, kn_ref[0].astype(BF16), nt, preferred_element_type=F32) + bn_ref[...]
    m = jnp.maximum(jnp.max(s1, axis=-1, keepdims=True), jnp.max(s2, axis=-1, keepdims=True))
    p1 = jnp.exp(s1 - m)
    p2 = jnp.exp(s2 - m)
    l = jnp.sum(p1, axis=-1, keepdims=True) + jnp.sum(p2, axis=-1, keepdims=True)
    o = (jnp.dot(p1.astype(BF16), vc_ref[0, 0].astype(BF16), preferred_element_type=F32)
         + jnp.dot(p2.astype(BF16), vn_ref[0].astype(BF16), preferred_element_type=F32)) / l
    first = lax.broadcasted_iota(jnp.int32, (1, LANES), 1) < HEAD_DIM
    for j in range(d // LANES):
        blk = o[:, j * LANES:(j + 1) * LANES]
        lo = blk[(2 * j) * n_s:(2 * j + 1) * n_s, :]
        hi = blk[(2 * j + 1) * n_s:(2 * j + 2) * n_s, :]
        o_ref[0, :, j * LANES:(j + 1) * LANES] = jnp.where(first, lo, hi).astype(o_ref.dtype)


def _att_step(proj, k_cache, v_cache, layer, bias_c, bias_n, q_col, k_col, v_col, n_h):
    n_b, n_s, _ = proj.shape
    n_w, d = k_cache.shape[2], k_cache.shape[3]
    spec = lambda c0: pl.BlockSpec((1, n_s, d), lambda b: (b, 0, c0))
    cspec = pl.BlockSpec((1, 1, n_w, d), lambda b: (layer, b, 0, 0))
    bspec = lambda a: pl.BlockSpec((None,) + a.shape[1:], lambda b: (layer, 0, 0))
    return pl.pallas_call(
        functools.partial(_att_step_kernel, n_h=n_h, scale=HEAD_DIM ** -0.5),
        grid=(n_b,),
        in_specs=[spec(q_col), spec(k_col), spec(v_col), cspec, cspec, bspec(bias_c), bspec(bias_n)],
        out_specs=pl.BlockSpec((1, n_s, d), lambda b: (b, 0, 0)),
        out_shape=jax.ShapeDtypeStruct((n_b, n_s, d), BF16),
        compiler_params=_cparams("parallel"),
        name="att_step",
    )(proj, proj, proj, k_cache, v_cache, bias_c, bias_n)


def _rwkv_prep_kernel(r_ref, k_ref, v_ref, u_ref, sh_ref, mu_ref, w0_ref, w1_ref, w2_ref, a0_ref, a1_ref, a2_ref,
                      kk_ref, ka_ref, nkk_out, w_out, b_out, k_out, r_out, v_out, prev):
    j = pl.program_id(1)

    @pl.when(j == 0)
    def _():
        prev[...] = sh_ref[0]

    n_r = r_ref.shape[1]
    row0 = lax.broadcasted_iota(jnp.int32, (SUBLANES, 1), 0) == 0

    def delta(x, i):
        sh = pltpu.roll(x, 1, axis=0)
        top = jnp.where(row0, prev[i:i + 1, :], sh[:SUBLANES])
        sh = jnp.concatenate([top, sh[SUBLANES:]], axis=0) if n_r > SUBLANES else top
        prev[i:i + 1, :] = x[n_r - 1:, :]
        return sh - x

    rp, kp, vp, up = (ref[0].astype(F32) for ref in (r_ref, k_ref, v_ref, u_ref))
    du = delta(up, 3)
    r = rp + delta(rp, 0) * mu_ref[0:1, :]
    k = kp + delta(kp, 1) * mu_ref[1:2, :]
    v = vp + delta(vp, 2) * mu_ref[2:3, :]
    xw = up + du * mu_ref[3:4, :]
    xa = up + du * mu_ref[4:5, :]
    dot = lambda a, b: jnp.dot(a.astype(BF16), b, preferred_element_type=F32)
    z = -(w0_ref[...] + dot(jnp.tanh(dot(xw, w1_ref[...])), w2_ref[...]))
    softplus = jnp.maximum(z, 0.0) + jnp.log(1.0 + jnp.exp(-jnp.abs(z)))
    w = -softplus - 0.5
    a = _sigmoid(a0_ref[...] + dot(dot(xa, a1_ref[...]), a2_ref[...]))
    kk = k * kk_ref[...]
    ss = jnp.dot((kk * kk).astype(BF16), _block_ones(kk.shape[1]), preferred_element_type=F32)
    kk = kk * lax.rsqrt(jnp.maximum(ss, 1e-24))
    nkk_out[0] = (-kk).astype(nkk_out.dtype)
    w_out[0] = -jnp.exp(w)
    b_out[0] = (kk * a).astype(b_out.dtype)
    k_out[0] = (k * (1.0 + (a - 1.0) * ka_ref[...])).astype(k_out.dtype)
    r_out[0] = r.astype(r_out.dtype)
    v_out[0] = v.astype(v_out.dtype)


def _layer_spec(a, layer):
    return pl.BlockSpec((None,) + a.shape[1:], lambda *_: (layer,) + (0,) * (a.ndim - 1),
                        pipeline_mode=pl.Buffered(1))


def _rwkv_prep(proj, shift0, lw, col0):
    n_b, n_l, _ = proj.shape
    d = shift0.shape[2]
    tr = min(n_l, PREP_ROWS)
    assert n_l % tr == 0
    spec = lambda c: pl.BlockSpec((1, tr, d), lambda b, j: (b, j, c))
    full = lambda a: _layer_spec(a, lw["layer"])
    ws = [lw["rwkv_mu"], lw["rwkv_w0"], lw["rwkv_w1"], lw["rwkv_w2"], lw["rwkv_a0"], lw["rwkv_a1"], lw["rwkv_a2"],
          lw["rwkv_k_k"], lw["rwkv_k_a"]]
    ospec = pl.BlockSpec((1, tr, d), lambda b, j: (b, j, 0))
    return pl.pallas_call(
        _rwkv_prep_kernel,
        grid=(n_b, n_l // tr),
        in_specs=[spec(col0), spec(col0 + 1), spec(col0 + 2), spec(col0 + 3),
                  pl.BlockSpec((1, 4, d), lambda b, j: (b, 0, 0))] + [full(a) for a in ws],
        out_specs=[ospec] * 6,
        out_shape=[jax.ShapeDtypeStruct((n_b, n_l, d), F32 if i == 1 else BF16) for i in range(6)],
        scratch_shapes=[pltpu.VMEM((4, d), F32)],
        compiler_params=_cparams("parallel", "arbitrary"),
        name="rwkv_prep",
    )(proj, proj, proj, proj, shift0, *ws)


def _rwkv_rec_kernel(nkk_ref, lw_ref, b_ref, k_ref, r_ref, v_ref, s0_ref, rk_ref, g_ref, beta_ref,
                     y_ref, s_out, state, *, nbk, nch, c):
    tb = pl.program_id(2)

    @pl.when(tb == 0)
    def _():
        state[...] = s0_ref[:, 0]

    c2 = 2 * c
    lane = lax.broadcasted_iota(jnp.int32, (1, LANES), 1)
    m0 = (lane < HEAD_DIM).astype(F32)
    m1 = 1.0 - m0
    by_head = lambda x: jnp.concatenate([x * m0, x * m1], axis=0)
    mm = lambda a, b: jnp.dot(a.astype(BF16), b.astype(BF16), preferred_element_type=F32)
    mm_nt = lambda a, b: lax.dot_general(a.astype(BF16), b.astype(BF16), (((1,), (1,)), ((), ())),
                                         preferred_element_type=F32)
    ii = lambda shape, d: lax.broadcasted_iota(jnp.int32, shape, d)
    tri = (ii((c, c), 0) >= ii((c, c), 1)).astype(BF16)
    rr, cc = ii((c2, c2), 0), ii((c2, c2), 1)
    strict = (rr // c == cc // c) & (cc % c < rr % c)
    incl = ii((c, c2), 1) % c <= ii((c, c2), 0)
    eye_t = (rr == cc).astype(F32)
    eye_s = (ii((LANES, LANES), 0) == ii((LANES, LANES), 1)).astype(F32)
    probs = [(bb, ci) for ci in range(nch) for bb in range(nbk)]
    rows = lambda ref, q: ref[q[0], q[1] * c:(q[1] + 1) * c, :].astype(F32)
    each = lambda fn, *lists: [fn(*xs) for xs in zip(*lists)]

    lw = [rows(lw_ref, q) for q in probs]

    def cumsum(x):
        hi = x.astype(BF16)
        lo = (x - hi.astype(F32)).astype(BF16)
        return jnp.dot(tri, hi, preferred_element_type=F32) + jnp.dot(tri, lo, preferred_element_type=F32)

    cum = each(cumsum, lw)
    tot = [x[c - 1:c, :] for x in cum]
    g_c = [jnp.exp(x) for x in tot]
    a_t = each(lambda q, x, l: rows(nkk_ref, q) * jnp.exp(x - l), probs, cum, lw)
    r_t = each(lambda q, x: rows(r_ref, q) * jnp.exp(x), probs, cum)
    ginv = [jnp.exp(-x) for x in cum]
    ghat = each(lambda x, t: jnp.exp(t - x), cum, tot)
    kx = [rows(k_ref, q) for q in probs]
    bx = [rows(b_ref, q) for q in probs]
    a_bd = [by_head(x) for x in a_t]
    lhs = each(lambda a, r: jnp.concatenate([a, r], axis=0), a_bd, r_t)
    if c2 % LANES == 0:
        sc = each(lambda l, k, b, gi: mm_nt(l, jnp.concatenate([by_head(k * gi), by_head(b * gi)], axis=0)),
                  lhs, kx, bx, ginv)
        sc_k, sc_b = [x[:, :c2] for x in sc], [x[:, c2:] for x in sc]
    else:
        sc_k = each(lambda l, k, gi: mm_nt(l, by_head(k * gi)), lhs, kx, ginv)
        sc_b = each(lambda l, b, gi: mm_nt(l, by_head(b * gi)), lhs, bx, ginv)
    l_ak = [jnp.where(strict, x[:c2], 0.0) for x in sc_k]
    l_ab = [jnp.where(strict, x[:c2], 0.0) for x in sc_b]
    m_rk = [jnp.where(incl, x[c2:], 0.0) for x in sc_k]
    m_rb = [jnp.where(incl, x[c2:], 0.0) for x in sc_b]
    col_t = ii((1, c2), 1)
    t0, t1 = (col_t < c).astype(F32), (col_t >= c).astype(F32)
    blockdiag = lambda x: jnp.concatenate([x * t0, x * t1], axis=0)
    pw = [x[:c] + x[c:] for x in l_ab]
    inv = [x + eye_t[:c] + eye_t[c:] for x in pw]
    for _ in range(int(math.log2(c)) - 1):
        pw = each(lambda x: mm(x, blockdiag(x)), pw)
        inv = each(lambda p, x: p + mm(p, blockdiag(x)), inv, pw)
    v_bd = [by_head(rows(v_ref, q)) for q in probs]
    x_in = each(lambda a, l, v: jnp.concatenate([a, mm(l, v)], axis=1), a_bd, l_ak, v_bd)
    m0w, m1w = jnp.concatenate([m0, m0], axis=1), jnp.concatenate([m1, m1], axis=1)
    uu = each(lambda t, x: (lambda u: jnp.concatenate([u * m0w, u * m1w], axis=0))(mm(t, x)), inv, x_in)
    bh_t = each(lambda b, gh: by_head(b * gh).T, bx, ghat)
    kh_t = each(lambda k, gh: by_head(k * gh).T, kx, ghat)
    bu = each(mm, bh_t, uu)
    g_mat = each(lambda gc, x: eye_s * gc + x[:, :LANES], g_c, bu)
    h_mat = each(lambda kt, v, x: mm(kt, v) + x[:, LANES:], kh_t, v_bd, bu)
    mu = each(mm, m_rb, uu)
    y_a = each(lambda r, x: r + x[:, :LANES], r_t, mu)
    y_b = each(lambda mk, v, x: mm(mk, v) + x[:, LANES:], m_rk, v_bd, mu)

    s_cur = [state[bb] for bb in range(nbk)]
    ys = {}
    for i, (bb, ci) in enumerate(probs):
        both = mm(jnp.concatenate([y_a[i], g_mat[i]], axis=0), s_cur[bb])
        ys[(bb, ci)] = both[:c] + y_b[i]
        s_cur[bb] = both[c:] + h_mat[i]
    for bb in range(nbk):
        state[bb] = s_cur[bb]

    @pl.when(tb == pl.num_programs(2) - 1)
    def _():
        s_out[:, 0] = state[...]

    ones = _block_ones(LANES)
    inv_n = 1.0 / HEAD_DIM
    head_sum = lambda x: jnp.dot(x.astype(BF16), ones, preferred_element_type=F32)
    for bb in range(nbk):
        y = jnp.concatenate([ys[(bb, ci)] for ci in range(nch)], axis=0) if nch > 1 else ys[(bb, 0)]
        mean = head_sum(y) * inv_n
        yc = y - mean
        var = head_sum(yc * yc) * inv_n
        yn = yc * lax.rsqrt(var + GN_EPS) * g_ref[...] + beta_ref[...]
        f32 = lambda ref: ref[bb].astype(F32)
        bonus = head_sum(f32(r_ref) * f32(k_ref) * rk_ref[...]) * f32(v_ref)
        y_ref[bb] = (yn + bonus).astype(y_ref.dtype)


def _rwkv_rec(prep, s0, layer, r_k, lnx_g, lnx_b):
    n_b, n_l, d = prep[0].shape
    n_pairs = d // LANES
    c = min(n_l, RWKV_CHUNK)
    nch = min(n_l // c, RWKV_GROUP // 2)
    nbk = min(n_b, RWKV_GROUP // nch)
    lb = nch * c
    assert n_l % lb == 0 and n_b % nbk == 0 and c & (c - 1) == 0
    spec = pl.BlockSpec((nbk, lb, LANES), lambda g, p, t: (g, t, p))
    sspec = pl.BlockSpec((nbk, 1, LANES, LANES), lambda g, p, t: (g, p, 0, 0))
    vec = pl.BlockSpec((None, 1, LANES), lambda g, p, t: (layer, 0, p))
    return pl.pallas_call(
        functools.partial(_rwkv_rec_kernel, nbk=nbk, nch=nch, c=c),
        grid=(n_b // nbk, n_pairs, n_l // lb),
        in_specs=[spec] * 6 + [sspec, vec, vec, vec],
        out_specs=[spec, sspec],
        out_shape=[jax.ShapeDtypeStruct((n_b, n_l, d), BF16),
                   jax.ShapeDtypeStruct((n_b, n_pairs, LANES, LANES), F32)],
        scratch_shapes=[pltpu.VMEM((nbk, LANES, LANES), F32)],
        compiler_params=_cparams("parallel", "parallel", "arbitrary"),
        name="rwkv_rec",
    )(*prep, s0, r_k, lnx_g, lnx_b)


def _pack_pairs(s):
    n_b, n_h, n, _ = s.shape
    st = jnp.swapaxes(s, -1, -2).reshape(n_b, n_h // 2, 2, n, n)
    z = jnp.zeros_like(st[:, :, 0])
    return jnp.concatenate([jnp.concatenate([st[:, :, 0], z], axis=-1),
                            jnp.concatenate([z, st[:, :, 1]], axis=-1)], axis=-2)


def _unpack_pairs(s):
    n_b, n_p, n2, _ = s.shape
    n = n2 // 2
    st = jnp.stack([s[:, :, :n, :n], s[:, :, n:, n:]], axis=2)
    return jnp.swapaxes(st, -1, -2).reshape(n_b, 2 * n_p, n, n)


N_MIX_IN = 7


def _mix_kernel(*refs, alpha, d_ssm, d_att, n_out):
    main, side = refs[:N_MIX_IN], refs[N_MIX_IN:2 * N_MIX_IN]
    wg_ref, bg_ref, wo_ref, lg_ref, lb_ref = refs[2 * N_MIX_IN:2 * N_MIX_IN + 5]
    outs = refs[2 * N_MIX_IN + 5:]
    silu = lambda g: g * _sigmoid(g)
    dot = lambda a, b: jnp.dot(a.astype(BF16), b, preferred_element_type=F32)

    def mix_rows(ins, out_refs):
        x_ref, ys_ref, gs_ref, ya_ref, ga_ref, yc_ref, gc_ref = ins
        n_r = x_ref.shape[0]
        n_part = 2 if n_r % 32 == 0 else 1
        for part in range(n_part):
            rows = slice(part * n_r // n_part, (part + 1) * n_r // n_part)
            ys = ys_ref[rows, :]
            z = 0.5 * ys * (1.0 + jnp.tanh(math.sqrt(2.0 / math.pi) * (ys + 0.044715 * (ys * ys * ys))))
            gate = lambda ref: silu(ref[rows, :].astype(F32))
            m_s = z * _sigmoid(dot(z, wg_ref[...]) + bg_ref[...]) * gate(gs_ref)
            m_a = ya_ref[rows, :] * gate(ga_ref)
            m_c = yc_ref[rows, :] * gate(gc_ref)
            out = (dot(m_s, wo_ref[0:d_ssm, :]) + dot(m_a, wo_ref[d_ssm:d_ssm + d_att, :])
                   + dot(m_c, wo_ref[d_ssm + d_att:, :]))
            h = alpha * x_ref[rows, :] + out
            mu = jnp.mean(h, axis=-1, keepdims=True)
            hc = h - mu
            var = jnp.mean(hc * hc, axis=-1, keepdims=True)
            y = hc * lax.rsqrt(var + LN_EPS) * lg_ref[...] + lb_ref[...]
            out_refs[0][rows, :] = y
            for yb_ref in out_refs[1:]:
                yb_ref[rows, :] = y.astype(BF16)

    mix_rows(main, outs[:n_out])

    @pl.when(pl.program_id(0) == pl.num_programs(0) - 1)
    def _():
        mix_rows(side, outs[n_out:])


def _mix(main, side, lw, alpha, gs_col, ga_col, gc_col, want_bf16):
    x, proj, ys, ya, yc = main
    m, d = x.shape
    ms = side[0].shape[0]
    d_ssm, d_att, d_rw = ys.shape[1], ya.shape[1], yc.shape[1]
    tm = min(m, MIX_ROWS)
    assert m % tm == 0
    row = lambda w, c: pl.BlockSpec((tm, w), lambda i: (i, c))
    fixed = lambda w, c: pl.BlockSpec((ms, w), lambda i: (0, c))
    specs = lambda blk: [blk(d, 0), blk(d_ssm, 0), blk(d_ssm, gs_col), blk(d_att, 0), blk(d_att, ga_col),
                         blk(d_rw, 0), blk(d_rw, gc_col)]
    operands = lambda t: (t[0], t[2], t[1], t[3], t[1], t[4], t[1])
    full = lambda a: _layer_spec(a, lw["layer"])
    ws = [lw["w_glu"], lw["b_glu"], lw["w_out"], lw["ln_g"], lw["ln_b"]]
    n_out = 2 if want_bf16 else 1
    shapes = lambda rows: [jax.ShapeDtypeStruct((rows, d), F32)] + [jax.ShapeDtypeStruct((rows, d), BF16)] * want_bf16
    return pl.pallas_call(
        functools.partial(_mix_kernel, alpha=alpha, d_ssm=d_ssm, d_att=d_att, n_out=n_out),
        grid=(m // tm,),
        in_specs=specs(row) + specs(fixed) + [full(a) for a in ws],
        out_specs=[row(d, 0)] * n_out + [fixed(d, 0)] * n_out,
        out_shape=shapes(m) + shapes(ms),
        compiler_params=_cparams("arbitrary"),
        name="mix_out",
    )(*operands(main), *operands(side), *ws)


def _layer(x, proj, st, lw, s5p):
    n_b, n_l, d_model = x.shape
    d_ssm = lw["w_glu"].shape[1]
    n_h = lw["n_att_heads"]
    d_att = n_h * HEAD_DIM
    d_rw = lw["rwkv_w0"].shape[2]
    n_rh = d_rw // HEAD_DIM
    n_blk = d_ssm // LANES
    proj = proj.reshape(n_b, n_l, -1)
    q0 = 2 * d_ssm
    r0 = q0 + 4 * d_att
    assert q0 % d_att == 0 and r0 % d_rw == 0 and d_ssm == d_rw

    if st is None:
        zeros = jnp.zeros((n_b, n_blk, 1, S5_BLOCK), F32)
        h0r = h0i = zeros
        wkv0 = jnp.zeros((n_b, n_rh // 2, LANES, LANES), F32)
        shift0 = jnp.zeros((n_b, 4, d_rw), F32)
    else:
        k_cache, v_cache, h0r, h0i, wkv0, shift0 = st
        h0r = h0r.reshape(n_b, n_blk, 1, S5_BLOCK)
        h0i = h0i.reshape(n_b, n_blk, 1, S5_BLOCK)
        wkv0 = _pack_pairs(wkv0)
        shift0 = shift0.reshape(n_b, 4, d_rw)

    ys, h_re, h_im = _s5(proj, s5p, lw["layer"], h0r, h0i)

    n_keep = min(ATT_REACH, n_l)
    keep = lambda c0: proj[:, n_l - n_keep:, c0:c0 + d_att].astype(F32).reshape(n_b, n_keep, n_h, HEAD_DIM)
    k_rows, v_rows = keep(q0 + d_att), keep(q0 + 2 * d_att)
    if st is None:
        pl0 = q0 // LANES
        ya = _att_prompt(proj, lw["bias_prompt"], lw["layer"], pl0, pl0 + d_att // LANES, pl0 + 2 * d_att // LANES,
                         n_h // 2)
    else:
        c0 = q0 // d_att
        ya = _att_step(proj, k_cache, v_cache, lw["layer"], lw["bias_cache"], lw["bias_new"], c0, c0 + 1, c0 + 2, n_h)

    prep = _rwkv_prep(proj, shift0, lw, r0 // d_rw)
    yc, wkv = _rwkv_rec(prep, wkv0, lw["layer"], lw["rwkv_r_k"], lw["rwkv_lnx_g"], lw["rwkv_lnx_b"])
    shift = proj[:, n_l - 1, r0:r0 + 4 * d_rw].astype(F32)

    m = n_b * n_l
    mix_in = (x.reshape(m, d_model), proj.reshape(m, -1), ys.reshape(m, d_ssm), ya.reshape(m, d_att),
              yc.reshape(m, d_rw))
    gate_cols = (1, (q0 + 3 * d_att) // d_att, (r0 + 4 * d_rw) // d_rw)
    n_g = d_ssm // SSM_GROUP
    states = (k_rows, v_rows, h_re.reshape(n_b, n_g, SSM_STATE), h_im.reshape(n_b, n_g, SSM_STATE),
              _unpack_pairs(wkv), shift)
    return mix_in, gate_cols, states


def kernel(x_prompt, x_sample, cache_att_k, cache_att_v, state_ssm_re, state_ssm_im, state_rwkv, state_rwkv_shift, w_in, ssm_lam_re, ssm_lam_im, ssm_log_dt, ssm_b_re, ssm_b_im, ssm_c_re, ssm_c_im, ssm_d, ssm_w_glu, ssm_b_glu, att_rel_bias, rwkv_mu, rwkv_w0, rwkv_w1, rwkv_w2, rwkv_a0, rwkv_a1, rwkv_a2, rwkv_k_k, rwkv_k_a, rwkv_r_k, rwkv_lnx_g, rwkv_lnx_b, w_out, ln_g, ln_b):
    depth = w_in.shape[0]
    alpha = (2.0 * depth) ** 0.25
    y_p, y_s = x_prompt, x_sample
    yb_p = yb_s = None
    seg_lens = sorted({x_prompt.shape[1] // S5_SEGMENTS, x_sample.shape[1] // S5_SEGMENTS})
    p_st, s_st = [], []
    n_d, n_sb, n_w = cache_att_k.shape[:3]
    n_s = x_sample.shape[1]
    n_h = att_rel_bias.shape[1]
    k_cache = cache_att_k.reshape(n_d, n_sb, n_w, -1)
    v_cache = cache_att_v.reshape(n_d, n_sb, n_w, -1)
    row = lambda a: a.reshape(depth, 1, -1)
    bias_p, bias_s = _rel_bias(att_rel_bias.reshape(depth * n_h, -1), n_s, n_w)
    bias_s = bias_s.reshape(depth, n_h * n_s, n_w + n_s)
    params = {"w_in": w_in, "n_att_heads": n_h, "w_glu": ssm_w_glu.astype(BF16), "b_glu": row(ssm_b_glu),
              "rwkv_mu": rwkv_mu, "rwkv_w0": row(rwkv_w0), "rwkv_w1": rwkv_w1.astype(BF16),
              "rwkv_w2": rwkv_w2.astype(BF16), "rwkv_a0": row(rwkv_a0), "rwkv_a1": rwkv_a1.astype(BF16),
              "rwkv_a2": rwkv_a2.astype(BF16), "rwkv_k_k": row(rwkv_k_k), "rwkv_k_a": row(rwkv_k_a),
              "rwkv_r_k": row(rwkv_r_k), "rwkv_lnx_g": row(rwkv_lnx_g), "rwkv_lnx_b": row(rwkv_lnx_b),
              "w_out": w_out.astype(BF16), "ln_g": row(ln_g), "ln_b": row(ln_b),
              "bias_prompt": bias_p.reshape(depth * n_h // 2, 2, Q_BLOCK, K_WINDOW),
              "bias_cache": bias_s[:, :, :n_w], "bias_new": bias_s[:, :, n_w:]}
    s5p = _s5_params(ssm_lam_re, ssm_lam_im, ssm_log_dt, ssm_b_re, ssm_b_im, ssm_c_re, ssm_c_im, ssm_d, seg_lens)
    for l in range(depth):
        lw = dict(params, layer=l)
        if yb_p is None:
            flat = lambda a: a.reshape(-1, a.shape[-1])
            yb_p, yb_s = _to_bf16(flat(y_p)), _to_bf16(flat(y_s))
        proj_p, proj_s = _in_proj(yb_p, yb_s, w_in, l)
        mix_p, gate_cols, st_p = _layer(y_p, proj_p, None, lw, s5p)
        mix_s, _, st_s = _layer(y_s, proj_s, (k_cache, v_cache, state_ssm_re[l], state_ssm_im[l],
                                              state_rwkv[l], state_rwkv_shift[l]), lw, s5p)
        last = l == depth - 1
        outs = _mix(mix_p, mix_s, lw, alpha, *gate_cols, want_bf16=not last)
        (y_p, y_s), (yb_p, yb_s) = (outs if last else outs[0::2]), ((None, None) if last else outs[1::2])
        y_p, y_s = y_p.reshape(x_prompt.shape), y_s.reshape(x_sample.shape)
        p_st.append(st_p)
        s_st.append(st_s)
    stacked = lambda states, i: jnp.stack([st[i] for st in states], axis=0)
    return (y_p, y_s) + tuple(stacked(p_st, i) for i in range(6)) + tuple(stacked(s_st, i) for i in range(6))
```
